```python
import jax, jax.numpy as jnp
from jax import lax
import numpy as np

D_MODEL = 1024
BATCH = 2
SEQ = 8192
DEPTH = 1

D_MIX = D_MODEL
D_ATTN = D_MIX // 2
D_CONV = D_MIX - D_ATTN
N_HEADS = 8
HEAD_DIM = D_ATTN // N_HEADS
CONV_WIDTH = 3
MOBA_BLOCK = 256
MOBA_TOPK = 3
Q_CHUNK = 64
ROPE_THETA = 10000.0
EPS = 1e-6
NEG = -1e30

kernel_name = "hymba_moba_shortconv_adaln_block"


def rmsnorm(x, g):
    xf = x.astype(jnp.float32)
    y = xf * lax.rsqrt(jnp.mean(xf * xf, axis=-1, keepdims=True) + EPS)
    return (y * g.astype(jnp.float32)).astype(x.dtype)


def rope(t, positions):
    half = HEAD_DIM // 2
    inv_freq = ROPE_THETA ** (-jnp.arange(half, dtype=jnp.float32) / half)
    ang = positions.astype(jnp.float32)[..., None] * inv_freq
    cos = jnp.cos(ang)[:, :, None, :]
    sin = jnp.sin(ang)[:, :, None, :]
    t1 = t[..., :half].astype(jnp.float32)
    t2 = t[..., half:].astype(jnp.float32)
    out = jnp.concatenate([t1 * cos - t2 * sin, t2 * cos + t1 * sin], axis=-1)
    return out.astype(t.dtype)


def moba_attention(q, k, v):
    B, H, S, Dh = q.shape
    nb = -(-S // MOBA_BLOCK)
    s_pad = nb * MOBA_BLOCK
    pad = [(0, 0), (0, 0), (0, s_pad - S), (0, 0)]
    q = jnp.pad(q, pad)
    k = jnp.pad(k, pad)
    v = jnp.pad(v, pad)
    kb = k.reshape(B, H, nb, MOBA_BLOCK, Dh)
    vb = v.reshape(B, H, nb, MOBA_BLOCK, Dh)
    scale = Dh ** -0.5
    n_sel = min(MOBA_TOPK, nb - 1)
    k_mean = jnp.mean(kb.astype(jnp.float32), axis=3)
    n_chunks = s_pad // Q_CHUNK
    q_chunks = q.reshape(B, H, n_chunks, Q_CHUNK, Dh).transpose(2, 0, 1, 3, 4)
    b_ix = jnp.arange(B)[:, None, None]
    h_ix = jnp.arange(H)[None, :, None]
    blk_ids = jnp.arange(nb)

    def chunk(args):
        ci, qc = args
        q_pos = ci * Q_CHUNK + jnp.arange(Q_CHUNK)
        own = (ci * Q_CHUNK) // MOBA_BLOCK
        k_own = lax.dynamic_index_in_dim(kb, own, axis=2, keepdims=False)
        v_own = lax.dynamic_index_in_dim(vb, own, axis=2, keepdims=False)
        k_pos = own * MOBA_BLOCK + jnp.arange(MOBA_BLOCK)
        logit_own = jnp.einsum('bhqd,bhld->bhql', qc, k_own,
                               preferred_element_type=jnp.float32) * scale
        logit_own = jnp.where(k_pos[None, :] <= q_pos[:, None], logit_own, NEG)
        if n_sel == 0:
            p_own = jax.nn.softmax(logit_own, axis=-1).astype(v.dtype)
            return jnp.einsum('bhql,bhld->bhqd', p_own, v_own)
        gate = jnp.einsum('bhqd,bhnd->bhqn', qc.astype(jnp.float32), k_mean)
        gate = jnp.where(blk_ids < own, gate, NEG)
        _, sel = lax.top_k(gate, n_sel)
        valid = sel < own
        sel_flat = sel.reshape(B, H, Q_CHUNK * n_sel)
        k_sel = kb[b_ix, h_ix, sel_flat].reshape(B, H, Q_CHUNK, n_sel, MOBA_BLOCK, Dh)
        v_sel = vb[b_ix, h_ix, sel_flat].reshape(B, H, Q_CHUNK, n_sel, MOBA_BLOCK, Dh)
        logit_past = jnp.einsum('bhqd,bhqnld->bhqnl', qc, k_sel,
                                preferred_element_type=jnp.float32) * scale
        logit_past = jnp.where(valid[..., None], logit_past, NEG)
        logit_past = logit_past.reshape(B, H, Q_CHUNK, n_sel * MOBA_BLOCK)
        logits = jnp.concatenate([logit_past, logit_own], axis=-1)
        p = jax.nn.softmax(logits, axis=-1).astype(v.dtype)
        p_past = p[..., :n_sel * MOBA_BLOCK].reshape(B, H, Q_CHUNK, n_sel, MOBA_BLOCK)
        p_own = p[..., n_sel * MOBA_BLOCK:]
        return (jnp.einsum('bhqnl,bhqnld->bhqd', p_past, v_sel)
                + jnp.einsum('bhql,bhld->bhqd', p_own, v_own))

    out = lax.map(chunk, (jnp.arange(n_chunks), q_chunks))
    out = out.transpose(1, 2, 0, 3, 4).reshape(B, H, s_pad, Dh)
    return out[:, :, :S]


def short_gated_conv(b_gate, c_gate, h, w_conv):
    u = c_gate * h
    y = lax.conv_general_dilated(
        u, w_conv[:, None, :], window_strides=(1,), padding=[(CONV_WIDTH - 1, 0)],
        dimension_numbers=('NWC', 'WIO', 'NWC'), feature_group_count=u.shape[-1])
    return b_gate * y


def setup_inputs(seed: int = 0) -> dict:
    key = jax.random.key(seed)
    ks = jax.random.split(key, 13)
    d_in_cols = 4 * D_ATTN + 4 * D_CONV
    x = jax.random.normal(ks[0], (BATCH, SEQ, D_MODEL), jnp.float32)
    c = jax.random.normal(ks[1], (BATCH, D_MODEL), jnp.float32)
    offsets = jax.random.randint(ks[2], (BATCH, 1), 0, 1024, dtype=jnp.int32)
    positions = jnp.arange(SEQ, dtype=jnp.int32)[None, :] + offsets
    w_ada = jax.random.normal(ks[3], (D_MODEL, 3 * D_MODEL), jnp.float32) * (0.5 * D_MODEL ** -0.5)
    b_ada = 0.01 * jax.random.normal(ks[4], (3 * D_MODEL,), jnp.float32)
    g_norm = 1.0 + 0.02 * jax.random.normal(ks[5], (D_MODEL,), jnp.float32)
    w_in = jax.random.normal(ks[6], (D_MODEL, d_in_cols), jnp.float32) * D_MODEL ** -0.5
    w_conv = jax.random.normal(ks[7], (CONV_WIDTH, D_CONV), jnp.float32) * CONV_WIDTH ** -0.5
    g_attn_out = 1.0 + 0.02 * jax.random.normal(ks[8], (D_ATTN,), jnp.float32)
    g_conv_out = 1.0 + 0.02 * jax.random.normal(ks[9], (D_CONV,), jnp.float32)
    w_out = jax.random.normal(ks[10], (D_MIX, D_MODEL), jnp.float32) * D_MIX ** -0.5
    g_final = 1.0 + 0.02 * jax.random.normal(ks[11], (D_MODEL,), jnp.float32)
    return {"x": x, "c": c, "positions": positions, "w_ada": w_ada, "b_ada": b_ada,
            "g_norm": g_norm, "w_in": w_in, "w_conv": w_conv, "g_attn_out": g_attn_out,
            "g_conv_out": g_conv_out, "w_out": w_out, "g_final": g_final}


def reference(x, c, positions, w_ada, b_ada, g_norm, w_in, w_conv, g_attn_out,
              g_conv_out, w_out, g_final):
    B, S, _ = x.shape
    splits = [D_ATTN, 2 * D_ATTN, 3 * D_ATTN, 4 * D_ATTN,
              4 * D_ATTN + D_CONV, 4 * D_ATTN + 2 * D_CONV, 4 * D_ATTN + 3 * D_CONV]
    for _ in range(DEPTH):
        mod = jax.nn.silu(c) @ w_ada + b_ada
        shift, scale, gate = jnp.split(mod, 3, axis=-1)
        h = rmsnorm(x, g_norm) * (1.0 + scale[:, None, :]) + shift[:, None, :]
        proj = h @ w_in
        q, k, v, z_attn, b_g, c_g, h_c, z_conv = jnp.split(proj, splits, axis=-1)
        q = rope(q.reshape(B, S, N_HEADS, HEAD_DIM), positions).transpose(0, 2, 1, 3)
        k = rope(k.reshape(B, S, N_HEADS, HEAD_DIM), positions).transpose(0, 2, 1, 3)
        v = v.reshape(B, S, N_HEADS, HEAD_DIM).transpose(0, 2, 1, 3)
        y_attn = moba_attention(q, k, v).transpose(0, 2, 1, 3).reshape(B, S, D_ATTN)
        y_attn = rmsnorm(y_attn, g_attn_out) * jax.nn.silu(z_attn)
        y_conv = short_gated_conv(b_g, c_g, h_c, w_conv)
        y_conv = rmsnorm(y_conv, g_conv_out) * jax.nn.silu(z_conv)
        y = jnp.concatenate([y_attn, y_conv], axis=-1) @ w_out
        x = x + gate[:, None, :] * y
    return rmsnorm(x, g_final)
```

```python
import functools

import jax
import jax.numpy as jnp
from jax import lax
from jax.experimental import pallas as pl
from jax.experimental.pallas import tpu as pltpu

D_MODEL = 1024
D_ATTN = 512
D_CONV = 512
N_HEADS = 8
HEAD_DIM = 64
HALF = HEAD_DIM // 2
CONV_WIDTH = 3
MOBA_BLOCK = 256
MOBA_TOPK = 3
ROPE_THETA = 10000.0
EPS = 1e-6
NEG = -1e30

LANES = 128
TILE = MOBA_BLOCK
OUT_TILE = 512
VMEM_LIMIT = 56 * 1024 * 1024

F32 = jnp.float32
BF16 = jnp.bfloat16


def _silu(z):
    return z * (1.0 / (1.0 + jnp.exp(-z)))


def _split_bf16(a):
    hi = a.astype(BF16)
    lo = (a - hi.astype(F32)).astype(BF16)
    return hi, lo


def _adaln_kernel(c_ref, w_ref, b_ref, o_ref):
    o_ref[...] = jnp.dot(_silu(c_ref[...]), w_ref[...], precision=lax.Precision.HIGHEST,
                         preferred_element_type=F32) + b_ref[...]


def _adaln(c, w_ada, b_ada):
    bsz = c.shape[0]
    n = w_ada.shape[1]
    bn = D_MODEL
    return pl.pallas_call(
        _adaln_kernel,
        grid=(n // bn,),
        in_specs=[pl.BlockSpec((bsz, D_MODEL), lambda j: (0, 0)),
                  pl.BlockSpec((D_MODEL, bn), lambda j: (0, j)),
                  pl.BlockSpec((1, bn), lambda j: (0, j))],
        out_specs=pl.BlockSpec((bsz, bn), lambda j: (0, j)),
        out_shape=jax.ShapeDtypeStruct((bsz, n), F32),
        compiler_params=pltpu.CompilerParams(dimension_semantics=("arbitrary",),
                                             vmem_limit_bytes=VMEM_LIMIT),
        name="adaln",
    )(c, w_ada, b_ada.reshape(1, n))


def _in_proj_kernel(n_blocks, x_ref, mod_ref, gnorm_ref, pos_ref, freq_ref, whi_ref, wlo_ref,
                    wconv_ref, gconv_ref,
                    q_ref, k_ref, v_ref, gz_ref, yc_ref,
                    kbd_ref, carry_ref):
    i = pl.program_id(1)

    @pl.when(i == 0)
    def _():
        kbd_ref[...] = jnp.zeros_like(kbd_ref)
        carry_ref[...] = jnp.zeros_like(carry_ref)

    x = x_ref[0]
    xn = x * lax.rsqrt(jnp.mean(x * x, axis=-1, keepdims=True) + EPS) * gnorm_ref[...]
    h = xn * (1.0 + mod_ref[0, 1:2, :]) + mod_ref[0, 0:1, :]
    h_hi, h_lo = _split_bf16(h)

    def proj(c0, c1):
        return jnp.dot(h_hi, whi_ref[:, c0:c1], preferred_element_type=F32)

    qk = (proj(0, 2 * D_ATTN)
          + jnp.dot(h_lo, whi_ref[:, 0:2 * D_ATTN], preferred_element_type=F32)
          + jnp.dot(h_hi, wlo_ref[...], preferred_element_type=F32))

    lane = lax.broadcasted_iota(jnp.int32, (TILE, LANES), 1)
    first_half = (lane % HEAD_DIM) < HALF
    ang = pos_ref[0].astype(F32) * freq_ref[...]
    cos = jnp.cos(ang)
    sin = jnp.sin(ang)
    sin = jnp.where(first_half, -sin, sin)

    def rope(t):
        partner = jnp.where(first_half, pltpu.roll(t, LANES - HALF, 1), pltpu.roll(t, HALF, 1))
        return t * cos + partner * sin

    n_groups = D_ATTN // LANES
    q_groups = [rope(qk[:, g * LANES:(g + 1) * LANES]) for g in range(n_groups)]
    k_groups = [rope(qk[:, D_ATTN + g * LANES:D_ATTN + (g + 1) * LANES]) for g in range(n_groups)]
    q = jnp.concatenate(q_groups, axis=1)
    k = jnp.concatenate(k_groups, axis=1)

    q_hi, q_lo = _split_bf16(q)
    kbd_hi, kbd_lo = _split_bf16(kbd_ref[...])
    nt = (((1,), (1,)), ((), ()))
    scores = (lax.dot_general(kbd_hi, q_hi, nt, preferred_element_type=F32)
              + lax.dot_general(kbd_lo, q_hi, nt, preferred_element_type=F32)
              + lax.dot_general(kbd_hi, q_lo, nt, preferred_element_type=F32))
    n_slots = scores.shape[0] // n_blocks
    g3 = scores.reshape(n_slots, n_blocks, TILE)
    blk = lax.broadcasted_iota(jnp.int32, g3.shape, 1)
    slot = lax.broadcasted_iota(jnp.int32, g3.shape, 0)
    past = blk < i
    g3 = jnp.where(past, g3, NEG)
    picked = jnp.zeros(g3.shape, jnp.bool_)
    for _ in range(min(MOBA_TOPK, n_blocks - 1)):
        top = jnp.max(g3, axis=1, keepdims=True)
        first = jnp.min(jnp.where(g3 == top, blk, n_blocks), axis=1, keepdims=True)
        hit = blk == first
        picked = jnp.logical_or(picked, hit)
        g3 = jnp.where(hit, -jnp.inf, g3)
    keep = jnp.logical_or(jnp.logical_and(picked, past), blk == i)
    bias = jnp.where(jnp.logical_or(keep, slot % 2 == 1), 0.0, NEG)
    bias_t = bias.reshape(scores.shape).T

    k_mean = jnp.mean(k, axis=0, keepdims=True)
    lane_w = lax.broadcasted_iota(jnp.int32, (1, D_ATTN), 1)
    for hd in range(N_HEADS):
        row = (4 * (hd // 2) + (2 if hd % 2 == 0 else 0)) * n_blocks + i
        own = (lane_w // HEAD_DIM) == hd
        kbd_ref[pl.ds(row, 1), :] = jnp.where(own, k_mean, 0.0)

    v = proj(2 * D_ATTN, 3 * D_ATTN)
    low = lane < HEAD_DIM
    onehot_even = jnp.where(lane == HEAD_DIM + i, 1.0, 0.0)
    onehot_odd = jnp.where(lane == i, 1.0, 0.0)
    scale = HEAD_DIM ** -0.5
    for g in range(n_groups):
        sl = slice(g * LANES, (g + 1) * LANES)
        qg, kg, vg, bg_t = q_groups[g] * scale, k_groups[g], v[:, sl], bias_t[:, sl]
        even = slice(2 * g * LANES, (2 * g + 1) * LANES)
        odd = slice((2 * g + 1) * LANES, (2 * g + 2) * LANES)
        q_ref[0, :, even] = jnp.where(low, qg, bg_t).astype(BF16)
        q_ref[0, :, odd] = jnp.where(low, bg_t, qg).astype(BF16)
        k_ref[0, :, even] = jnp.where(low, kg, onehot_even).astype(BF16)
        k_ref[0, :, odd] = jnp.where(low, onehot_odd, kg).astype(BF16)
        v_ref[0, :, even] = jnp.where(low, vg, 1.0).astype(BF16)
        v_ref[0, :, odd] = jnp.where(low, 1.0, vg).astype(BF16)

    gz_ref[0] = _silu(proj(3 * D_ATTN, 4 * D_ATTN)).astype(BF16)

    c0 = 4 * D_ATTN
    b_g = proj(c0, c0 + D_CONV)
    u = proj(c0 + D_CONV, c0 + 2 * D_CONV) * proj(c0 + 2 * D_CONV, c0 + 3 * D_CONV)
    z_c = proj(c0 + 3 * D_CONV, c0 + 4 * D_CONV)
    rows = lax.broadcasted_iota(jnp.int32, u.shape, 0)
    prev1 = carry_ref[7:8, :]
    prev2 = carry_ref[6:7, :]
    u1 = jnp.where(rows == 0, prev1, pltpu.roll(u, 1, 0))
    u2 = jnp.where(rows == 0, prev2, jnp.where(rows == 1, prev1, pltpu.roll(u, 2, 0)))
    carry_ref[...] = u[TILE - 8:TILE, :]
    y = b_g * (wconv_ref[2:3, :] * u + wconv_ref[1:2, :] * u1 + wconv_ref[0:1, :] * u2)
    y = y * lax.rsqrt(jnp.mean(y * y, axis=-1, keepdims=True) + EPS) * gconv_ref[...]
    yc_ref[0] = (y * _silu(z_c)).astype(BF16)


def _in_proj(x, mod3, g_norm, positions, inv_freq, w_hi, w_lo, w_conv, g_conv_out):
    bsz, seq, _ = x.shape
    n_blocks = seq // TILE
    wide = N_HEADS * LANES
    row_spec = lambda width: pl.BlockSpec((1, TILE, width), lambda b, i: (b, i, 0))
    const = lambda shape: pl.BlockSpec(shape, lambda b, i: (0,) * len(shape))
    kbd_rows = 2 * N_HEADS * n_blocks
    return pl.pallas_call(
        functools.partial(_in_proj_kernel, n_blocks),
        grid=(bsz, n_blocks),
        in_specs=[row_spec(D_MODEL),
                  pl.BlockSpec((1, 3, D_MODEL), lambda b, i: (b, 0, 0)),
                  const((1, D_MODEL)),
                  row_spec(1),
                  const((1, LANES)),
                  const(w_hi.shape),
                  const(w_lo.shape),
                  const((CONV_WIDTH, D_CONV)),
                  const((1, D_CONV))],
        out_specs=[row_spec(wide), row_spec(wide), row_spec(wide),
                   row_spec(D_ATTN), row_spec(D_CONV)],
        out_shape=[jax.ShapeDtypeStruct((bsz, seq, wide), BF16)] * 3
        + [jax.ShapeDtypeStruct((bsz, seq, D_ATTN), BF16),
           jax.ShapeDtypeStruct((bsz, seq, D_CONV), BF16)],
        scratch_shapes=[pltpu.VMEM((kbd_rows, D_ATTN), F32),
                        pltpu.VMEM((8, D_CONV), F32)],
        compiler_params=pltpu.CompilerParams(dimension_semantics=("arbitrary", "arbitrary"),
                                             vmem_limit_bytes=VMEM_LIMIT),
        name="in_proj",
    )(x, mod3, g_norm.reshape(1, D_MODEL), positions.reshape(bsz, seq, 1), inv_freq,
      w_hi, w_lo, w_conv, g_conv_out.reshape(1, D_CONV))


def _moba_kernel(q_ref, k_ref, v_ref, o_ref):
    i = pl.program_id(2)
    nt = (((1,), (1,)), ((), ()))
    heads = (slice(0, LANES), slice(LANES, 2 * LANES))
    q = [q_ref[0, :, hs] for hs in heads]

    def scores(hd, start):
        kj = k_ref[0, pl.ds(start, TILE), heads[hd]]
        return lax.dot_general(q[hd], kj, nt, preferred_element_type=F32)

    def pv(hd, p, start):
        vj = v_ref[0, pl.ds(start, TILE), heads[hd]]
        return jnp.dot(p.astype(BF16), vj, preferred_element_type=F32)

    own = pl.multiple_of(i * TILE, TILE)
    r = lax.broadcasted_iota(jnp.int32, (TILE, TILE), 0)
    c = lax.broadcasted_iota(jnp.int32, (TILE, TILE), 1)
    init = []
    for hd in range(2):
        s = jnp.where(c <= r, scores(hd, own), NEG)
        m = jnp.max(s, axis=1, keepdims=True)
        init += [m, pv(hd, jnp.exp(s - m), own)]

    def body(j, carry):
        start = pl.multiple_of(j * TILE, TILE)
        out = []
        for hd in range(2):
            m, acc = carry[2 * hd], carry[2 * hd + 1]
            s = scores(hd, start)
            m_new = jnp.maximum(m, jnp.max(s, axis=1, keepdims=True))
            acc = jnp.exp(m - m_new) * acc + pv(hd, jnp.exp(s - m_new), start)
            out += [m_new, acc]
        return tuple(out)

    fin = lax.fori_loop(0, i, body, tuple(init))
    outs = [fin[2 * hd + 1] / pltpu.roll(fin[2 * hd + 1], HEAD_DIM, 1) for hd in range(2)]
    lane = lax.broadcasted_iota(jnp.int32, (TILE, LANES), 1)
    o_ref[0] = jnp.where(lane < HEAD_DIM, outs[0], outs[1])


def _moba(q_aug, k_aug, v_aug):
    bsz, seq, wide = q_aug.shape
    pairs = wide // (2 * LANES)
    return pl.pallas_call(
        _moba_kernel,
        grid=(bsz, pairs, seq // TILE),
        in_specs=[pl.BlockSpec((1, TILE, 2 * LANES), lambda b, p, i: (b, i, p)),
                  pl.BlockSpec((1, seq, 2 * LANES), lambda b, p, i: (b, 0, p)),
                  pl.BlockSpec((1, seq, 2 * LANES), lambda b, p, i: (b, 0, p))],
        out_specs=pl.BlockSpec((1, TILE, LANES), lambda b, p, i: (b, i, p)),
        out_shape=jax.ShapeDtypeStruct((bsz, seq, pairs * LANES), F32),
        compiler_params=pltpu.CompilerParams(
            dimension_semantics=("arbitrary", "arbitrary", "arbitrary"),
            vmem_limit_bytes=VMEM_LIMIT),
        name="moba",
    )(q_aug, k_aug, v_aug)


def _out_proj_kernel(ya_ref, gz_ref, yc_ref, x_ref, mod_ref, gattn_ref, wout_ref, gfin_ref, o_ref):
    ya = ya_ref[0]
    yn = ya * lax.rsqrt(jnp.mean(ya * ya, axis=-1, keepdims=True) + EPS) * gattn_ref[...]
    yn = (yn * gz_ref[0].astype(F32)).astype(BF16)
    y = (jnp.dot(yn, wout_ref[0:D_ATTN, :], preferred_element_type=F32)
         + jnp.dot(yc_ref[0], wout_ref[D_ATTN:, :], preferred_element_type=F32))
    xo = x_ref[0] + mod_ref[0, 2:3, :] * y
    o_ref[0] = xo * lax.rsqrt(jnp.mean(xo * xo, axis=-1, keepdims=True) + EPS) * gfin_ref[...]


def _out_proj(y_attn, gz, yc, x, mod3, g_attn_out, w_out, g_final):
    bsz, seq, _ = x.shape
    row_spec = lambda width: pl.BlockSpec((1, OUT_TILE, width), lambda b, i: (b, i, 0))
    const = lambda shape: pl.BlockSpec(shape, lambda b, i: (0,) * len(shape))
    return pl.pallas_call(
        _out_proj_kernel,
        grid=(bsz, seq // OUT_TILE),
        in_specs=[row_spec(D_ATTN), row_spec(D_ATTN), row_spec(D_CONV), row_spec(D_MODEL),
                  pl.BlockSpec((1, 3, D_MODEL), lambda b, i: (b, 0, 0)),
                  const((1, D_ATTN)), const(w_out.shape), const((1, D_MODEL))],
        out_specs=row_spec(D_MODEL),
        out_shape=jax.ShapeDtypeStruct((bsz, seq, D_MODEL), F32),
        compiler_params=pltpu.CompilerParams(dimension_semantics=("arbitrary", "arbitrary"),
                                             vmem_limit_bytes=VMEM_LIMIT),
        name="out_proj",
    )(y_attn, gz, yc, x, mod3, g_attn_out.reshape(1, D_ATTN), w_out, g_final.reshape(1, D_MODEL))


def kernel(x, c, positions, w_ada, b_ada, g_norm, w_in, w_conv, g_attn_out, g_conv_out, w_out, g_final):
    bsz, seq, _ = x.shape
    assert seq % TILE == 0 and seq % OUT_TILE == 0 and seq // TILE <= HEAD_DIM // 2
    mod3 = _adaln(c, w_ada, b_ada).reshape(bsz, 3, D_MODEL)

    w_hi = w_in.astype(BF16)
    w_qk = w_in[:, :2 * D_ATTN]
    w_lo = (w_qk - w_hi[:, :2 * D_ATTN].astype(F32)).astype(BF16)
    inv_freq = ROPE_THETA ** (-jnp.arange(HALF, dtype=F32) / HALF)
    inv_freq = jnp.tile(inv_freq, LANES // HALF).reshape(1, LANES)

    q_aug, k_aug, v_aug, gz, yc = _in_proj(x, mod3, g_norm, positions, inv_freq, w_hi, w_lo,
                                           w_conv, g_conv_out)
    y_attn = _moba(q_aug, k_aug, v_aug)
    return _out_proj(y_attn, gz, yc, x, mod3, g_attn_out, w_out.astype(BF16), g_final)
```

```python
import functools

import jax
import jax.numpy as jnp
from jax import lax
from jax.experimental import pallas as pl
from jax.experimental.pallas import tpu as pltpu

D_MODEL = 1024
D_ATTN = 512
D_CONV = 512
N_HEADS = 8
HEAD_DIM = 64
HALF = HEAD_DIM // 2
CONV_WIDTH = 3
MOBA_BLOCK = 256
MOBA_TOPK = 3
ROPE_THETA = 10000.0
EPS = 1e-6
NEG = -1e30

LANES = 128
V_ROWS = HEAD_DIM + 16
TILE = MOBA_BLOCK
OUT_TILE = 512
VMEM_LIMIT = 56 * 1024 * 1024

F32 = jnp.float32
BF16 = jnp.bfloat16


def _silu(z):
    return z * (1.0 / (1.0 + jnp.exp(-z)))


def _split_bf16(a):
    hi = a.astype(BF16)
    lo = (a - hi.astype(F32)).astype(BF16)
    return hi, lo


def _adaln_kernel(c_ref, w_ref, b_ref, o_ref):
    o_ref[...] = jnp.dot(_silu(c_ref[...]), w_ref[...], precision=lax.Precision.HIGHEST,
                         preferred_element_type=F32) + b_ref[...]


def _adaln(c, w_ada, b_ada):
    bsz = c.shape[0]
    n = w_ada.shape[1]
    bn = D_MODEL
    return pl.pallas_call(
        _adaln_kernel,
        grid=(n // bn,),
        in_specs=[pl.BlockSpec((bsz, D_MODEL), lambda j: (0, 0)),
                  pl.BlockSpec((D_MODEL, bn), lambda j: (0, j)),
                  pl.BlockSpec((1, bn), lambda j: (0, j))],
        out_specs=pl.BlockSpec((bsz, bn), lambda j: (0, j)),
        out_shape=jax.ShapeDtypeStruct((bsz, n), F32),
        compiler_params=pltpu.CompilerParams(dimension_semantics=("arbitrary",),
                                             vmem_limit_bytes=VMEM_LIMIT),
        name="adaln",
    )(c, w_ada, b_ada.reshape(1, n))


def _in_proj_kernel(n_blocks, x_ref, mod_ref, gnorm_ref, pos_ref, freq_ref, whi_ref, wlo_ref,
                    wconv_ref, gconv_ref,
                    q_ref, k_ref, v_ref, gz_ref, yc_ref,
                    kbd_ref, carry_ref):
    i = pl.program_id(1)

    @pl.when(i == 0)
    def _():
        kbd_ref[...] = jnp.zeros_like(kbd_ref)
        carry_ref[...] = jnp.zeros_like(carry_ref)

    x = x_ref[0]
    xn = x * lax.rsqrt(jnp.mean(x * x, axis=-1, keepdims=True) + EPS) * gnorm_ref[...]
    h = xn * (1.0 + mod_ref[0, 1:2, :]) + mod_ref[0, 0:1, :]
    h_hi, h_lo = _split_bf16(h)

    def proj(c0, c1):
        return jnp.dot(h_hi, whi_ref[:, c0:c1], preferred_element_type=F32)

    qk = (proj(0, 2 * D_ATTN)
          + jnp.dot(h_lo, whi_ref[:, 0:2 * D_ATTN], preferred_element_type=F32)
          + jnp.dot(h_hi, wlo_ref[...], preferred_element_type=F32))

    lane = lax.broadcasted_iota(jnp.int32, (TILE, LANES), 1)
    first_half = (lane % HEAD_DIM) < HALF
    ang = pos_ref[0].astype(F32) * freq_ref[...]
    cos = jnp.cos(ang)
    sin = jnp.sin(ang)
    sin = jnp.where(first_half, -sin, sin)

    def rope(t):
        partner = jnp.where(first_half, pltpu.roll(t, LANES - HALF, 1), pltpu.roll(t, HALF, 1))
        return t * cos + partner * sin

    n_groups = D_ATTN // LANES
    q_groups = [rope(qk[:, g * LANES:(g + 1) * LANES]) for g in range(n_groups)]
    k_groups = [rope(qk[:, D_ATTN + g * LANES:D_ATTN + (g + 1) * LANES]) for g in range(n_groups)]
    q = jnp.concatenate(q_groups, axis=1)
    k = jnp.concatenate(k_groups, axis=1)

    q_hi, q_lo = _split_bf16(q)
    kbd_hi, kbd_lo = _split_bf16(kbd_ref[...])
    nt = (((1,), (1,)), ((), ()))
    scores = (lax.dot_general(kbd_hi, q_hi, nt, preferred_element_type=F32)
              + lax.dot_general(kbd_lo, q_hi, nt, preferred_element_type=F32)
              + lax.dot_general(kbd_hi, q_lo, nt, preferred_element_type=F32))
    g3 = scores.reshape(N_HEADS, n_blocks, TILE)
    blk = lax.broadcasted_iota(jnp.int32, g3.shape, 1)
    past = blk < i
    g3 = jnp.where(past, g3, NEG)
    picked = jnp.zeros(g3.shape, jnp.bool_)
    for _ in range(min(MOBA_TOPK, n_blocks - 1)):
        top = jnp.max(g3, axis=1, keepdims=True)
        first = jnp.min(jnp.where(g3 == top, blk, n_blocks), axis=1, keepdims=True)
        hit = blk == first
        picked = jnp.logical_or(picked, hit)
        g3 = jnp.where(hit, -jnp.inf, g3)
    keep = jnp.logical_or(jnp.logical_and(picked, past), blk == i)
    bias = jnp.where(keep, 0.0, NEG)

    k_mean = jnp.mean(k, axis=0, keepdims=True)
    lane_w = lax.broadcasted_iota(jnp.int32, (1, D_ATTN), 1)
    for hd in range(N_HEADS):
        own = (lane_w // HEAD_DIM) == hd
        kbd_ref[pl.ds(hd * n_blocks + i, 1), :] = jnp.where(own, k_mean, 0.0)

    q_t = (q * (HEAD_DIM ** -0.5)).T
    pad = jnp.zeros((HEAD_DIM - n_blocks, TILE), F32)
    parts = []
    for hd in range(N_HEADS):
        dims = q_t[hd * HEAD_DIM:(hd + 1) * HEAD_DIM]
        parts += [dims, bias[hd], pad] if hd % 2 == 0 else [bias[hd], pad, dims]
    q_ref[0] = jnp.concatenate(parts, axis=0).astype(BF16)

    low = lane < HEAD_DIM
    onehot_even = jnp.where(lane == HEAD_DIM + i, 1.0, 0.0)
    onehot_odd = jnp.where(lane == i, 1.0, 0.0)
    for g in range(n_groups):
        even = slice(2 * g * LANES, (2 * g + 1) * LANES)
        odd = slice((2 * g + 1) * LANES, (2 * g + 2) * LANES)
        k_ref[0, :, even] = jnp.where(low, k_groups[g], onehot_even).astype(BF16)
        k_ref[0, :, odd] = jnp.where(low, onehot_odd, k_groups[g]).astype(BF16)

    v_t = proj(2 * D_ATTN, 3 * D_ATTN).T
    ones = jnp.ones((V_ROWS - HEAD_DIM, TILE), F32)
    parts = []
    for hd in range(N_HEADS):
        parts += [v_t[hd * HEAD_DIM:(hd + 1) * HEAD_DIM], ones]
    v_ref[0] = jnp.concatenate(parts, axis=0).astype(BF16)

    gz_ref[0] = _silu(proj(3 * D_ATTN, 4 * D_ATTN)).astype(BF16)

    c0 = 4 * D_ATTN
    b_g = proj(c0, c0 + D_CONV)
    u = proj(c0 + D_CONV, c0 + 2 * D_CONV) * proj(c0 + 2 * D_CONV, c0 + 3 * D_CONV)
    z_c = proj(c0 + 3 * D_CONV, c0 + 4 * D_CONV)
    rows = lax.broadcasted_iota(jnp.int32, u.shape, 0)
    prev1 = carry_ref[7:8, :]
    prev2 = carry_ref[6:7, :]
    u1 = jnp.where(rows == 0, prev1, pltpu.roll(u, 1, 0))
    u2 = jnp.where(rows == 0, prev2, jnp.where(rows == 1, prev1, pltpu.roll(u, 2, 0)))
    carry_ref[...] = u[TILE - 8:TILE, :]
    y = b_g * (wconv_ref[2:3, :] * u + wconv_ref[1:2, :] * u1 + wconv_ref[0:1, :] * u2)
    y = y * lax.rsqrt(jnp.mean(y * y, axis=-1, keepdims=True) + EPS) * gconv_ref[...]
    yc_ref[0] = (y * _silu(z_c)).astype(BF16)


def _in_proj(x, mod3, g_norm, positions, inv_freq, w_hi, w_lo, w_conv, g_conv_out):
    bsz, seq, _ = x.shape
    n_blocks = seq // TILE
    wide = N_HEADS * LANES
    row_spec = lambda width: pl.BlockSpec((1, TILE, width), lambda b, i: (b, i, 0))
    col_spec = lambda height: pl.BlockSpec((1, height, TILE), lambda b, i: (b, 0, i))
    const = lambda shape: pl.BlockSpec(shape, lambda b, i: (0,) * len(shape))
    kbd_rows = N_HEADS * n_blocks
    return pl.pallas_call(
        functools.partial(_in_proj_kernel, n_blocks),
        grid=(bsz, n_blocks),
        in_specs=[row_spec(D_MODEL),
                  pl.BlockSpec((1, 3, D_MODEL), lambda b, i: (b, 0, 0)),
                  const((1, D_MODEL)),
                  row_spec(1),
                  const((1, LANES)),
                  const(w_hi.shape),
                  const(w_lo.shape),
                  const((CONV_WIDTH, D_CONV)),
                  const((1, D_CONV))],
        out_specs=[col_spec(wide), row_spec(wide), col_spec(N_HEADS * V_ROWS),
                   row_spec(D_ATTN), row_spec(D_CONV)],
        out_shape=[jax.ShapeDtypeStruct((bsz, wide, seq), BF16),
                   jax.ShapeDtypeStruct((bsz, seq, wide), BF16),
                   jax.ShapeDtypeStruct((bsz, N_HEADS * V_ROWS, seq), BF16),
                   jax.ShapeDtypeStruct((bsz, seq, D_ATTN), BF16),
                   jax.ShapeDtypeStruct((bsz, seq, D_CONV), BF16)],
        scratch_shapes=[pltpu.VMEM((kbd_rows, D_ATTN), F32),
                        pltpu.VMEM((8, D_CONV), F32)],
        compiler_params=pltpu.CompilerParams(dimension_semantics=("arbitrary", "arbitrary"),
                                             vmem_limit_bytes=VMEM_LIMIT),
        name="in_proj",
    )(x, mod3, g_norm.reshape(1, D_MODEL), positions.reshape(bsz, seq, 1), inv_freq,
      w_hi, w_lo, w_conv, g_conv_out.reshape(1, D_CONV))


def _moba_kernel(q_ref, k_ref, v_ref, o_ref):
    i = pl.program_id(2)
    q = [q_ref[0, hd * LANES:(hd + 1) * LANES, :] for hd in range(2)]

    def scores(hd, start):
        kj = k_ref[0, pl.ds(start, TILE), hd * LANES:(hd + 1) * LANES]
        return jnp.dot(kj, q[hd], preferred_element_type=F32)

    def pv(hd, p, start):
        vj = v_ref[0, hd * V_ROWS:(hd + 1) * V_ROWS, pl.ds(start, TILE)]
        return jnp.dot(vj, p.astype(BF16), preferred_element_type=F32)

    def update(hd, s, m, acc, start):
        m_new = jnp.maximum(m, jnp.max(s, axis=0, keepdims=True))
        acc = jnp.exp(m - m_new) * acc + pv(hd, jnp.exp(s - m_new), start)
        return m_new, acc

    def body(j, carry):
        m0, acc0, m1, acc1, s0, s1 = carry
        start = pl.multiple_of(j * TILE, TILE)
        nxt = pl.multiple_of((j + 1) * TILE, TILE)
        s0_next = scores(0, nxt)
        s1_next = scores(1, nxt)
        m0, acc0 = update(0, s0, m0, acc0, start)
        m1, acc1 = update(1, s1, m1, acc1, start)
        return m0, acc0, m1, acc1, s0_next, s1_next

    m_init = jnp.full((1, TILE), -jnp.inf, F32)
    acc_init = jnp.zeros((V_ROWS, TILE), F32)
    fin = lax.fori_loop(0, i, body, (m_init, acc_init, m_init, acc_init,
                                     scores(0, 0), scores(1, 0)))
    own = pl.multiple_of(i * TILE, TILE)
    key = lax.broadcasted_iota(jnp.int32, (TILE, TILE), 0)
    qry = lax.broadcasted_iota(jnp.int32, (TILE, TILE), 1)
    outs = []
    for hd in range(2):
        s = jnp.where(key <= qry, fin[4 + hd], NEG)
        _, acc = update(hd, s, fin[2 * hd], fin[2 * hd + 1], own)
        outs.append(acc[:HEAD_DIM] / acc[HEAD_DIM:HEAD_DIM + 1])
    o_ref[0] = jnp.concatenate(outs, axis=0).T


def _moba(q_aug, k_aug, v_aug):
    bsz, seq, wide = k_aug.shape
    pairs = wide // (2 * LANES)
    return pl.pallas_call(
        _moba_kernel,
        grid=(bsz, pairs, seq // TILE),
        in_specs=[pl.BlockSpec((1, 2 * LANES, TILE), lambda b, p, i: (b, p, i)),
                  pl.BlockSpec((1, seq, 2 * LANES), lambda b, p, i: (b, 0, p)),
                  pl.BlockSpec((1, 2 * V_ROWS, seq), lambda b, p, i: (b, p, 0))],
        out_specs=pl.BlockSpec((1, TILE, LANES), lambda b, p, i: (b, i, p)),
        out_shape=jax.ShapeDtypeStruct((bsz, seq, pairs * LANES), F32),
        compiler_params=pltpu.CompilerParams(
            dimension_semantics=("arbitrary", "arbitrary", "arbitrary"),
            vmem_limit_bytes=VMEM_LIMIT),
        name="moba",
    )(q_aug, k_aug, v_aug)


def _out_proj_kernel(ya_ref, gz_ref, yc_ref, x_ref, mod_ref, gattn_ref, wout_ref, gfin_ref, o_ref):
    ya = ya_ref[0]
    yn = ya * lax.rsqrt(jnp.mean(ya * ya, axis=-1, keepdims=True) + EPS) * gattn_ref[...]
    yn = (yn * gz_ref[0].astype(F32)).astype(BF16)
    y = (jnp.dot(yn, wout_ref[0:D_ATTN, :], preferred_element_type=F32)
         + jnp.dot(yc_ref[0], wout_ref[D_ATTN:, :], preferred_element_type=F32))
    xo = x_ref[0] + mod_ref[0, 2:3, :] * y
    o_ref[0] = xo * lax.rsqrt(jnp.mean(xo * xo, axis=-1, keepdims=True) + EPS) * gfin_ref[...]


def _out_proj(y_attn, gz, yc, x, mod3, g_attn_out, w_out, g_final):
    bsz, seq, _ = x.shape
    row_spec = lambda width: pl.BlockSpec((1, OUT_TILE, width), lambda b, i: (b, i, 0))
    const = lambda shape: pl.BlockSpec(shape, lambda b, i: (0,) * len(shape))
    return pl.pallas_call(
        _out_proj_kernel,
        grid=(bsz, seq // OUT_TILE),
        in_specs=[row_spec(D_ATTN), row_spec(D_ATTN), row_spec(D_CONV), row_spec(D_MODEL),
                  pl.BlockSpec((1, 3, D_MODEL), lambda b, i: (b, 0, 0)),
                  const((1, D_ATTN)), const(w_out.shape), const((1, D_MODEL))],
        out_specs=row_spec(D_MODEL),
        out_shape=jax.ShapeDtypeStruct((bsz, seq, D_MODEL), F32),
        compiler_params=pltpu.CompilerParams(dimension_semantics=("arbitrary", "arbitrary"),
                                             vmem_limit_bytes=VMEM_LIMIT),
        name="out_proj",
    )(y_attn, gz, yc, x, mod3, g_attn_out.reshape(1, D_ATTN), w_out, g_final.reshape(1, D_MODEL))


def kernel(x, c, positions, w_ada, b_ada, g_norm, w_in, w_conv, g_attn_out, g_conv_out, w_out, g_final):
    bsz, seq, _ = x.shape
    assert seq % TILE == 0 and seq % OUT_TILE == 0 and seq // TILE <= HEAD_DIM // 2
    mod3 = _adaln(c, w_ada, b_ada).reshape(bsz, 3, D_MODEL)

    w_hi = w_in.astype(BF16)
    w_qk = w_in[:, :2 * D_ATTN]
    w_lo = (w_qk - w_hi[:, :2 * D_ATTN].astype(F32)).astype(BF16)
    inv_freq = ROPE_THETA ** (-jnp.arange(HALF, dtype=F32) / HALF)
    inv_freq = jnp.tile(inv_freq, LANES // HALF).reshape(1, LANES)

    q_aug, k_aug, v_aug, gz, yc = _in_proj(x, mod3, g_norm, positions, inv_freq, w_hi, w_lo,
                                           w_conv, g_conv_out)
    y_attn = _moba(q_aug, k_aug, v_aug)
    return _out_proj(y_attn, gz, yc, x, mod3, g_attn_out, w_out.astype(BF16), g_final)
```

```python
import functools

import jax
import jax.numpy as jnp
from jax import lax
from jax.experimental import pallas as pl
from jax.experimental.pallas import tpu as pltpu

D_MODEL = 1024
D_ATTN = 512
D_CONV = 512
N_HEADS = 8
HEAD_DIM = 64
HALF = HEAD_DIM // 2
CONV_WIDTH = 3
MOBA_BLOCK = 256
MOBA_TOPK = 3
ROPE_THETA = 10000.0
EPS = 1e-6
NEG = -1e30
LOG2E = 1.4426950408889634

LANES = 128
V_ROWS = HEAD_DIM + 16
TILE = MOBA_BLOCK
CHUNK = 2 * TILE
OUT_TILE = 512
VMEM_LIMIT = 56 * 1024 * 1024

F32 = jnp.float32
BF16 = jnp.bfloat16


def _silu(z):
    return z * (1.0 / (1.0 + jnp.exp(-z)))


def _split_bf16(a):
    hi = a.astype(BF16)
    lo = (a - hi.astype(F32)).astype(BF16)
    return hi, lo


def _adaln_kernel(c_ref, w_ref, b_ref, o_ref):
    o_ref[...] = jnp.dot(_silu(c_ref[...]), w_ref[...], precision=lax.Precision.HIGHEST,
                         preferred_element_type=F32) + b_ref[...]


def _adaln(c, w_ada, b_ada):
    bsz = c.shape[0]
    n = w_ada.shape[1]
    bn = D_MODEL
    return pl.pallas_call(
        _adaln_kernel,
        grid=(n // bn,),
        in_specs=[pl.BlockSpec((bsz, D_MODEL), lambda j: (0, 0)),
                  pl.BlockSpec((D_MODEL, bn), lambda j: (0, j)),
                  pl.BlockSpec((1, bn), lambda j: (0, j))],
        out_specs=pl.BlockSpec((bsz, bn), lambda j: (0, j)),
        out_shape=jax.ShapeDtypeStruct((bsz, n), F32),
        compiler_params=pltpu.CompilerParams(dimension_semantics=("arbitrary",),
                                             vmem_limit_bytes=VMEM_LIMIT),
        name="adaln",
    )(c, w_ada, b_ada.reshape(1, n))


def _in_proj_kernel(n_blocks, x_ref, mod_ref, gnorm_ref, pos_ref, freq_ref, whi_ref, wlo_ref,
                    wconv_ref, gconv_ref,
                    q_ref, k_ref, v_ref, gz_ref, yc_ref,
                    kbd_ref, carry_ref):
    i = pl.program_id(1)

    @pl.when(i == 0)
    def _():
        kbd_ref[...] = jnp.zeros_like(kbd_ref)
        carry_ref[...] = jnp.zeros_like(carry_ref)

    x = x_ref[0]
    xn = x * lax.rsqrt(jnp.mean(x * x, axis=-1, keepdims=True) + EPS) * gnorm_ref[...]
    h = xn * (1.0 + mod_ref[0, 1:2, :]) + mod_ref[0, 0:1, :]
    h_hi, h_lo = _split_bf16(h)

    def proj(c0, c1):
        return jnp.dot(h_hi, whi_ref[:, c0:c1], preferred_element_type=F32)

    qk = (proj(0, 2 * D_ATTN)
          + jnp.dot(h_lo, whi_ref[:, 0:2 * D_ATTN], preferred_element_type=F32)
          + jnp.dot(h_hi, wlo_ref[...], preferred_element_type=F32))

    lane = lax.broadcasted_iota(jnp.int32, (TILE, LANES), 1)
    first_half = (lane % HEAD_DIM) < HALF
    ang = pos_ref[0].astype(F32) * freq_ref[...]
    cos = jnp.cos(ang)
    sin = jnp.sin(ang)
    sin = jnp.where(first_half, -sin, sin)

    def rope(t):
        partner = jnp.where(first_half, pltpu.roll(t, LANES - HALF, 1), pltpu.roll(t, HALF, 1))
        return t * cos + partner * sin

    n_groups = D_ATTN // LANES
    q_groups = [rope(qk[:, g * LANES:(g + 1) * LANES]) for g in range(n_groups)]
    k_groups = [rope(qk[:, D_ATTN + g * LANES:D_ATTN + (g + 1) * LANES]) for g in range(n_groups)]
    q = jnp.concatenate(q_groups, axis=1)
    k = jnp.concatenate(k_groups, axis=1)

    q_hi, q_lo = _split_bf16(q)
    kbd_hi, kbd_lo = _split_bf16(kbd_ref[...])
    nt = (((1,), (1,)), ((), ()))
    scores = (lax.dot_general(kbd_hi, q_hi, nt, preferred_element_type=F32)
              + lax.dot_general(kbd_lo, q_hi, nt, preferred_element_type=F32)
              + lax.dot_general(kbd_hi, q_lo, nt, preferred_element_type=F32))
    g3 = scores.reshape(N_HEADS, n_blocks, TILE)
    blk = lax.broadcasted_iota(jnp.int32, g3.shape, 1)
    past = blk < i
    g3 = jnp.where(past, g3, NEG)
    picked = jnp.zeros(g3.shape, jnp.bool_)
    for _ in range(min(MOBA_TOPK, n_blocks - 1)):
        top = jnp.max(g3, axis=1, keepdims=True)
        first = jnp.min(jnp.where(g3 == top, blk, n_blocks), axis=1, keepdims=True)
        hit = blk == first
        picked = jnp.logical_or(picked, hit)
        g3 = jnp.where(hit, -jnp.inf, g3)
    keep = jnp.logical_or(jnp.logical_and(picked, past), blk == i)
    bias = jnp.where(keep, 0.0, NEG)

    k_mean = jnp.mean(k, axis=0, keepdims=True)
    lane_w = lax.broadcasted_iota(jnp.int32, (1, D_ATTN), 1)
    for hd in range(N_HEADS):
        own = (lane_w // HEAD_DIM) == hd
        kbd_ref[pl.ds(hd * n_blocks + i, 1), :] = jnp.where(own, k_mean, 0.0)

    q_t = (q * (HEAD_DIM ** -0.5 * LOG2E)).T
    pad = jnp.zeros((HEAD_DIM - n_blocks, TILE), F32)
    parts = []
    for hd in range(N_HEADS):
        dims = q_t[hd * HEAD_DIM:(hd + 1) * HEAD_DIM]
        parts += [dims, bias[hd], pad] if hd % 2 == 0 else [bias[hd], pad, dims]
    q_ref[0] = jnp.concatenate(parts, axis=0).astype(BF16)

    low = lane < HEAD_DIM
    onehot_even = jnp.where(lane == HEAD_DIM + i, 1.0, 0.0)
    onehot_odd = jnp.where(lane == i, 1.0, 0.0)
    for g in range(n_groups):
        even = slice(2 * g * LANES, (2 * g + 1) * LANES)
        odd = slice((2 * g + 1) * LANES, (2 * g + 2) * LANES)
        k_ref[0, :, even] = jnp.where(low, k_groups[g], onehot_even).astype(BF16)
        k_ref[0, :, odd] = jnp.where(low, onehot_odd, k_groups[g]).astype(BF16)

    v_t = proj(2 * D_ATTN, 3 * D_ATTN).T
    ones = jnp.ones((V_ROWS - HEAD_DIM, TILE), F32)
    parts = []
    for hd in range(N_HEADS):
        parts += [v_t[hd * HEAD_DIM:(hd + 1) * HEAD_DIM], ones]
    v_ref[0] = jnp.concatenate(parts, axis=0).astype(BF16)

    gz_ref[0] = _silu(proj(3 * D_ATTN, 4 * D_ATTN)).astype(BF16)

    c0 = 4 * D_ATTN
    b_g = proj(c0, c0 + D_CONV)
    u = proj(c0 + D_CONV, c0 + 2 * D_CONV) * proj(c0 + 2 * D_CONV, c0 + 3 * D_CONV)
    z_c = proj(c0 + 3 * D_CONV, c0 + 4 * D_CONV)
    rows = lax.broadcasted_iota(jnp.int32, u.shape, 0)
    prev1 = carry_ref[7:8, :]
    prev2 = carry_ref[6:7, :]
    u1 = jnp.where(rows == 0, prev1, pltpu.roll(u, 1, 0))
    u2 = jnp.where(rows == 0, prev2, jnp.where(rows == 1, prev1, pltpu.roll(u, 2, 0)))
    carry_ref[...] = u[TILE - 8:TILE, :]
    y = b_g * (wconv_ref[2:3, :] * u + wconv_ref[1:2, :] * u1 + wconv_ref[0:1, :] * u2)
    y = y * lax.rsqrt(jnp.mean(y * y, axis=-1, keepdims=True) + EPS) * gconv_ref[...]
    yc_ref[0] = (y * _silu(z_c)).astype(BF16)


def _in_proj(x, mod3, g_norm, positions, inv_freq, w_hi, w_lo, w_conv, g_conv_out):
    bsz, seq, _ = x.shape
    n_blocks = seq // TILE
    wide = N_HEADS * LANES
    row_spec = lambda width: pl.BlockSpec((1, TILE, width), lambda b, i: (b, i, 0))
    col_spec = lambda height: pl.BlockSpec((1, height, TILE), lambda b, i: (b, 0, i))
    const = lambda shape: pl.BlockSpec(shape, lambda b, i: (0,) * len(shape))
    kbd_rows = N_HEADS * n_blocks
    return pl.pallas_call(
        functools.partial(_in_proj_kernel, n_blocks),
        grid=(bsz, n_blocks),
        in_specs=[row_spec(D_MODEL),
                  pl.BlockSpec((1, 3, D_MODEL), lambda b, i: (b, 0, 0)),
                  const((1, D_MODEL)),
                  row_spec(1),
                  const((1, LANES)),
                  const(w_hi.shape),
                  const(w_lo.shape),
                  const((CONV_WIDTH, D_CONV)),
                  const((1, D_CONV))],
        out_specs=[col_spec(wide), row_spec(wide), col_spec(N_HEADS * V_ROWS),
                   row_spec(D_ATTN), row_spec(D_CONV)],
        out_shape=[jax.ShapeDtypeStruct((bsz, wide, seq), BF16),
                   jax.ShapeDtypeStruct((bsz, seq, wide), BF16),
                   jax.ShapeDtypeStruct((bsz, N_HEADS * V_ROWS, seq), BF16),
                   jax.ShapeDtypeStruct((bsz, seq, D_ATTN), BF16),
                   jax.ShapeDtypeStruct((bsz, seq, D_CONV), BF16)],
        scratch_shapes=[pltpu.VMEM((kbd_rows, D_ATTN), F32),
                        pltpu.VMEM((8, D_CONV), F32)],
        compiler_params=pltpu.CompilerParams(dimension_semantics=("arbitrary", "arbitrary"),
                                             vmem_limit_bytes=VMEM_LIMIT),
        name="in_proj",
    )(x, mod3, g_norm.reshape(1, D_MODEL), positions.reshape(bsz, seq, 1), inv_freq,
      w_hi, w_lo, w_conv, g_conv_out.reshape(1, D_CONV))


def _moba_kernel(q_ref, k_ref, v_ref, o_ref, s_ref, m_ref, acc_ref):
    i = pl.program_id(2)
    q = [q_ref[0, hd * LANES:(hd + 1) * LANES, :] for hd in range(2)]

    def issue_scores(buf, chunk):
        start = pl.multiple_of(chunk * CHUNK, CHUNK)
        for hd in range(2):
            kj = k_ref[0, pl.ds(start, CHUNK), hd * LANES:(hd + 1) * LANES]
            s_ref[buf, hd] = jnp.dot(kj, q[hd], preferred_element_type=F32)

    def absorb(buf, chunk, causal=False):
        start = pl.multiple_of(chunk * CHUNK, CHUNK)
        for hd in range(2):
            s = s_ref[buf, hd]
            if causal:
                key = lax.broadcasted_iota(jnp.int32, (CHUNK, CHUNK), 0)
                qry = lax.broadcasted_iota(jnp.int32, (CHUNK, CHUNK), 1)
                s = jnp.where(key <= qry, s, NEG)
            m = m_ref[hd]
            m_new = jnp.maximum(m, jnp.max(s, axis=0, keepdims=True))
            p = jnp.exp2((s - m_new).astype(BF16))
            vj = v_ref[0, hd * V_ROWS:(hd + 1) * V_ROWS, pl.ds(start, CHUNK)]
            acc_ref[hd] = (jnp.exp2(m - m_new) * acc_ref[hd]
                           + jnp.dot(vj, p, preferred_element_type=F32))
            m_ref[hd] = m_new

    m_ref[...] = jnp.full(m_ref.shape, -jnp.inf, F32)
    acc_ref[...] = jnp.zeros(acc_ref.shape, F32)
    odd = i % 2

    @pl.when(odd == 1)
    def _():
        issue_scores(0, 0)
        absorb(0, 0)

    issue_scores(0, odd)

    def pair(t, carry):
        first = odd + 2 * t
        issue_scores(1, first + 1)
        absorb(0, first)
        issue_scores(0, first + 2)
        absorb(1, first + 1)
        return carry

    lax.fori_loop(0, i // 2, pair, 0)
    absorb(0, i, causal=True)
    outs = [acc_ref[hd, :HEAD_DIM] / acc_ref[hd, HEAD_DIM:HEAD_DIM + 1] for hd in range(2)]
    o_ref[0] = jnp.concatenate(outs, axis=0).T


def _moba(q_aug, k_aug, v_aug):
    bsz, seq, wide = k_aug.shape
    pairs = wide // (2 * LANES)
    return pl.pallas_call(
        _moba_kernel,
        grid=(bsz, pairs, seq // CHUNK),
        in_specs=[pl.BlockSpec((1, 2 * LANES, CHUNK), lambda b, p, i: (b, p, i)),
                  pl.BlockSpec((1, seq, 2 * LANES), lambda b, p, i: (b, 0, p)),
                  pl.BlockSpec((1, 2 * V_ROWS, seq), lambda b, p, i: (b, p, 0))],
        out_specs=pl.BlockSpec((1, CHUNK, LANES), lambda b, p, i: (b, i, p)),
        out_shape=jax.ShapeDtypeStruct((bsz, seq, pairs * LANES), F32),
        scratch_shapes=[pltpu.VMEM((2, 2, CHUNK, CHUNK), F32),
                        pltpu.VMEM((2, 1, CHUNK), F32),
                        pltpu.VMEM((2, V_ROWS, CHUNK), F32)],
        compiler_params=pltpu.CompilerParams(
            dimension_semantics=("arbitrary", "arbitrary", "arbitrary"),
            vmem_limit_bytes=VMEM_LIMIT),
        name="moba",
    )(q_aug, k_aug, v_aug)


def _out_proj_kernel(ya_ref, gz_ref, yc_ref, x_ref, mod_ref, gattn_ref, wout_ref, gfin_ref, o_ref):
    ya = ya_ref[0]
    yn = ya * lax.rsqrt(jnp.mean(ya * ya, axis=-1, keepdims=True) + EPS) * gattn_ref[...]
    yn = (yn * gz_ref[0].astype(F32)).astype(BF16)
    y = (jnp.dot(yn, wout_ref[0:D_ATTN, :], preferred_element_type=F32)
         + jnp.dot(yc_ref[0], wout_ref[D_ATTN:, :], preferred_element_type=F32))
    xo = x_ref[0] + mod_ref[0, 2:3, :] * y
    o_ref[0] = xo * lax.rsqrt(jnp.mean(xo * xo, axis=-1, keepdims=True) + EPS) * gfin_ref[...]


def _out_proj(y_attn, gz, yc, x, mod3, g_attn_out, w_out, g_final):
    bsz, seq, _ = x.shape
    row_spec = lambda width: pl.BlockSpec((1, OUT_TILE, width), lambda b, i: (b, i, 0))
    const = lambda shape: pl.BlockSpec(shape, lambda b, i: (0,) * len(shape))
    return pl.pallas_call(
        _out_proj_kernel,
        grid=(bsz, seq // OUT_TILE),
        in_specs=[row_spec(D_ATTN), row_spec(D_ATTN), row_spec(D_CONV), row_spec(D_MODEL),
                  pl.BlockSpec((1, 3, D_MODEL), lambda b, i: (b, 0, 0)),
                  const((1, D_ATTN)), const(w_out.shape), const((1, D_MODEL))],
        out_specs=row_spec(D_MODEL),
        out_shape=jax.ShapeDtypeStruct((bsz, seq, D_MODEL), F32),
        compiler_params=pltpu.CompilerParams(dimension_semantics=("arbitrary", "arbitrary"),
                                             vmem_limit_bytes=VMEM_LIMIT),
        name="out_proj",
    )(y_attn, gz, yc, x, mod3, g_attn_out.reshape(1, D_ATTN), w_out, g_final.reshape(1, D_MODEL))


def kernel(x, c, positions, w_ada, b_ada, g_norm, w_in, w_conv, g_attn_out, g_conv_out, w_out, g_final):
    bsz, seq, _ = x.shape
    assert seq % TILE == 0 and seq % OUT_TILE == 0 and seq // TILE <= HEAD_DIM // 2
    mod3 = _adaln(c, w_ada, b_ada).reshape(bsz, 3, D_MODEL)

    w_hi = w_in.astype(BF16)
    w_qk = w_in[:, :2 * D_ATTN]
    w_lo = (w_qk - w_hi[:, :2 * D_ATTN].astype(F32)).astype(BF16)
    inv_freq = ROPE_THETA ** (-jnp.arange(HALF, dtype=F32) / HALF)
    inv_freq = jnp.tile(inv_freq, LANES // HALF).reshape(1, LANES)

    q_aug, k_aug, v_aug, gz, yc = _in_proj(x, mod3, g_norm, positions, inv_freq, w_hi, w_lo,
                                           w_conv, g_conv_out)
    y_attn = _moba(q_aug, k_aug, v_aug)
    return _out_proj(y_attn, gz, yc, x, mod3, g_attn_out, w_out.astype(BF16), g_final)
```

```python
import functools

import jax
import jax.numpy as jnp
from jax import lax
from jax.experimental import pallas as pl
from jax.experimental.pallas import tpu as pltpu

D_MODEL = 1024
D_ATTN = 512
D_CONV = 512
N_HEADS = 8
HEAD_DIM = 64
HALF = HEAD_DIM // 2
CONV_WIDTH = 3
MOBA_BLOCK = 256
MOBA_TOPK = 3
ROPE_THETA = 10000.0
EPS = 1e-6
NEG = -1e30
LOG2E = 1.4426950408889634

LANES = 128
V_ROWS = HEAD_DIM + 16
TILE = MOBA_BLOCK
CHUNK = 2 * TILE
OUT_TILE = 512
VMEM_LIMIT = 56 * 1024 * 1024

F32 = jnp.float32
BF16 = jnp.bfloat16


def _silu(z):
    return z * (1.0 / (1.0 + jnp.exp(-z)))


def _split_bf16(a):
    hi = a.astype(BF16)
    lo = (a - hi.astype(F32)).astype(BF16)
    return hi, lo


def _adaln_kernel(c_ref, w_ref, b_ref, o_ref):
    o_ref[...] = jnp.dot(_silu(c_ref[...]), w_ref[...], precision=lax.Precision.HIGHEST,
                         preferred_element_type=F32) + b_ref[...]


def _adaln(c, w_ada, b_ada):
    bsz = c.shape[0]
    n = w_ada.shape[1]
    bn = D_MODEL
    return pl.pallas_call(
        _adaln_kernel,
        grid=(n // bn,),
        in_specs=[pl.BlockSpec((bsz, D_MODEL), lambda j: (0, 0)),
                  pl.BlockSpec((D_MODEL, bn), lambda j: (0, j)),
                  pl.BlockSpec((1, bn), lambda j: (0, j))],
        out_specs=pl.BlockSpec((bsz, bn), lambda j: (0, j)),
        out_shape=jax.ShapeDtypeStruct((bsz, n), F32),
        compiler_params=pltpu.CompilerParams(dimension_semantics=("arbitrary",),
                                             vmem_limit_bytes=VMEM_LIMIT),
        name="adaln",
    )(c, w_ada, b_ada.reshape(1, n))


def _in_proj_kernel(n_blocks, x_ref, mod_ref, gnorm_ref, pos_ref, freq_ref, whi_ref,
                    wconv_ref, gconv_ref,
                    q_ref, k_ref, v_ref, gz_ref, yc_ref,
                    kbd_ref, carry_ref):
    i = pl.program_id(1)

    @pl.when(i == 0)
    def _():
        kbd_ref[...] = jnp.zeros_like(kbd_ref)
        carry_ref[...] = jnp.zeros_like(carry_ref)

    x = x_ref[0]
    xn = x * lax.rsqrt(jnp.mean(x * x, axis=-1, keepdims=True) + EPS) * gnorm_ref[...]
    h = xn * (1.0 + mod_ref[0, 1:2, :]) + mod_ref[0, 0:1, :]
    h_hi = h.astype(BF16)

    def proj(c0, c1):
        return jnp.dot(h_hi, whi_ref[:, c0:c1], preferred_element_type=F32)

    qk = proj(0, 2 * D_ATTN)

    lane = lax.broadcasted_iota(jnp.int32, (TILE, LANES), 1)
    first_half = (lane % HEAD_DIM) < HALF
    ang = pos_ref[0].astype(F32) * freq_ref[...]
    cos = jnp.cos(ang)
    sin = jnp.sin(ang)
    sin = jnp.where(first_half, -sin, sin)

    def rope(t):
        partner = jnp.where(first_half, pltpu.roll(t, LANES - HALF, 1), pltpu.roll(t, HALF, 1))
        return t * cos + partner * sin

    n_groups = D_ATTN // LANES
    q_groups = [rope(qk[:, g * LANES:(g + 1) * LANES]) for g in range(n_groups)]
    k_groups = [rope(qk[:, D_ATTN + g * LANES:D_ATTN + (g + 1) * LANES]) for g in range(n_groups)]
    q = jnp.concatenate(q_groups, axis=1)
    k = jnp.concatenate(k_groups, axis=1)

    q_hi, q_lo = _split_bf16(q)
    kbd_hi, kbd_lo = _split_bf16(kbd_ref[...])
    nt = (((1,), (1,)), ((), ()))
    scores = (lax.dot_general(kbd_hi, q_hi, nt, preferred_element_type=F32)
              + lax.dot_general(kbd_lo, q_hi, nt, preferred_element_type=F32)
              + lax.dot_general(kbd_hi, q_lo, nt, preferred_element_type=F32))
    g3 = scores.reshape(N_HEADS, n_blocks, TILE)
    blk = lax.broadcasted_iota(jnp.int32, g3.shape, 1)
    past = blk < i
    g3 = jnp.where(past, g3, NEG)
    picked = jnp.zeros(g3.shape, jnp.bool_)
    for _ in range(min(MOBA_TOPK, n_blocks - 1)):
        top = jnp.max(g3, axis=1, keepdims=True)
        first = jnp.min(jnp.where(g3 == top, blk, n_blocks), axis=1, keepdims=True)
        hit = blk == first
        picked = jnp.logical_or(picked, hit)
        g3 = jnp.where(hit, -jnp.inf, g3)
    keep = jnp.logical_or(jnp.logical_and(picked, past), blk == i)
    bias = jnp.where(keep, 0.0, NEG)

    k_mean = jnp.mean(k, axis=0, keepdims=True)
    lane_w = lax.broadcasted_iota(jnp.int32, (1, D_ATTN), 1)
    for hd in range(N_HEADS):
        own = (lane_w // HEAD_DIM) == hd
        kbd_ref[pl.ds(hd * n_blocks + i, 1), :] = jnp.where(own, k_mean, 0.0)

    q_t = (q * (HEAD_DIM ** -0.5 * LOG2E)).T
    pad = jnp.zeros((HEAD_DIM - n_blocks, TILE), F32)
    parts = []
    for hd in range(N_HEADS):
        dims = q_t[hd * HEAD_DIM:(hd + 1) * HEAD_DIM]
        parts += [dims, bias[hd], pad] if hd % 2 == 0 else [bias[hd], pad, dims]
    q_ref[0] = jnp.concatenate(parts, axis=0).astype(BF16)

    low = lane < HEAD_DIM
    onehot_even = jnp.where(lane == HEAD_DIM + i, 1.0, 0.0)
    onehot_odd = jnp.where(lane == i, 1.0, 0.0)
    for g in range(n_groups):
        even = slice(2 * g * LANES, (2 * g + 1) * LANES)
        odd = slice((2 * g + 1) * LANES, (2 * g + 2) * LANES)
        k_ref[0, :, even] = jnp.where(low, k_groups[g], onehot_even).astype(BF16)
        k_ref[0, :, odd] = jnp.where(low, onehot_odd, k_groups[g]).astype(BF16)

    v_t = proj(2 * D_ATTN, 3 * D_ATTN).T
    ones = jnp.ones((V_ROWS - HEAD_DIM, TILE), F32)
    parts = []
    for hd in range(N_HEADS):
        parts += [v_t[hd * HEAD_DIM:(hd + 1) * HEAD_DIM], ones]
    v_ref[0] = jnp.concatenate(parts, axis=0).astype(BF16)

    gz_ref[0] = _silu(proj(3 * D_ATTN, 4 * D_ATTN)).astype(BF16)

    c0 = 4 * D_ATTN
    b_g = proj(c0, c0 + D_CONV)
    u = proj(c0 + D_CONV, c0 + 2 * D_CONV) * proj(c0 + 2 * D_CONV, c0 + 3 * D_CONV)
    z_c = proj(c0 + 3 * D_CONV, c0 + 4 * D_CONV)
    rows = lax.broadcasted_iota(jnp.int32, u.shape, 0)
    prev1 = carry_ref[7:8, :]
    prev2 = carry_ref[6:7, :]
    u1 = jnp.where(rows == 0, prev1, pltpu.roll(u, 1, 0))
    u2 = jnp.where(rows == 0, prev2, jnp.where(rows == 1, prev1, pltpu.roll(u, 2, 0)))
    carry_ref[...] = u[TILE - 8:TILE, :]
    y = b_g * (wconv_ref[2:3, :] * u + wconv_ref[1:2, :] * u1 + wconv_ref[0:1, :] * u2)
    y = y * lax.rsqrt(jnp.mean(y * y, axis=-1, keepdims=True) + EPS) * gconv_ref[...]
    yc_ref[0] = (y * _silu(z_c)).astype(BF16)


def _in_proj(x, mod3, g_norm, positions, inv_freq, w_hi, w_conv, g_conv_out):
    bsz, seq, _ = x.shape
    n_blocks = seq // TILE
    wide = N_HEADS * LANES
    row_spec = lambda width: pl.BlockSpec((1, TILE, width), lambda b, i: (b, i, 0))
    col_spec = lambda height: pl.BlockSpec((1, height, TILE), lambda b, i: (b, 0, i))
    const = lambda shape: pl.BlockSpec(shape, lambda b, i: (0,) * len(shape))
    kbd_rows = N_HEADS * n_blocks
    return pl.pallas_call(
        functools.partial(_in_proj_kernel, n_blocks),
        grid=(bsz, n_blocks),
        in_specs=[row_spec(D_MODEL),
                  pl.BlockSpec((1, 3, D_MODEL), lambda b, i: (b, 0, 0)),
                  const((1, D_MODEL)),
                  row_spec(1),
                  const((1, LANES)),
                  const(w_hi.shape),
                  const((CONV_WIDTH, D_CONV)),
                  const((1, D_CONV))],
        out_specs=[col_spec(wide), row_spec(wide), col_spec(N_HEADS * V_ROWS),
                   row_spec(D_ATTN), row_spec(D_CONV)],
        out_shape=[jax.ShapeDtypeStruct((bsz, wide, seq), BF16),
                   jax.ShapeDtypeStruct((bsz, seq, wide), BF16),
                   jax.ShapeDtypeStruct((bsz, N_HEADS * V_ROWS, seq), BF16),
                   jax.ShapeDtypeStruct((bsz, seq, D_ATTN), BF16),
                   jax.ShapeDtypeStruct((bsz, seq, D_CONV), BF16)],
        scratch_shapes=[pltpu.VMEM((kbd_rows, D_ATTN), F32),
                        pltpu.VMEM((8, D_CONV), F32)],
        compiler_params=pltpu.CompilerParams(dimension_semantics=("arbitrary", "arbitrary"),
                                             vmem_limit_bytes=VMEM_LIMIT),
        name="in_proj",
    )(x, mod3, g_norm.reshape(1, D_MODEL), positions.reshape(bsz, seq, 1), inv_freq,
      w_hi, w_conv, g_conv_out.reshape(1, D_CONV))


def _moba_kernel(q_ref, k_ref, v_ref, o_ref, s_ref, m_ref, acc_ref):
    i = pl.program_id(2)
    q = [q_ref[0, hd * LANES:(hd + 1) * LANES, :] for hd in range(2)]

    def issue_scores(buf, chunk):
        start = pl.multiple_of(chunk * CHUNK, CHUNK)
        for hd in range(2):
            kj = k_ref[0, pl.ds(start, CHUNK), hd * LANES:(hd + 1) * LANES]
            s_ref[buf, hd] = jnp.dot(kj, q[hd], preferred_element_type=F32)

    def absorb(buf, chunk, causal=False):
        start = pl.multiple_of(chunk * CHUNK, CHUNK)
        for hd in range(2):
            s = s_ref[buf, hd]
            if causal:
                key = lax.broadcasted_iota(jnp.int32, (CHUNK, CHUNK), 0)
                qry = lax.broadcasted_iota(jnp.int32, (CHUNK, CHUNK), 1)
                s = jnp.where(key <= qry, s, NEG)
            m = m_ref[hd]
            m_new = jnp.maximum(m, jnp.max(s, axis=0, keepdims=True))
            p = jnp.exp2((s - m_new).astype(BF16))
            vj = v_ref[0, hd * V_ROWS:(hd + 1) * V_ROWS, pl.ds(start, CHUNK)]
            acc_ref[hd] = (jnp.exp2(m - m_new) * acc_ref[hd]
                           + jnp.dot(vj, p, preferred_element_type=F32))
            m_ref[hd] = m_new

    m_ref[...] = jnp.full(m_ref.shape, -jnp.inf, F32)
    acc_ref[...] = jnp.zeros(acc_ref.shape, F32)
    odd = i % 2

    @pl.when(odd == 1)
    def _():
        issue_scores(0, 0)
        absorb(0, 0)

    issue_scores(0, odd)

    def pair(t, carry):
        first = odd + 2 * t
        issue_scores(1, first + 1)
        absorb(0, first)
        issue_scores(0, first + 2)
        absorb(1, first + 1)
        return carry

    lax.fori_loop(0, i // 2, pair, 0)
    absorb(0, i, causal=True)
    outs = [acc_ref[hd, :HEAD_DIM] / acc_ref[hd, HEAD_DIM:HEAD_DIM + 1] for hd in range(2)]
    o_ref[0] = jnp.concatenate(outs, axis=0).T


def _moba(q_aug, k_aug, v_aug):
    bsz, seq, wide = k_aug.shape
    pairs = wide // (2 * LANES)
    return pl.pallas_call(
        _moba_kernel,
        grid=(bsz, pairs, seq // CHUNK),
        in_specs=[pl.BlockSpec((1, 2 * LANES, CHUNK), lambda b, p, i: (b, p, i)),
                  pl.BlockSpec((1, seq, 2 * LANES), lambda b, p, i: (b, 0, p)),
                  pl.BlockSpec((1, 2 * V_ROWS, seq), lambda b, p, i: (b, p, 0))],
        out_specs=pl.BlockSpec((1, CHUNK, LANES), lambda b, p, i: (b, i, p)),
        out_shape=jax.ShapeDtypeStruct((bsz, seq, pairs * LANES), F32),
        scratch_shapes=[pltpu.VMEM((2, 2, CHUNK, CHUNK), F32),
                        pltpu.VMEM((2, 1, CHUNK), F32),
                        pltpu.VMEM((2, V_ROWS, CHUNK), F32)],
        compiler_params=pltpu.CompilerParams(
            dimension_semantics=("arbitrary", "arbitrary", "arbitrary"),
            vmem_limit_bytes=VMEM_LIMIT),
        name="moba",
    )(q_aug, k_aug, v_aug)


def _out_proj_kernel(ya_ref, gz_ref, yc_ref, x_ref, mod_ref, gattn_ref, wout_ref, gfin_ref, o_ref):
    ya = ya_ref[0]
    yn = ya * lax.rsqrt(jnp.mean(ya * ya, axis=-1, keepdims=True) + EPS) * gattn_ref[...]
    yn = (yn * gz_ref[0].astype(F32)).astype(BF16)
    y = (jnp.dot(yn, wout_ref[0:D_ATTN, :], preferred_element_type=F32)
         + jnp.dot(yc_ref[0], wout_ref[D_ATTN:, :], preferred_element_type=F32))
    xo = x_ref[0] + mod_ref[0, 2:3, :] * y
    o_ref[0] = xo * lax.rsqrt(jnp.mean(xo * xo, axis=-1, keepdims=True) + EPS) * gfin_ref[...]


def _out_proj(y_attn, gz, yc, x, mod3, g_attn_out, w_out, g_final):
    bsz, seq, _ = x.shape
    row_spec = lambda width: pl.BlockSpec((1, OUT_TILE, width), lambda b, i: (b, i, 0))
    const = lambda shape: pl.BlockSpec(shape, lambda b, i: (0,) * len(shape))
    return pl.pallas_call(
        _out_proj_kernel,
        grid=(bsz, seq // OUT_TILE),
        in_specs=[row_spec(D_ATTN), row_spec(D_ATTN), row_spec(D_CONV), row_spec(D_MODEL),
                  pl.BlockSpec((1, 3, D_MODEL), lambda b, i: (b, 0, 0)),
                  const((1, D_ATTN)), const(w_out.shape), const((1, D_MODEL))],
        out_specs=row_spec(D_MODEL),
        out_shape=jax.ShapeDtypeStruct((bsz, seq, D_MODEL), F32),
        compiler_params=pltpu.CompilerParams(dimension_semantics=("arbitrary", "arbitrary"),
                                             vmem_limit_bytes=VMEM_LIMIT),
        name="out_proj",
    )(y_attn, gz, yc, x, mod3, g_attn_out.reshape(1, D_ATTN), w_out, g_final.reshape(1, D_MODEL))


def kernel(x, c, positions, w_ada, b_ada, g_norm, w_in, w_conv, g_attn_out, g_conv_out, w_out, g_final):
    bsz, seq, _ = x.shape
    assert seq % TILE == 0 and seq % OUT_TILE == 0 and seq // TILE <= HEAD_DIM // 2
    mod3 = _adaln(c, w_ada, b_ada).reshape(bsz, 3, D_MODEL)

    inv_freq = ROPE_THETA ** (-jnp.arange(HALF, dtype=F32) / HALF)
    inv_freq = jnp.tile(inv_freq, LANES // HALF).reshape(1, LANES)

    q_aug, k_aug, v_aug, gz, yc = _in_proj(x, mod3, g_norm, positions, inv_freq,
                                           w_in.astype(BF16), w_conv, g_conv_out)
    y_attn = _moba(q_aug, k_aug, v_aug)
    return _out_proj(y_attn, gz, yc, x, mod3, g_attn_out, w_out.astype(BF16), g_final)
```

```python
import functools

import jax
import jax.numpy as jnp
from jax import lax
from jax.experimental import pallas as pl
from jax.experimental.pallas import tpu as pltpu

D_MODEL = 1024
D_ATTN = 512
D_CONV = 512
N_HEADS = 8
HEAD_DIM = 64
HALF = HEAD_DIM // 2
CONV_WIDTH = 3
MOBA_BLOCK = 256
MOBA_TOPK = 3
ROPE_THETA = 10000.0
EPS = 1e-6
NEG = -1e30
LOG2E = 1.4426950408889634

LANES = 128
V_ROWS = HEAD_DIM + 16
TILE = MOBA_BLOCK
CHUNK = 2 * TILE
OUT_TILE = 512
VMEM_LIMIT = 56 * 1024 * 1024

F32 = jnp.float32
BF16 = jnp.bfloat16


def _silu(z):
    return z * (1.0 / (1.0 + jnp.exp(-z)))


def _split_bf16(a):
    hi = a.astype(BF16)
    lo = (a - hi.astype(F32)).astype(BF16)
    return hi, lo


def _adaln_kernel(c_ref, w_ref, b_ref, o_ref):
    o_ref[...] = jnp.dot(_silu(c_ref[...]), w_ref[...], precision=lax.Precision.HIGHEST,
                         preferred_element_type=F32) + b_ref[...]


def _adaln(c, w_ada, b_ada):
    bsz = c.shape[0]
    n = w_ada.shape[1]
    bn = D_MODEL
    return pl.pallas_call(
        _adaln_kernel,
        grid=(n // bn,),
        in_specs=[pl.BlockSpec((bsz, D_MODEL), lambda j: (0, 0)),
                  pl.BlockSpec((D_MODEL, bn), lambda j: (0, j)),
                  pl.BlockSpec((1, bn), lambda j: (0, j))],
        out_specs=pl.BlockSpec((bsz, bn), lambda j: (0, j)),
        out_shape=jax.ShapeDtypeStruct((bsz, n), F32),
        compiler_params=pltpu.CompilerParams(dimension_semantics=("arbitrary",),
                                             vmem_limit_bytes=VMEM_LIMIT),
        name="adaln",
    )(c, w_ada, b_ada.reshape(1, n))


def _in_proj_kernel(n_blocks, x_ref, mod_ref, gnorm_ref, pos_ref, freq_ref, whi_ref,
                    wconv_ref, gconv_ref,
                    q_ref, k_ref, v_ref, gz_ref, yc_ref,
                    kbd_ref, carry_ref):
    i = pl.program_id(1)

    @pl.when(i == 0)
    def _():
        kbd_ref[...] = jnp.zeros_like(kbd_ref)
        carry_ref[...] = jnp.zeros_like(carry_ref)

    x = x_ref[0]
    xn = x * lax.rsqrt(jnp.mean(x * x, axis=-1, keepdims=True) + EPS) * gnorm_ref[...]
    h = xn * (1.0 + mod_ref[0, 1:2, :]) + mod_ref[0, 0:1, :]
    h_hi = h.astype(BF16)

    def proj(c0, c1):
        return jnp.dot(h_hi, whi_ref[:, c0:c1], preferred_element_type=F32)

    qk = proj(0, 2 * D_ATTN)

    lane = lax.broadcasted_iota(jnp.int32, (TILE, LANES), 1)
    first_half = (lane % HEAD_DIM) < HALF
    ang = pos_ref[0].astype(F32) * freq_ref[...]
    cos = jnp.cos(ang)
    sin = jnp.sin(ang)
    sin = jnp.where(first_half, -sin, sin)

    def rope(t):
        partner = jnp.where(first_half, pltpu.roll(t, LANES - HALF, 1), pltpu.roll(t, HALF, 1))
        return t * cos + partner * sin

    n_groups = D_ATTN // LANES
    q_groups = [rope(qk[:, g * LANES:(g + 1) * LANES]) for g in range(n_groups)]
    k_groups = [rope(qk[:, D_ATTN + g * LANES:D_ATTN + (g + 1) * LANES]) for g in range(n_groups)]
    q = jnp.concatenate(q_groups, axis=1)
    k = jnp.concatenate(k_groups, axis=1)

    q_hi, q_lo = _split_bf16(q)
    kbd_hi, kbd_lo = _split_bf16(kbd_ref[...])
    nt = (((1,), (1,)), ((), ()))
    scores = (lax.dot_general(kbd_hi, q_hi, nt, preferred_element_type=F32)
              + lax.dot_general(kbd_lo, q_hi, nt, preferred_element_type=F32)
              + lax.dot_general(kbd_hi, q_lo, nt, preferred_element_type=F32))
    g3 = scores.reshape(N_HEADS, n_blocks, TILE)
    blk = lax.broadcasted_iota(jnp.int32, g3.shape, 1)
    past = blk < i
    g3 = jnp.where(past, g3, NEG)
    picked = jnp.zeros(g3.shape, jnp.bool_)
    for _ in range(min(MOBA_TOPK, n_blocks - 1)):
        top = jnp.max(g3, axis=1, keepdims=True)
        first = jnp.min(jnp.where(g3 == top, blk, n_blocks), axis=1, keepdims=True)
        hit = blk == first
        picked = jnp.logical_or(picked, hit)
        g3 = jnp.where(hit, -jnp.inf, g3)
    keep = jnp.logical_or(jnp.logical_and(picked, past), blk == i)
    bias = jnp.where(keep, 0.0, NEG)

    k_mean = jnp.mean(k, axis=0, keepdims=True)
    lane_w = lax.broadcasted_iota(jnp.int32, (1, D_ATTN), 1)
    for hd in range(N_HEADS):
        own = (lane_w // HEAD_DIM) == hd
        kbd_ref[pl.ds(hd * n_blocks + i, 1), :] = jnp.where(own, k_mean, 0.0)

    q_t = (q * (HEAD_DIM ** -0.5 * LOG2E)).T
    pad = jnp.zeros((HEAD_DIM - n_blocks, TILE), F32)
    parts = []
    for hd in range(N_HEADS):
        dims = q_t[hd * HEAD_DIM:(hd + 1) * HEAD_DIM]
        parts += [dims, bias[hd], pad] if hd % 2 == 0 else [bias[hd], pad, dims]
    q_ref[0] = jnp.concatenate(parts, axis=0).astype(BF16)

    low = lane < HEAD_DIM
    onehot_even = jnp.where(lane == HEAD_DIM + i, 1.0, 0.0)
    onehot_odd = jnp.where(lane == i, 1.0, 0.0)
    for g in range(n_groups):
        even = slice(2 * g * LANES, (2 * g + 1) * LANES)
        odd = slice((2 * g + 1) * LANES, (2 * g + 2) * LANES)
        k_ref[0, :, even] = jnp.where(low, k_groups[g], onehot_even).astype(BF16)
        k_ref[0, :, odd] = jnp.where(low, onehot_odd, k_groups[g]).astype(BF16)

    v_t = proj(2 * D_ATTN, 3 * D_ATTN).T
    ones = jnp.ones((V_ROWS - HEAD_DIM, TILE), F32)
    parts = []
    for hd in range(N_HEADS):
        parts += [v_t[hd * HEAD_DIM:(hd + 1) * HEAD_DIM], ones]
    v_ref[0] = jnp.concatenate(parts, axis=0).astype(BF16)

    gz_ref[0] = _silu(proj(3 * D_ATTN, 4 * D_ATTN)).astype(BF16)

    c0 = 4 * D_ATTN
    b_g = proj(c0, c0 + D_CONV)
    u = proj(c0 + D_CONV, c0 + 2 * D_CONV) * proj(c0 + 2 * D_CONV, c0 + 3 * D_CONV)
    z_c = proj(c0 + 3 * D_CONV, c0 + 4 * D_CONV)
    rows = lax.broadcasted_iota(jnp.int32, u.shape, 0)
    prev1 = carry_ref[7:8, :]
    prev2 = carry_ref[6:7, :]
    u1 = jnp.where(rows == 0, prev1, pltpu.roll(u, 1, 0))
    u2 = jnp.where(rows == 0, prev2, jnp.where(rows == 1, prev1, pltpu.roll(u, 2, 0)))
    carry_ref[...] = u[TILE - 8:TILE, :]
    y = b_g * (wconv_ref[2:3, :] * u + wconv_ref[1:2, :] * u1 + wconv_ref[0:1, :] * u2)
    y = y * lax.rsqrt(jnp.mean(y * y, axis=-1, keepdims=True) + EPS) * gconv_ref[...]
    yc_ref[0] = (y * _silu(z_c)).astype(BF16)


def _in_proj(x, mod3, g_norm, positions, inv_freq, w_hi, w_conv, g_conv_out):
    bsz, seq, _ = x.shape
    n_blocks = seq // TILE
    wide = N_HEADS * LANES
    row_spec = lambda width: pl.BlockSpec((1, TILE, width), lambda b, i: (b, i, 0))
    col_spec = lambda height: pl.BlockSpec((1, height, TILE), lambda b, i: (b, 0, i))
    const = lambda shape: pl.BlockSpec(shape, lambda b, i: (0,) * len(shape))
    kbd_rows = N_HEADS * n_blocks
    return pl.pallas_call(
        functools.partial(_in_proj_kernel, n_blocks),
        grid=(bsz, n_blocks),
        in_specs=[row_spec(D_MODEL),
                  pl.BlockSpec((1, 3, D_MODEL), lambda b, i: (b, 0, 0)),
                  const((1, D_MODEL)),
                  row_spec(1),
                  const((1, LANES)),
                  const(w_hi.shape),
                  const((CONV_WIDTH, D_CONV)),
                  const((1, D_CONV))],
        out_specs=[col_spec(wide), row_spec(wide), col_spec(N_HEADS * V_ROWS),
                   row_spec(D_ATTN), row_spec(D_CONV)],
        out_shape=[jax.ShapeDtypeStruct((bsz, wide, seq), BF16),
                   jax.ShapeDtypeStruct((bsz, seq, wide), BF16),
                   jax.ShapeDtypeStruct((bsz, N_HEADS * V_ROWS, seq), BF16),
                   jax.ShapeDtypeStruct((bsz, seq, D_ATTN), BF16),
                   jax.ShapeDtypeStruct((bsz, seq, D_CONV), BF16)],
        scratch_shapes=[pltpu.VMEM((kbd_rows, D_ATTN), F32),
                        pltpu.VMEM((8, D_CONV), F32)],
        compiler_params=pltpu.CompilerParams(dimension_semantics=("arbitrary", "arbitrary"),
                                             vmem_limit_bytes=VMEM_LIMIT),
        name="in_proj",
    )(x, mod3, g_norm.reshape(1, D_MODEL), positions.reshape(bsz, seq, 1), inv_freq,
      w_hi, w_conv, g_conv_out.reshape(1, D_CONV))


def _even_tile(g, n_chunks):
    return jnp.where(g % 2 == 0, g, n_chunks - 1 - g)


def _moba_kernel(n_chunks, qe_ref, qo_ref, k_ref, v_ref, o_ref, s_ref, m_ref, acc_ref):
    even, odd = 0, 1
    q_refs = (qe_ref, qo_ref)
    n_even = _even_tile(pl.program_id(2), n_chunks)
    n_odd = n_chunks - 1 - n_even

    def issue_scores(buf, slot, chunk):
        start = pl.multiple_of(chunk * CHUNK, CHUNK)
        for hd in range(2):
            kj = k_ref[0, pl.ds(start, CHUNK), hd * LANES:(hd + 1) * LANES]
            qh = q_refs[slot][0, hd * LANES:(hd + 1) * LANES, :]
            s_ref[buf, hd] = jnp.dot(kj, qh, preferred_element_type=F32)

    def absorb(buf, slot, chunk, causal=False):
        start = pl.multiple_of(chunk * CHUNK, CHUNK)
        for hd in range(2):
            s = s_ref[buf, hd]
            if causal:
                key = lax.broadcasted_iota(jnp.int32, (CHUNK, CHUNK), 0)
                qry = lax.broadcasted_iota(jnp.int32, (CHUNK, CHUNK), 1)
                s = jnp.where(key <= qry, s, NEG)
            m = m_ref[slot, hd]
            m_new = jnp.maximum(m, jnp.max(s, axis=0, keepdims=True))
            p = jnp.exp2((s - m_new).astype(BF16))
            vj = v_ref[0, hd * V_ROWS:(hd + 1) * V_ROWS, pl.ds(start, CHUNK)]
            acc_ref[slot, hd] = (jnp.exp2(m - m_new) * acc_ref[slot, hd]
                                 + jnp.dot(vj, p, preferred_element_type=F32))
            m_ref[slot, hd] = m_new

    def finish(slot):
        outs = [acc_ref[slot, hd, :HEAD_DIM] / acc_ref[slot, hd, HEAD_DIM:HEAD_DIM + 1]
                for hd in range(2)]
        o_ref[0, slot, 0] = jnp.concatenate(outs, axis=0).T

    def pairs(slot, count):
        def pair(t, carry):
            issue_scores(1, slot, 2 * t + 1)
            absorb(0, slot, 2 * t)
            issue_scores(0, slot, 2 * t + 2)
            absorb(1, slot, 2 * t + 1)
            return carry
        lax.fori_loop(0, count, pair, 0)

    m_ref[...] = jnp.full(m_ref.shape, -jnp.inf, F32)
    acc_ref[...] = jnp.zeros(acc_ref.shape, F32)
    issue_scores(0, odd, 0)
    pairs(odd, (n_odd - 1) // 2)
    issue_scores(1, odd, n_odd)
    absorb(0, odd, n_odd - 1)
    issue_scores(0, even, 0)
    absorb(1, odd, n_odd, causal=True)
    finish(odd)
    pairs(even, n_even // 2)
    absorb(0, even, n_even, causal=True)
    finish(even)


def _moba(q_aug, k_aug, v_aug):
    bsz, seq, wide = k_aug.shape
    pairs = wide // (2 * LANES)
    n_chunks = seq // CHUNK
    q_spec = lambda tile: pl.BlockSpec((1, 2 * LANES, CHUNK), lambda b, p, g: (b, p, tile(g)))
    return pl.pallas_call(
        functools.partial(_moba_kernel, n_chunks),
        grid=(bsz, pairs, n_chunks // 2),
        in_specs=[q_spec(lambda g: _even_tile(g, n_chunks)),
                  q_spec(lambda g: n_chunks - 1 - _even_tile(g, n_chunks)),
                  pl.BlockSpec((1, seq, 2 * LANES), lambda b, p, g: (b, 0, p)),
                  pl.BlockSpec((1, 2 * V_ROWS, seq), lambda b, p, g: (b, p, 0))],
        out_specs=pl.BlockSpec((1, 2, 1, CHUNK, LANES), lambda b, p, g: (b, 0, g, 0, p)),
        out_shape=jax.ShapeDtypeStruct((bsz, 2, n_chunks // 2, CHUNK, pairs * LANES), F32),
        scratch_shapes=[pltpu.VMEM((2, 2, CHUNK, CHUNK), F32),
                        pltpu.VMEM((2, 2, 1, CHUNK), F32),
                        pltpu.VMEM((2, 2, V_ROWS, CHUNK), F32)],
        compiler_params=pltpu.CompilerParams(
            dimension_semantics=("arbitrary", "arbitrary", "arbitrary"),
            vmem_limit_bytes=VMEM_LIMIT),
        name="moba",
    )(q_aug, q_aug, k_aug, v_aug)


def _out_proj_kernel(ya_ref, gz_ref, yc_ref, x_ref, mod_ref, gattn_ref, wout_ref, gfin_ref, o_ref):
    ya = ya_ref[0, 0, 0]
    yn = ya * lax.rsqrt(jnp.mean(ya * ya, axis=-1, keepdims=True) + EPS) * gattn_ref[...]
    yn = (yn * gz_ref[0].astype(F32)).astype(BF16)
    y = (jnp.dot(yn, wout_ref[0:D_ATTN, :], preferred_element_type=F32)
         + jnp.dot(yc_ref[0], wout_ref[D_ATTN:, :], preferred_element_type=F32))
    xo = x_ref[0] + mod_ref[0, 2:3, :] * y
    o_ref[0] = xo * lax.rsqrt(jnp.mean(xo * xo, axis=-1, keepdims=True) + EPS) * gfin_ref[...]


def _out_proj(y_attn, gz, yc, x, mod3, g_attn_out, w_out, g_final):
    bsz, seq, _ = x.shape
    n_tiles = seq // OUT_TILE
    assert OUT_TILE == CHUNK and y_attn.shape == (bsz, 2, n_tiles // 2, CHUNK, D_ATTN)
    row_spec = lambda width: pl.BlockSpec((1, OUT_TILE, width), lambda b, i: (b, i, 0))
    const = lambda shape: pl.BlockSpec(shape, lambda b, i: (0,) * len(shape))
    return pl.pallas_call(
        _out_proj_kernel,
        grid=(bsz, seq // OUT_TILE),
        in_specs=[pl.BlockSpec((1, 1, 1, OUT_TILE, D_ATTN),
                               lambda b, i: (b, i % 2, jnp.minimum(i, n_tiles - 1 - i), 0, 0)),
                  row_spec(D_ATTN), row_spec(D_CONV), row_spec(D_MODEL),
                  pl.BlockSpec((1, 3, D_MODEL), lambda b, i: (b, 0, 0)),
                  const((1, D_ATTN)), const(w_out.shape), const((1, D_MODEL))],
        out_specs=row_spec(D_MODEL),
        out_shape=jax.ShapeDtypeStruct((bsz, seq, D_MODEL), F32),
        compiler_params=pltpu.CompilerParams(dimension_semantics=("arbitrary", "arbitrary"),
                                             vmem_limit_bytes=VMEM_LIMIT),
        name="out_proj",
    )(y_attn, gz, yc, x, mod3, g_attn_out.reshape(1, D_ATTN), w_out, g_final.reshape(1, D_MODEL))


def kernel(x, c, positions, w_ada, b_ada, g_norm, w_in, w_conv, g_attn_out, g_conv_out, w_out, g_final):
    bsz, seq, _ = x.shape
    assert seq % TILE == 0 and seq % OUT_TILE == 0 and seq // TILE <= HEAD_DIM // 2
    mod3 = _adaln(c, w_ada, b_ada).reshape(bsz, 3, D_MODEL)

    inv_freq = ROPE_THETA ** (-jnp.arange(HALF, dtype=F32) / HALF)
    inv_freq = jnp.tile(inv_freq, LANES // HALF).reshape(1, LANES)

    q_aug, k_aug, v_aug, gz, yc = _in_proj(x, mod3, g_norm, positions, inv_freq,
                                           w_in.astype(BF16), w_conv, g_conv_out)
    y_attn = _moba(q_aug, k_aug, v_aug)
    return _out_proj(y_attn, gz, yc, x, mod3, g_attn_out, w_out.astype(BF16), g_final)
```

```python
import functools

import jax
import jax.numpy as jnp
from jax import lax
from jax.experimental import pallas as pl
from jax.experimental.pallas import tpu as pltpu

D_MODEL = 1024
D_ATTN = 512
D_CONV = 512
N_HEADS = 8
HEAD_DIM = 64
HALF = HEAD_DIM // 2
CONV_WIDTH = 3
MOBA_BLOCK = 256
MOBA_TOPK = 3
ROPE_THETA = 10000.0
EPS = 1e-6
NEG = -1e30
LOG2E = 1.4426950408889634

LANES = 128
V_ROWS = HEAD_DIM + 16
TILE = MOBA_BLOCK
CHUNK = 2 * TILE
OUT_TILE = 512
VMEM_LIMIT = 56 * 1024 * 1024

F32 = jnp.float32
BF16 = jnp.bfloat16


def _silu(z):
    return z * (1.0 / (1.0 + jnp.exp(-z)))


def _split_bf16(a):
    hi = a.astype(BF16)
    lo = (a - hi.astype(F32)).astype(BF16)
    return hi, lo


def _adaln_kernel(c_ref, w_ref, b_ref, o_ref):
    o_ref[...] = jnp.dot(_silu(c_ref[...]), w_ref[...], precision=lax.Precision.HIGHEST,
                         preferred_element_type=F32) + b_ref[...]


def _adaln(c, w_ada, b_ada):
    bsz = c.shape[0]
    n = w_ada.shape[1]
    bn = D_MODEL
    return pl.pallas_call(
        _adaln_kernel,
        grid=(n // bn,),
        in_specs=[pl.BlockSpec((bsz, D_MODEL), lambda j: (0, 0)),
                  pl.BlockSpec((D_MODEL, bn), lambda j: (0, j)),
                  pl.BlockSpec((1, bn), lambda j: (0, j))],
        out_specs=pl.BlockSpec((bsz, bn), lambda j: (0, j)),
        out_shape=jax.ShapeDtypeStruct((bsz, n), F32),
        compiler_params=pltpu.CompilerParams(dimension_semantics=("arbitrary",),
                                             vmem_limit_bytes=VMEM_LIMIT),
        name="adaln",
    )(c, w_ada, b_ada.reshape(1, n))


def _in_proj_kernel(n_blocks, x_ref, mod_ref, gnorm_ref, pos_ref, freq_ref, whi_ref,
                    wconv_ref, gconv_ref,
                    q_ref, k_ref, v_ref, gz_ref, yc_ref,
                    kbd_ref, carry_ref, cos_ref, sin_ref):
    i = pl.program_id(1)

    @pl.when(i == 0)
    def _():
        kbd_ref[...] = jnp.zeros_like(kbd_ref)
        carry_ref[...] = jnp.zeros_like(carry_ref)

    x = x_ref[0]
    xn = x * lax.rsqrt(jnp.mean(x * x, axis=-1, keepdims=True) + EPS) * gnorm_ref[...]
    h = xn * (1.0 + mod_ref[0, 1:2, :]) + mod_ref[0, 0:1, :]
    h_hi = h.astype(BF16)

    def proj(c0, c1):
        return jnp.dot(h_hi, whi_ref[:, c0:c1], preferred_element_type=F32)

    lane_c = lax.broadcasted_iota(jnp.int32, (TILE // 4, LANES), 1)
    ang = pos_ref[0].astype(F32) * freq_ref[...]
    sign = jnp.where((lane_c % HEAD_DIM) < HALF, -1.0, 1.0)
    for table_ref, table, scale in ((cos_ref, jnp.cos(ang), None), (sin_ref, jnp.sin(ang), sign)):
        for a in range(LANES // HALF):
            z = jnp.where(lane_c // HALF == a, table, 0.0)
            z = z + pltpu.roll(z, 2 * HALF, 1)
            z = z + pltpu.roll(z, HALF, 1)
            table_ref[pl.ds(a, TILE // 4, stride=LANES // HALF), :] = z if scale is None else z * scale
    cos = cos_ref[...]
    sin = sin_ref[...]

    qk = proj(0, 2 * D_ATTN)

    lane = lax.broadcasted_iota(jnp.int32, (TILE, LANES), 1)
    first_half = (lane % HEAD_DIM) < HALF

    def rope(t):
        partner = jnp.where(first_half, pltpu.roll(t, LANES - HALF, 1), pltpu.roll(t, HALF, 1))
        return t * cos + partner * sin

    n_groups = D_ATTN // LANES
    q_groups = [rope(qk[:, g * LANES:(g + 1) * LANES]) for g in range(n_groups)]
    k_groups = [rope(qk[:, D_ATTN + g * LANES:D_ATTN + (g + 1) * LANES]) for g in range(n_groups)]
    q = jnp.concatenate(q_groups, axis=1)
    k = jnp.concatenate(k_groups, axis=1)

    q_hi, q_lo = _split_bf16(q)
    kbd_hi, kbd_lo = _split_bf16(kbd_ref[...])
    nt = (((1,), (1,)), ((), ()))
    scores = (lax.dot_general(kbd_hi, q_hi, nt, preferred_element_type=F32)
              + lax.dot_general(kbd_lo, q_hi, nt, preferred_element_type=F32)
              + lax.dot_general(kbd_hi, q_lo, nt, preferred_element_type=F32))
    g3 = scores.reshape(N_HEADS, n_blocks, TILE)
    blk = lax.broadcasted_iota(jnp.int32, g3.shape, 1)
    past = blk < i
    g3 = jnp.where(past, g3, NEG)
    picked = jnp.zeros(g3.shape, jnp.bool_)
    for _ in range(min(MOBA_TOPK, n_blocks - 1)):
        top = jnp.max(g3, axis=1, keepdims=True)
        first = jnp.min(jnp.where(g3 == top, blk, n_blocks), axis=1, keepdims=True)
        hit = blk == first
        picked = jnp.logical_or(picked, hit)
        g3 = jnp.where(hit, -jnp.inf, g3)
    keep = jnp.logical_or(jnp.logical_and(picked, past), blk == i)
    bias = jnp.where(keep, 0.0, NEG)

    k_mean = jnp.mean(k, axis=0, keepdims=True)
    lane_w = lax.broadcasted_iota(jnp.int32, (1, D_ATTN), 1)
    for hd in range(N_HEADS):
        own = (lane_w // HEAD_DIM) == hd
        kbd_ref[pl.ds(hd * n_blocks + i, 1), :] = jnp.where(own, k_mean, 0.0)

    q_t = (q * (HEAD_DIM ** -0.5 * LOG2E)).T
    pad = jnp.zeros((HEAD_DIM - n_blocks, TILE), F32)
    parts = []
    for hd in range(N_HEADS):
        dims = q_t[hd * HEAD_DIM:(hd + 1) * HEAD_DIM]
        parts += [dims, bias[hd], pad] if hd % 2 == 0 else [bias[hd], pad, dims]
    q_ref[0] = jnp.concatenate(parts, axis=0).astype(BF16)

    low = lane < HEAD_DIM
    onehot_even = jnp.where(lane == HEAD_DIM + i, 1.0, 0.0)
    onehot_odd = jnp.where(lane == i, 1.0, 0.0)
    for g in range(n_groups):
        even = slice(2 * g * LANES, (2 * g + 1) * LANES)
        odd = slice((2 * g + 1) * LANES, (2 * g + 2) * LANES)
        k_ref[0, :, even] = jnp.where(low, k_groups[g], onehot_even).astype(BF16)
        k_ref[0, :, odd] = jnp.where(low, onehot_odd, k_groups[g]).astype(BF16)

    v_t = proj(2 * D_ATTN, 3 * D_ATTN).T
    ones = jnp.ones((V_ROWS - HEAD_DIM, TILE), F32)
    parts = []
    for hd in range(N_HEADS):
        parts += [v_t[hd * HEAD_DIM:(hd + 1) * HEAD_DIM], ones]
    v_ref[0] = jnp.concatenate(parts, axis=0).astype(BF16)

    gz_ref[0] = _silu(proj(3 * D_ATTN, 4 * D_ATTN)).astype(BF16)

    c0 = 4 * D_ATTN
    b_g = proj(c0, c0 + D_CONV)
    u = proj(c0 + D_CONV, c0 + 2 * D_CONV) * proj(c0 + 2 * D_CONV, c0 + 3 * D_CONV)
    z_c = proj(c0 + 3 * D_CONV, c0 + 4 * D_CONV)
    rows = lax.broadcasted_iota(jnp.int32, u.shape, 0)
    prev1 = carry_ref[7:8, :]
    prev2 = carry_ref[6:7, :]
    u1 = jnp.where(rows == 0, prev1, pltpu.roll(u, 1, 0))
    u2 = jnp.where(rows == 0, prev2, jnp.where(rows == 1, prev1, pltpu.roll(u, 2, 0)))
    carry_ref[...] = u[TILE - 8:TILE, :]
    y = b_g * (wconv_ref[2:3, :] * u + wconv_ref[1:2, :] * u1 + wconv_ref[0:1, :] * u2)
    y = y * lax.rsqrt(jnp.mean(y * y, axis=-1, keepdims=True) + EPS) * gconv_ref[...]
    yc_ref[0] = (y * _silu(z_c)).astype(BF16)


def _in_proj(x, mod3, g_norm, positions, inv_freq, w_hi, w_conv, g_conv_out):
    bsz, seq, _ = x.shape
    n_blocks = seq // TILE
    wide = N_HEADS * LANES
    row_spec = lambda width: pl.BlockSpec((1, TILE, width), lambda b, i: (b, i, 0))
    col_spec = lambda height: pl.BlockSpec((1, height, TILE), lambda b, i: (b, 0, i))
    const = lambda shape: pl.BlockSpec(shape, lambda b, i: (0,) * len(shape))
    kbd_rows = N_HEADS * n_blocks
    per_row = LANES // HALF
    pos_compact = jnp.repeat(positions.reshape(bsz, seq // per_row, per_row), HALF, axis=2)
    return pl.pallas_call(
        functools.partial(_in_proj_kernel, n_blocks),
        grid=(bsz, n_blocks),
        in_specs=[row_spec(D_MODEL),
                  pl.BlockSpec((1, 3, D_MODEL), lambda b, i: (b, 0, 0)),
                  const((1, D_MODEL)),
                  pl.BlockSpec((1, TILE // 4, LANES), lambda b, i: (b, i, 0)),
                  const((1, LANES)),
                  const(w_hi.shape),
                  const((CONV_WIDTH, D_CONV)),
                  const((1, D_CONV))],
        out_specs=[col_spec(wide), row_spec(wide), col_spec(N_HEADS * V_ROWS),
                   row_spec(D_ATTN), row_spec(D_CONV)],
        out_shape=[jax.ShapeDtypeStruct((bsz, wide, seq), BF16),
                   jax.ShapeDtypeStruct((bsz, seq, wide), BF16),
                   jax.ShapeDtypeStruct((bsz, N_HEADS * V_ROWS, seq), BF16),
                   jax.ShapeDtypeStruct((bsz, seq, D_ATTN), BF16),
                   jax.ShapeDtypeStruct((bsz, seq, D_CONV), BF16)],
        scratch_shapes=[pltpu.VMEM((kbd_rows, D_ATTN), F32),
                        pltpu.VMEM((8, D_CONV), F32),
                        pltpu.VMEM((TILE, LANES), F32),
                        pltpu.VMEM((TILE, LANES), F32)],
        compiler_params=pltpu.CompilerParams(dimension_semantics=("arbitrary", "arbitrary"),
                                             vmem_limit_bytes=VMEM_LIMIT),
        name="in_proj",
    )(x, mod3, g_norm.reshape(1, D_MODEL), pos_compact, inv_freq,
      w_hi, w_conv, g_conv_out.reshape(1, D_CONV))


def _even_tile(g, n_chunks):
    return jnp.where(g % 2 == 0, g, n_chunks - 1 - g)


def _moba_kernel(n_chunks, qe_ref, qo_ref, k_ref, v_ref, o_ref, s_ref, smax_ref, m_ref,
                 acc_ref):
    even, odd = 0, 1
    q_refs = (qe_ref, qo_ref)
    n_even = _even_tile(pl.program_id(2), n_chunks)
    n_odd = n_chunks - 1 - n_even

    def issue_scores(buf, slot, chunk):
        start = pl.multiple_of(chunk * CHUNK, CHUNK)
        for hd in range(2):
            kj = k_ref[0, pl.ds(start, CHUNK), hd * LANES:(hd + 1) * LANES]
            qh = q_refs[slot][0, hd * LANES:(hd + 1) * LANES, :]
            s = jnp.dot(kj, qh, preferred_element_type=F32)
            s_ref[buf, hd] = s.astype(BF16)
            smax_ref[buf, hd] = jnp.max(s, axis=0, keepdims=True)

    def absorb(buf, slot, chunk, causal=False):
        start = pl.multiple_of(chunk * CHUNK, CHUNK)
        for hd in range(2):
            s = s_ref[buf, hd]
            smax = smax_ref[buf, hd]
            if causal:
                key = lax.broadcasted_iota(jnp.int32, (CHUNK, CHUNK), 0)
                qry = lax.broadcasted_iota(jnp.int32, (CHUNK, CHUNK), 1)
                s = jnp.where(key <= qry, s.astype(F32), NEG)
                smax = jnp.max(s, axis=0, keepdims=True)
            m = m_ref[slot, hd]
            m_new = jnp.maximum(m, smax).astype(BF16)
            p = jnp.exp2(s.astype(BF16) - m_new)
            m_new = m_new.astype(F32)
            vj = v_ref[0, hd * V_ROWS:(hd + 1) * V_ROWS, pl.ds(start, CHUNK)]
            acc_ref[slot, hd] = (jnp.exp2(m - m_new) * acc_ref[slot, hd]
                                 + jnp.dot(vj, p, preferred_element_type=F32))
            m_ref[slot, hd] = m_new

    def finish(slot):
        outs = [acc_ref[slot, hd, :HEAD_DIM] / acc_ref[slot, hd, HEAD_DIM:HEAD_DIM + 1]
                for hd in range(2)]
        o_ref[0, slot, 0] = jnp.concatenate(outs, axis=0).T

    def pairs(slot, count):
        def pair(t, carry):
            issue_scores(1, slot, 2 * t + 1)
            absorb(0, slot, 2 * t)
            issue_scores(0, slot, 2 * t + 2)
            absorb(1, slot, 2 * t + 1)
            return carry
        lax.fori_loop(0, count, pair, 0)

    m_ref[...] = jnp.full(m_ref.shape, -jnp.inf, F32)
    acc_ref[...] = jnp.zeros(acc_ref.shape, F32)
    issue_scores(0, odd, 0)
    pairs(odd, (n_odd - 1) // 2)
    issue_scores(1, odd, n_odd)
    absorb(0, odd, n_odd - 1)
    issue_scores(0, even, 0)
    absorb(1, odd, n_odd, causal=True)
    finish(odd)
    pairs(even, n_even // 2)
    absorb(0, even, n_even, causal=True)
    finish(even)


def _moba(q_aug, k_aug, v_aug):
    bsz, seq, wide = k_aug.shape
    pairs = wide // (2 * LANES)
    n_chunks = seq // CHUNK
    q_spec = lambda tile: pl.BlockSpec((1, 2 * LANES, CHUNK), lambda b, p, g: (b, p, tile(g)))
    return pl.pallas_call(
        functools.partial(_moba_kernel, n_chunks),
        grid=(bsz, pairs, n_chunks // 2),
        in_specs=[q_spec(lambda g: _even_tile(g, n_chunks)),
                  q_spec(lambda g: n_chunks - 1 - _even_tile(g, n_chunks)),
                  pl.BlockSpec((1, seq, 2 * LANES), lambda b, p, g: (b, 0, p)),
                  pl.BlockSpec((1, 2 * V_ROWS, seq), lambda b, p, g: (b, p, 0))],
        out_specs=pl.BlockSpec((1, 2, 1, CHUNK, LANES), lambda b, p, g: (b, 0, g, 0, p)),
        out_shape=jax.ShapeDtypeStruct((bsz, 2, n_chunks // 2, CHUNK, pairs * LANES), F32),
        scratch_shapes=[pltpu.VMEM((2, 2, CHUNK, CHUNK), BF16),
                        pltpu.VMEM((2, 2, 1, CHUNK), F32),
                        pltpu.VMEM((2, 2, 1, CHUNK), F32),
                        pltpu.VMEM((2, 2, V_ROWS, CHUNK), F32)],
        compiler_params=pltpu.CompilerParams(
            dimension_semantics=("arbitrary", "arbitrary", "arbitrary"),
            vmem_limit_bytes=VMEM_LIMIT),
        name="moba",
    )(q_aug, q_aug, k_aug, v_aug)


def _out_proj_kernel(ya_ref, gz_ref, yc_ref, x_ref, mod_ref, gattn_ref, wout_ref, gfin_ref, o_ref):
    ya = ya_ref[0, 0, 0]
    yn = ya * lax.rsqrt(jnp.mean(ya * ya, axis=-1, keepdims=True) + EPS) * gattn_ref[...]
    yn = (yn * gz_ref[0].astype(F32)).astype(BF16)
    y = (jnp.dot(yn, wout_ref[0:D_ATTN, :], preferred_element_type=F32)
         + jnp.dot(yc_ref[0], wout_ref[D_ATTN:, :], preferred_element_type=F32))
    xo = x_ref[0] + mod_ref[0, 2:3, :] * y
    o_ref[0] = xo * lax.rsqrt(jnp.mean(xo * xo, axis=-1, keepdims=True) + EPS) * gfin_ref[...]


def _out_proj(y_attn, gz, yc, x, mod3, g_attn_out, w_out, g_final):
    bsz, seq, _ = x.shape
    n_tiles = seq // OUT_TILE
    assert OUT_TILE == CHUNK and y_attn.shape == (bsz, 2, n_tiles // 2, CHUNK, D_ATTN)
    row_spec = lambda width: pl.BlockSpec((1, OUT_TILE, width), lambda b, i: (b, i, 0))
    const = lambda shape: pl.BlockSpec(shape, lambda b, i: (0,) * len(shape))
    return pl.pallas_call(
        _out_proj_kernel,
        grid=(bsz, seq // OUT_TILE),
        in_specs=[pl.BlockSpec((1, 1, 1, OUT_TILE, D_ATTN),
                               lambda b, i: (b, i % 2, jnp.minimum(i, n_tiles - 1 - i), 0, 0)),
                  row_spec(D_ATTN), row_spec(D_CONV), row_spec(D_MODEL),
                  pl.BlockSpec((1, 3, D_MODEL), lambda b, i: (b, 0, 0)),
                  const((1, D_ATTN)), const(w_out.shape), const((1, D_MODEL))],
        out_specs=row_spec(D_MODEL),
        out_shape=jax.ShapeDtypeStruct((bsz, seq, D_MODEL), F32),
        compiler_params=pltpu.CompilerParams(dimension_semantics=("arbitrary", "arbitrary"),
                                             vmem_limit_bytes=VMEM_LIMIT),
        name="out_proj",
    )(y_attn, gz, yc, x, mod3, g_attn_out.reshape(1, D_ATTN), w_out, g_final.reshape(1, D_MODEL))


def kernel(x, c, positions, w_ada, b_ada, g_norm, w_in, w_conv, g_attn_out, g_conv_out, w_out, g_final):
    bsz, seq, _ = x.shape
    assert seq % TILE == 0 and seq % OUT_TILE == 0 and seq // TILE <= HEAD_DIM // 2
    mod3 = _adaln(c, w_ada, b_ada).reshape(bsz, 3, D_MODEL)

    inv_freq = ROPE_THETA ** (-jnp.arange(HALF, dtype=F32) / HALF)
    inv_freq = jnp.tile(inv_freq, LANES // HALF).reshape(1, LANES)

    q_aug, k_aug, v_aug, gz, yc = _in_proj(x, mod3, g_norm, positions, inv_freq,
                                           w_in.astype(BF16), w_conv, g_conv_out)
    y_attn = _moba(q_aug, k_aug, v_aug)
    return _out_proj(y_attn, gz, yc, x, mod3, g_attn_out, w_out.astype(BF16), g_final)
```

```python
import functools

import jax
import jax.numpy as jnp
from jax import lax
from jax.experimental import pallas as pl
from jax.experimental.pallas import tpu as pltpu

D_MODEL = 1024
D_ATTN = 512
D_CONV = 512
N_HEADS = 8
HEAD_DIM = 64
HALF = HEAD_DIM // 2
CONV_WIDTH = 3
MOBA_BLOCK = 256
MOBA_TOPK = 3
ROPE_THETA = 10000.0
EPS = 1e-6
NEG = -1e30
LOG2E = 1.4426950408889634

LANES = 128
V_ROWS = HEAD_DIM + 16
TILE = MOBA_BLOCK
CHUNK = 2 * TILE
MXU_COLS = 256
OUT_TILE = 512
VMEM_LIMIT = 56 * 1024 * 1024

F32 = jnp.float32
BF16 = jnp.bfloat16


def _silu(z):
    return z * (1.0 / (1.0 + jnp.exp(-z)))


def _split_bf16(a):
    hi = a.astype(BF16)
    lo = (a - hi.astype(F32)).astype(BF16)
    return hi, lo


def _adaln_kernel(c_ref, w_ref, b_ref, o_ref):
    act = _silu(c_ref[...])
    rows = [jnp.sum(act[:, b:b + 1] * w_ref[...], axis=0, keepdims=True)
            for b in range(act.shape[1])]
    o_ref[...] = jnp.concatenate(rows, axis=0) + b_ref[...]


def _adaln(c, w_ada, b_ada):
    bsz = c.shape[0]
    n = w_ada.shape[1]
    bn = D_MODEL
    return pl.pallas_call(
        _adaln_kernel,
        grid=(n // bn,),
        in_specs=[pl.BlockSpec((D_MODEL, bsz), lambda j: (0, 0)),
                  pl.BlockSpec((D_MODEL, bn), lambda j: (0, j)),
                  pl.BlockSpec((1, bn), lambda j: (0, j))],
        out_specs=pl.BlockSpec((bsz, bn), lambda j: (0, j)),
        out_shape=jax.ShapeDtypeStruct((bsz, n), F32),
        compiler_params=pltpu.CompilerParams(dimension_semantics=("arbitrary",),
                                             vmem_limit_bytes=VMEM_LIMIT),
        name="adaln",
    )(c.T, w_ada, b_ada.reshape(1, n))


def _in_proj_kernel(n_blocks, x_ref, mod_ref, gnorm_ref, pos_ref, freq_ref, whi_ref,
                    wconv_ref, gconv_ref,
                    q_ref, k_ref, v_ref, gz_ref, yc_ref,
                    kbd_ref, carry_ref, cos_ref, sin_ref, proj_ref):
    i = pl.program_id(1)

    @pl.when(i == 0)
    def _():
        kbd_ref[...] = jnp.zeros_like(kbd_ref)
        carry_ref[...] = jnp.zeros_like(carry_ref)

    x = x_ref[0]
    xn = x * lax.rsqrt(jnp.mean(x * x, axis=-1, keepdims=True) + EPS) * gnorm_ref[...]
    h = xn * (1.0 + mod_ref[0, 1:2, :]) + mod_ref[0, 0:1, :]
    h_hi = h.astype(BF16)

    n_cols = whi_ref.shape[1]
    for c0 in range(0, n_cols, D_ATTN):
        proj_ref[:, c0:c0 + D_ATTN] = jnp.dot(h_hi, whi_ref[:, c0:c0 + D_ATTN],
                                              preferred_element_type=F32)

    def proj(c0, c1):
        return proj_ref[:, c0:c1]

    lane_c = lax.broadcasted_iota(jnp.int32, (TILE // 4, LANES), 1)
    ang = pos_ref[0].astype(F32) * freq_ref[...]
    sign = jnp.where((lane_c % HEAD_DIM) < HALF, -1.0, 1.0)
    for table_ref, table, scale in ((cos_ref, jnp.cos(ang), None), (sin_ref, jnp.sin(ang), sign)):
        for a in range(LANES // HALF):
            z = jnp.where(lane_c // HALF == a, table, 0.0)
            z = z + pltpu.roll(z, 2 * HALF, 1)
            z = z + pltpu.roll(z, HALF, 1)
            table_ref[pl.ds(a, TILE // 4, stride=LANES // HALF), :] = z if scale is None else z * scale
    cos = cos_ref[...]
    sin = sin_ref[...]

    lane = lax.broadcasted_iota(jnp.int32, (TILE, LANES), 1)
    first_half = (lane % HEAD_DIM) < HALF

    def rope(t):
        partner = jnp.where(first_half, pltpu.roll(t, LANES - HALF, 1), pltpu.roll(t, HALF, 1))
        return t * cos + partner * sin

    n_groups = D_ATTN // LANES
    q_groups = [rope(proj(g * LANES, (g + 1) * LANES)) for g in range(n_groups)]
    k_groups = [rope(proj(D_ATTN + g * LANES, D_ATTN + (g + 1) * LANES)) for g in range(n_groups)]
    q = jnp.concatenate(q_groups, axis=1)
    k = jnp.concatenate(k_groups, axis=1)

    q_hi, q_lo = _split_bf16(q)
    kbd_hi, kbd_lo = _split_bf16(kbd_ref[...])
    nt = (((1,), (1,)), ((), ()))
    scores = (lax.dot_general(kbd_hi, q_hi, nt, preferred_element_type=F32)
              + lax.dot_general(kbd_lo, q_hi, nt, preferred_element_type=F32)
              + lax.dot_general(kbd_hi, q_lo, nt, preferred_element_type=F32))
    g3 = scores.reshape(N_HEADS, n_blocks, TILE)
    blk = lax.broadcasted_iota(jnp.int32, g3.shape, 1)
    past = blk < i
    g3 = jnp.where(past, g3, NEG)
    picked = jnp.zeros(g3.shape, jnp.bool_)
    for _ in range(min(MOBA_TOPK, n_blocks - 1)):
        top = jnp.max(g3, axis=1, keepdims=True)
        first = jnp.min(jnp.where(g3 == top, blk, n_blocks), axis=1, keepdims=True)
        hit = blk == first
        picked = jnp.logical_or(picked, hit)
        g3 = jnp.where(hit, -jnp.inf, g3)
    keep = jnp.logical_or(jnp.logical_and(picked, past), blk == i)
    bias = jnp.where(keep, 0.0, NEG)

    k_mean = jnp.mean(k, axis=0, keepdims=True)
    lane_w = lax.broadcasted_iota(jnp.int32, (1, D_ATTN), 1)
    for hd in range(N_HEADS):
        own = (lane_w // HEAD_DIM) == hd
        kbd_ref[pl.ds(hd * n_blocks + i, 1), :] = jnp.where(own, k_mean, 0.0)

    q_t = (q * (HEAD_DIM ** -0.5 * LOG2E)).T
    pad = jnp.zeros((HEAD_DIM - n_blocks, TILE), F32)
    parts = []
    for hd in range(N_HEADS):
        dims = q_t[hd * HEAD_DIM:(hd + 1) * HEAD_DIM]
        parts += [dims, bias[hd], pad] if hd % 2 == 0 else [bias[hd], pad, dims]
    q_ref[0] = jnp.concatenate(parts, axis=0).astype(BF16)

    low = lane < HEAD_DIM
    onehot_even = jnp.where(lane == HEAD_DIM + i, 1.0, 0.0)
    onehot_odd = jnp.where(lane == i, 1.0, 0.0)
    for g in range(n_groups):
        even = slice(2 * g * LANES, (2 * g + 1) * LANES)
        odd = slice((2 * g + 1) * LANES, (2 * g + 2) * LANES)
        k_ref[0, :, even] = jnp.where(low, k_groups[g], onehot_even).astype(BF16)
        k_ref[0, :, odd] = jnp.where(low, onehot_odd, k_groups[g]).astype(BF16)

    v_t = proj(2 * D_ATTN, 3 * D_ATTN).T
    ones = jnp.ones((V_ROWS - HEAD_DIM, TILE), F32)
    parts = []
    for hd in range(N_HEADS):
        parts += [v_t[hd * HEAD_DIM:(hd + 1) * HEAD_DIM], ones]
    v_ref[0] = jnp.concatenate(parts, axis=0).astype(BF16)

    gz_ref[0] = _silu(proj(3 * D_ATTN, 4 * D_ATTN)).astype(BF16)

    c0 = 4 * D_ATTN
    b_g = proj(c0, c0 + D_CONV)
    u = proj(c0 + D_CONV, c0 + 2 * D_CONV) * proj(c0 + 2 * D_CONV, c0 + 3 * D_CONV)
    z_c = proj(c0 + 3 * D_CONV, c0 + 4 * D_CONV)
    rows = lax.broadcasted_iota(jnp.int32, u.shape, 0)
    prev1 = carry_ref[7:8, :]
    prev2 = carry_ref[6:7, :]
    u1 = jnp.where(rows == 0, prev1, pltpu.roll(u, 1, 0))
    u2 = jnp.where(rows == 0, prev2, jnp.where(rows == 1, prev1, pltpu.roll(u, 2, 0)))
    carry_ref[...] = u[TILE - 8:TILE, :]
    y = b_g * (wconv_ref[2:3, :] * u + wconv_ref[1:2, :] * u1 + wconv_ref[0:1, :] * u2)
    y = y * lax.rsqrt(jnp.mean(y * y, axis=-1, keepdims=True) + EPS) * gconv_ref[...]
    yc_ref[0] = (y * _silu(z_c)).astype(BF16)


def _in_proj(x, mod3, g_norm, positions, inv_freq, w_hi, w_conv, g_conv_out):
    bsz, seq, _ = x.shape
    n_blocks = seq // TILE
    wide = N_HEADS * LANES
    row_spec = lambda width: pl.BlockSpec((1, TILE, width), lambda b, i: (b, i, 0))
    col_spec = lambda height: pl.BlockSpec((1, height, TILE), lambda b, i: (b, 0, i))
    const = lambda shape: pl.BlockSpec(shape, lambda b, i: (0,) * len(shape))
    kbd_rows = N_HEADS * n_blocks
    per_row = LANES // HALF
    pos_compact = jnp.repeat(positions.reshape(bsz, seq // per_row, per_row), HALF, axis=2)
    return pl.pallas_call(
        functools.partial(_in_proj_kernel, n_blocks),
        grid=(bsz, n_blocks),
        in_specs=[row_spec(D_MODEL),
                  pl.BlockSpec((1, 3, D_MODEL), lambda b, i: (b, 0, 0)),
                  const((1, D_MODEL)),
                  pl.BlockSpec((1, TILE // 4, LANES), lambda b, i: (b, i, 0)),
                  const((1, LANES)),
                  const(w_hi.shape),
                  const((CONV_WIDTH, D_CONV)),
                  const((1, D_CONV))],
        out_specs=[col_spec(wide), row_spec(wide), col_spec(N_HEADS * V_ROWS),
                   row_spec(D_ATTN), row_spec(D_CONV)],
        out_shape=[jax.ShapeDtypeStruct((bsz, wide, seq), BF16),
                   jax.ShapeDtypeStruct((bsz, seq, wide), BF16),
                   jax.ShapeDtypeStruct((bsz, N_HEADS * V_ROWS, seq), BF16),
                   jax.ShapeDtypeStruct((bsz, seq, D_ATTN), BF16),
                   jax.ShapeDtypeStruct((bsz, seq, D_CONV), BF16)],
        scratch_shapes=[pltpu.VMEM((kbd_rows, D_ATTN), F32),
                        pltpu.VMEM((8, D_CONV), F32),
                        pltpu.VMEM((TILE, LANES), F32),
                        pltpu.VMEM((TILE, LANES), F32),
                        pltpu.VMEM((TILE, w_hi.shape[1]), F32)],
        compiler_params=pltpu.CompilerParams(dimension_semantics=("arbitrary", "arbitrary"),
                                             vmem_limit_bytes=VMEM_LIMIT),
        name="in_proj",
    )(x, mod3, g_norm.reshape(1, D_MODEL), pos_compact, inv_freq,
      w_hi, w_conv, g_conv_out.reshape(1, D_CONV))


def _even_tile(g, n_chunks):
    return jnp.where(g % 2 == 0, g, n_chunks - 1 - g)


def _moba_kernel(n_chunks, qe_ref, qo_ref, k_ref, v_ref, o_ref, s_ref, smax_ref, m_ref,
                 acc_ref):
    even, odd = 0, 1
    q_refs = (qe_ref, qo_ref)
    n_even = _even_tile(pl.program_id(2), n_chunks)
    n_odd = n_chunks - 1 - n_even

    def issue_scores(buf, slot, chunk):
        start = pl.multiple_of(chunk * CHUNK, CHUNK)
        for hd in range(2):
            kj = k_ref[0, pl.ds(start, CHUNK), hd * LANES:(hd + 1) * LANES]
            qh = q_refs[slot][0, hd * LANES:(hd + 1) * LANES, :]
            s = jnp.dot(kj, qh, preferred_element_type=F32)
            s_ref[buf, hd] = s.astype(BF16)
            smax_ref[buf, hd] = jnp.max(s, axis=0, keepdims=True)

    def absorb(buf, slot, chunk, causal=False):
        start = pl.multiple_of(chunk * CHUNK, CHUNK)
        for hd in range(2):
            s = s_ref[buf, hd]
            smax = smax_ref[buf, hd]
            if causal:
                key = lax.broadcasted_iota(jnp.int32, (CHUNK, CHUNK), 0)
                qry = lax.broadcasted_iota(jnp.int32, (CHUNK, CHUNK), 1)
                s = jnp.where(key <= qry, s.astype(F32), NEG)
                smax = jnp.max(s, axis=0, keepdims=True)
            m = m_ref[slot, hd]
            m_new = jnp.maximum(m, smax).astype(BF16)
            p = jnp.exp2(s.astype(BF16) - m_new)
            m_new = m_new.astype(F32)
            vj = v_ref[0, hd * V_ROWS:(hd + 1) * V_ROWS, pl.ds(start, CHUNK)]
            acc_ref[slot, hd] = (jnp.exp2(m - m_new) * acc_ref[slot, hd]
                                 + jnp.dot(vj, p, preferred_element_type=F32))
            m_ref[slot, hd] = m_new

    def finish(slot):
        outs = [acc_ref[slot, hd, :HEAD_DIM] / acc_ref[slot, hd, HEAD_DIM:HEAD_DIM + 1]
                for hd in range(2)]
        o_ref[0, slot, 0] = jnp.concatenate(outs, axis=0).T.astype(o_ref.dtype)

    def issue_and_absorb(issue_buf, issue_slot, issue_chunk, buf, slot, chunk):
        issue_start = pl.multiple_of(issue_chunk * CHUNK, CHUNK)
        start = pl.multiple_of(chunk * CHUNK, CHUNK)
        for hd in range(2):
            rows = slice(hd * LANES, (hd + 1) * LANES)
            for half in range(CHUNK // MXU_COLS):
                cols = slice(half * MXU_COLS, (half + 1) * MXU_COLS)
                kj = k_ref[0, pl.ds(issue_start, CHUNK), rows]
                s = jnp.dot(kj, q_refs[issue_slot][0, rows, cols],
                            preferred_element_type=F32)
                s_ref[issue_buf, hd, :, cols] = s.astype(BF16)
                smax_ref[issue_buf, hd, :, cols] = jnp.max(s, axis=0, keepdims=True)

                m = m_ref[slot, hd, :, cols]
                m_new = jnp.maximum(m, smax_ref[buf, hd, :, cols]).astype(BF16)
                p = jnp.exp2(s_ref[buf, hd, :, cols] - m_new)
                m_new = m_new.astype(F32)
                vj = v_ref[0, hd * V_ROWS:(hd + 1) * V_ROWS, pl.ds(start, CHUNK)]
                acc_ref[slot, hd, :, cols] = (jnp.exp2(m - m_new) * acc_ref[slot, hd, :, cols]
                                              + jnp.dot(vj, p, preferred_element_type=F32))
                m_ref[slot, hd, :, cols] = m_new

    def pairs(slot, count):
        def pair(t, carry):
            issue_and_absorb(1, slot, 2 * t + 1, 0, slot, 2 * t)
            issue_and_absorb(0, slot, 2 * t + 2, 1, slot, 2 * t + 1)
            return carry
        lax.fori_loop(0, count, pair, 0)

    m_ref[...] = jnp.full(m_ref.shape, -jnp.inf, F32)
    acc_ref[...] = jnp.zeros(acc_ref.shape, F32)
    issue_scores(0, odd, 0)
    pairs(odd, (n_odd - 1) // 2)
    issue_and_absorb(1, odd, n_odd, 0, odd, n_odd - 1)
    issue_scores(0, even, 0)
    absorb(1, odd, n_odd, causal=True)
    finish(odd)
    pairs(even, n_even // 2)
    absorb(0, even, n_even, causal=True)
    finish(even)


def _moba(q_aug, k_aug, v_aug):
    bsz, seq, wide = k_aug.shape
    pairs = wide // (2 * LANES)
    n_chunks = seq // CHUNK
    q_spec = lambda tile: pl.BlockSpec((1, 2 * LANES, CHUNK), lambda b, p, g: (b, p, tile(g)))
    return pl.pallas_call(
        functools.partial(_moba_kernel, n_chunks),
        grid=(bsz, pairs, n_chunks // 2),
        in_specs=[q_spec(lambda g: _even_tile(g, n_chunks)),
                  q_spec(lambda g: n_chunks - 1 - _even_tile(g, n_chunks)),
                  pl.BlockSpec((1, seq, 2 * LANES), lambda b, p, g: (b, 0, p)),
                  pl.BlockSpec((1, 2 * V_ROWS, seq), lambda b, p, g: (b, p, 0))],
        out_specs=pl.BlockSpec((1, 2, 1, CHUNK, LANES), lambda b, p, g: (b, 0, g, 0, p)),
        out_shape=jax.ShapeDtypeStruct((bsz, 2, n_chunks // 2, CHUNK, pairs * LANES), BF16),
        scratch_shapes=[pltpu.VMEM((2, 2, CHUNK, CHUNK), BF16),
                        pltpu.VMEM((2, 2, 1, CHUNK), F32),
                        pltpu.VMEM((2, 2, 1, CHUNK), F32),
                        pltpu.VMEM((2, 2, V_ROWS, CHUNK), F32)],
        compiler_params=pltpu.CompilerParams(
            dimension_semantics=("arbitrary", "arbitrary", "arbitrary"),
            vmem_limit_bytes=VMEM_LIMIT),
        name="moba",
    )(q_aug, q_aug, k_aug, v_aug)


def _out_proj_kernel(ya_ref, gz_ref, yc_ref, x_ref, mod_ref, gattn_ref, wout_ref, gfin_ref, o_ref):
    ya = ya_ref[0, 0, 0].astype(F32)
    yn = ya * lax.rsqrt(jnp.mean(ya * ya, axis=-1, keepdims=True) + EPS) * gattn_ref[...]
    yn = (yn * gz_ref[0].astype(F32)).astype(BF16)
    y = (jnp.dot(yn, wout_ref[0:D_ATTN, :], preferred_element_type=F32)
         + jnp.dot(yc_ref[0], wout_ref[D_ATTN:, :], preferred_element_type=F32))
    xo = x_ref[0] + mod_ref[0, 2:3, :] * y
    o_ref[0] = xo * lax.rsqrt(jnp.mean(xo * xo, axis=-1, keepdims=True) + EPS) * gfin_ref[...]


def _out_proj(y_attn, gz, yc, x, mod3, g_attn_out, w_out, g_final):
    bsz, seq, _ = x.shape
    n_tiles = seq // OUT_TILE
    assert OUT_TILE == CHUNK and y_attn.shape == (bsz, 2, n_tiles // 2, CHUNK, D_ATTN)
    row_spec = lambda width: pl.BlockSpec((1, OUT_TILE, width), lambda b, i: (b, i, 0))
    const = lambda shape: pl.BlockSpec(shape, lambda b, i: (0,) * len(shape))
    return pl.pallas_call(
        _out_proj_kernel,
        grid=(bsz, seq // OUT_TILE),
        in_specs=[pl.BlockSpec((1, 1, 1, OUT_TILE, D_ATTN),
                               lambda b, i: (b, i % 2, jnp.minimum(i, n_tiles - 1 - i), 0, 0)),
                  row_spec(D_ATTN), row_spec(D_CONV), row_spec(D_MODEL),
                  pl.BlockSpec((1, 3, D_MODEL), lambda b, i: (b, 0, 0)),
                  const((1, D_ATTN)), const(w_out.shape), const((1, D_MODEL))],
        out_specs=row_spec(D_MODEL),
        out_shape=jax.ShapeDtypeStruct((bsz, seq, D_MODEL), F32),
        compiler_params=pltpu.CompilerParams(dimension_semantics=("arbitrary", "arbitrary"),
                                             vmem_limit_bytes=VMEM_LIMIT),
        name="out_proj",
    )(y_attn, gz, yc, x, mod3, g_attn_out.reshape(1, D_ATTN), w_out, g_final.reshape(1, D_MODEL))


def kernel(x, c, positions, w_ada, b_ada, g_norm, w_in, w_conv, g_attn_out, g_conv_out, w_out, g_final):
    bsz, seq, _ = x.shape
    assert seq % TILE == 0 and seq % OUT_TILE == 0 and seq // TILE <= HEAD_DIM // 2
    mod3 = _adaln(c, w_ada, b_ada).reshape(bsz, 3, D_MODEL)

    inv_freq = ROPE_THETA ** (-jnp.arange(HALF, dtype=F32) / HALF)
    inv_freq = jnp.tile(inv_freq, LANES // HALF).reshape(1, LANES)

    q_aug, k_aug, v_aug, gz, yc = _in_proj(x, mod3, g_norm, positions, inv_freq,
                                           w_in.astype(BF16), w_conv, g_conv_out)
    y_attn = _moba(q_aug, k_aug, v_aug)
    return _out_proj(y_attn, gz, yc, x, mod3, g_attn_out, w_out.astype(BF16), g_final)
```

```python
import functools

import jax
import jax.numpy as jnp
from jax import lax
from jax.experimental import pallas as pl
from jax.experimental.pallas import tpu as pltpu

D_MODEL = 1024
D_ATTN = 512
D_CONV = 512
N_HEADS = 8
HEAD_DIM = 64
HALF = HEAD_DIM // 2
CONV_WIDTH = 3
MOBA_BLOCK = 256
MOBA_TOPK = 3
ROPE_THETA = 10000.0
EPS = 1e-6
NEG = -1e30
LOG2E = 1.4426950408889634

LANES = 128
V_ROWS = HEAD_DIM + 16
TILE = MOBA_BLOCK
CHUNK = 2 * TILE
MXU_COLS = 256
PAIRS_PER_TRIP = 3
OUT_TILE = 512
VMEM_LIMIT = 56 * 1024 * 1024

F32 = jnp.float32
BF16 = jnp.bfloat16


def _silu(z):
    return z * (1.0 / (1.0 + jnp.exp(-z)))


def _split_bf16(a):
    hi = a.astype(BF16)
    lo = (a - hi.astype(F32)).astype(BF16)
    return hi, lo


def _adaln_kernel(c_ref, w_ref, b_ref, o_ref):
    act = _silu(c_ref[...])
    rows = [jnp.sum(act[:, b:b + 1] * w_ref[...], axis=0, keepdims=True)
            for b in range(act.shape[1])]
    o_ref[...] = jnp.concatenate(rows, axis=0) + b_ref[...]


def _adaln(c, w_ada, b_ada):
    bsz = c.shape[0]
    n = w_ada.shape[1]
    bn = D_MODEL
    return pl.pallas_call(
        _adaln_kernel,
        grid=(n // bn,),
        in_specs=[pl.BlockSpec((D_MODEL, bsz), lambda j: (0, 0)),
                  pl.BlockSpec((D_MODEL, bn), lambda j: (0, j)),
                  pl.BlockSpec((1, bn), lambda j: (0, j))],
        out_specs=pl.BlockSpec((bsz, bn), lambda j: (0, j)),
        out_shape=jax.ShapeDtypeStruct((bsz, n), F32),
        compiler_params=pltpu.CompilerParams(dimension_semantics=("arbitrary",),
                                             vmem_limit_bytes=VMEM_LIMIT),
        name="adaln",
    )(c.T, w_ada, b_ada.reshape(1, n))


def _in_proj_kernel(n_blocks, x_ref, mod_ref, gnorm_ref, pos_ref, freq_ref, whi_ref,
                    wconv_ref, gconv_ref,
                    q_ref, k_ref, v_ref, gz_ref, yc_ref,
                    kbd_ref, carry_ref, cos_ref, sin_ref, proj_ref):
    i = pl.program_id(1)

    @pl.when(i == 0)
    def _():
        kbd_ref[...] = jnp.zeros_like(kbd_ref)
        carry_ref[...] = jnp.zeros_like(carry_ref)

    x = x_ref[0]
    xn = x * lax.rsqrt(jnp.mean(x * x, axis=-1, keepdims=True) + EPS) * gnorm_ref[...]
    h = xn * (1.0 + mod_ref[0, 1:2, :]) + mod_ref[0, 0:1, :]
    h_hi = h.astype(BF16)

    n_cols = whi_ref.shape[1]
    for c0 in range(0, n_cols, D_ATTN):
        proj_ref[:, c0:c0 + D_ATTN] = jnp.dot(h_hi, whi_ref[:, c0:c0 + D_ATTN],
                                              preferred_element_type=F32)

    def proj(c0, c1):
        return proj_ref[:, c0:c1]

    lane_c = lax.broadcasted_iota(jnp.int32, (TILE // 4, LANES), 1)
    ang = pos_ref[0].astype(F32) * freq_ref[...]
    sign = jnp.where((lane_c % HEAD_DIM) < HALF, -1.0, 1.0)
    for table_ref, table, scale in ((cos_ref, jnp.cos(ang), None), (sin_ref, jnp.sin(ang), sign)):
        for a in range(LANES // HALF):
            z = jnp.where(lane_c // HALF == a, table, 0.0)
            z = z + pltpu.roll(z, 2 * HALF, 1)
            z = z + pltpu.roll(z, HALF, 1)
            table_ref[pl.ds(a, TILE // 4, stride=LANES // HALF), :] = z if scale is None else z * scale
    cos = cos_ref[...]
    sin = sin_ref[...]

    lane = lax.broadcasted_iota(jnp.int32, (TILE, LANES), 1)
    first_half = (lane % HEAD_DIM) < HALF

    def rope(t):
        partner = jnp.where(first_half, pltpu.roll(t, LANES - HALF, 1), pltpu.roll(t, HALF, 1))
        return t * cos + partner * sin

    n_groups = D_ATTN // LANES
    q_groups = [rope(proj(g * LANES, (g + 1) * LANES)) for g in range(n_groups)]
    k_groups = [rope(proj(D_ATTN + g * LANES, D_ATTN + (g + 1) * LANES)) for g in range(n_groups)]
    q = jnp.concatenate(q_groups, axis=1)
    k = jnp.concatenate(k_groups, axis=1)

    q_hi, q_lo = _split_bf16(q)
    kbd_hi, kbd_lo = _split_bf16(kbd_ref[...])
    nt = (((1,), (1,)), ((), ()))
    scores = (lax.dot_general(kbd_hi, q_hi, nt, preferred_element_type=F32)
              + lax.dot_general(kbd_lo, q_hi, nt, preferred_element_type=F32)
              + lax.dot_general(kbd_hi, q_lo, nt, preferred_element_type=F32))
    g3 = scores.reshape(N_HEADS, n_blocks, TILE)
    blk = lax.broadcasted_iota(jnp.int32, g3.shape, 1)
    past = blk < i
    g3 = jnp.where(past, g3, NEG)
    picked = jnp.zeros(g3.shape, jnp.bool_)
    for _ in range(min(MOBA_TOPK, n_blocks - 1)):
        top = jnp.max(g3, axis=1, keepdims=True)
        first = jnp.min(jnp.where(g3 == top, blk, n_blocks), axis=1, keepdims=True)
        hit = blk == first
        picked = jnp.logical_or(picked, hit)
        g3 = jnp.where(hit, -jnp.inf, g3)
    keep = jnp.logical_or(jnp.logical_and(picked, past), blk == i)
    bias = jnp.where(keep, 0.0, NEG)

    k_mean = jnp.mean(k, axis=0, keepdims=True)
    lane_w = lax.broadcasted_iota(jnp.int32, (1, D_ATTN), 1)
    for hd in range(N_HEADS):
        own = (lane_w // HEAD_DIM) == hd
        kbd_ref[pl.ds(hd * n_blocks + i, 1), :] = jnp.where(own, k_mean, 0.0)

    q_t = (q * (HEAD_DIM ** -0.5 * LOG2E)).T
    pad = jnp.zeros((HEAD_DIM - n_blocks, TILE), F32)
    parts = []
    for hd in range(N_HEADS):
        dims = q_t[hd * HEAD_DIM:(hd + 1) * HEAD_DIM]
        parts += [dims, bias[hd], pad] if hd % 2 == 0 else [bias[hd], pad, dims]
    q_ref[0] = jnp.concatenate(parts, axis=0).astype(BF16)

    low = lane < HEAD_DIM
    onehot_even = jnp.where(lane == HEAD_DIM + i, 1.0, 0.0)
    onehot_odd = jnp.where(lane == i, 1.0, 0.0)
    for g in range(n_groups):
        even = slice(2 * g * LANES, (2 * g + 1) * LANES)
        odd = slice((2 * g + 1) * LANES, (2 * g + 2) * LANES)
        k_ref[0, :, even] = jnp.where(low, k_groups[g], onehot_even).astype(BF16)
        k_ref[0, :, odd] = jnp.where(low, onehot_odd, k_groups[g]).astype(BF16)

    v_t = proj(2 * D_ATTN, 3 * D_ATTN).T
    ones = jnp.ones((V_ROWS - HEAD_DIM, TILE), F32)
    parts = []
    for hd in range(N_HEADS):
        parts += [v_t[hd * HEAD_DIM:(hd + 1) * HEAD_DIM], ones]
    v_ref[0] = jnp.concatenate(parts, axis=0).astype(BF16)

    gz_ref[0] = _silu(proj(3 * D_ATTN, 4 * D_ATTN)).astype(BF16)

    c0 = 4 * D_ATTN
    b_g = proj(c0, c0 + D_CONV)
    u = proj(c0 + D_CONV, c0 + 2 * D_CONV) * proj(c0 + 2 * D_CONV, c0 + 3 * D_CONV)
    z_c = proj(c0 + 3 * D_CONV, c0 + 4 * D_CONV)
    rows = lax.broadcasted_iota(jnp.int32, u.shape, 0)
    prev1 = carry_ref[7:8, :]
    prev2 = carry_ref[6:7, :]
    u1 = jnp.where(rows == 0, prev1, pltpu.roll(u, 1, 0))
    u2 = jnp.where(rows == 0, prev2, jnp.where(rows == 1, prev1, pltpu.roll(u, 2, 0)))
    carry_ref[...] = u[TILE - 8:TILE, :]
    y = b_g * (wconv_ref[2:3, :] * u + wconv_ref[1:2, :] * u1 + wconv_ref[0:1, :] * u2)
    y = y * lax.rsqrt(jnp.mean(y * y, axis=-1, keepdims=True) + EPS) * gconv_ref[...]
    yc_ref[0] = (y * _silu(z_c)).astype(BF16)


def _in_proj(x, mod3, g_norm, positions, inv_freq, w_hi, w_conv, g_conv_out):
    bsz, seq, _ = x.shape
    n_blocks = seq // TILE
    wide = N_HEADS * LANES
    row_spec = lambda width: pl.BlockSpec((1, TILE, width), lambda b, i: (b, i, 0))
    col_spec = lambda height: pl.BlockSpec((1, height, TILE), lambda b, i: (b, 0, i))
    const = lambda shape: pl.BlockSpec(shape, lambda b, i: (0,) * len(shape))
    kbd_rows = N_HEADS * n_blocks
    per_row = LANES // HALF
    pos_compact = jnp.repeat(positions.reshape(bsz, seq // per_row, per_row), HALF, axis=2)
    return pl.pallas_call(
        functools.partial(_in_proj_kernel, n_blocks),
        grid=(bsz, n_blocks),
        in_specs=[row_spec(D_MODEL),
                  pl.BlockSpec((1, 3, D_MODEL), lambda b, i: (b, 0, 0)),
                  const((1, D_MODEL)),
                  pl.BlockSpec((1, TILE // 4, LANES), lambda b, i: (b, i, 0)),
                  const((1, LANES)),
                  const(w_hi.shape),
                  const((CONV_WIDTH, D_CONV)),
                  const((1, D_CONV))],
        out_specs=[col_spec(wide), row_spec(wide), col_spec(N_HEADS * V_ROWS),
                   row_spec(D_ATTN), row_spec(D_CONV)],
        out_shape=[jax.ShapeDtypeStruct((bsz, wide, seq), BF16),
                   jax.ShapeDtypeStruct((bsz, seq, wide), BF16),
                   jax.ShapeDtypeStruct((bsz, N_HEADS * V_ROWS, seq), BF16),
                   jax.ShapeDtypeStruct((bsz, seq, D_ATTN), BF16),
                   jax.ShapeDtypeStruct((bsz, seq, D_CONV), BF16)],
        scratch_shapes=[pltpu.VMEM((kbd_rows, D_ATTN), F32),
                        pltpu.VMEM((8, D_CONV), F32),
                        pltpu.VMEM((TILE, LANES), F32),
                        pltpu.VMEM((TILE, LANES), F32),
                        pltpu.VMEM((TILE, w_hi.shape[1]), F32)],
        compiler_params=pltpu.CompilerParams(dimension_semantics=("arbitrary", "arbitrary"),
                                             vmem_limit_bytes=VMEM_LIMIT),
        name="in_proj",
    )(x, mod3, g_norm.reshape(1, D_MODEL), pos_compact, inv_freq,
      w_hi, w_conv, g_conv_out.reshape(1, D_CONV))


def _moba_kernel(n_chunks, qa_ref, qb_ref, k_ref, v_ref, o_ref, q_ref, s_ref, smax_ref, m_ref,
                 acc_ref):
    g = pl.program_id(2)
    own = (g, n_chunks - 1 - g)
    n_past = n_chunks - 1
    q_ref[0] = qa_ref[0]
    q_ref[1] = qb_ref[0]

    def past_item(j):
        in_first = j < own[0]
        return jnp.where(in_first, 0, 1), jnp.where(in_first, j, j - own[0])

    def issue_unit(buf, slot, chunk, hd, cols):
        start = pl.multiple_of(chunk * CHUNK, CHUNK)
        rows = slice(hd * LANES, (hd + 1) * LANES)
        kj = k_ref[0, pl.ds(start, CHUNK), rows]
        s = jnp.dot(kj, q_ref[slot, rows, cols], preferred_element_type=F32)
        s_ref[buf, hd, :, cols] = s.astype(BF16)
        smax_ref[buf, hd, :, cols] = jnp.max(s, axis=0, keepdims=True)

    def absorb_unit(buf, slot, chunk, hd, cols, causal):
        start = pl.multiple_of(chunk * CHUNK, CHUNK)
        s = s_ref[buf, hd, :, cols]
        smax = smax_ref[buf, hd, :, cols]
        if causal:
            key = lax.broadcasted_iota(jnp.int32, s.shape, 0)
            qry = lax.broadcasted_iota(jnp.int32, s.shape, 1) + cols.start
            s = jnp.where(key <= qry, s.astype(F32), NEG)
            smax = jnp.max(s, axis=0, keepdims=True)
        m = m_ref[slot, hd, :, cols]
        m_new = jnp.maximum(m, smax).astype(BF16)
        p = jnp.exp2(s.astype(BF16) - m_new)
        m_new = m_new.astype(F32)
        vj = v_ref[0, hd * V_ROWS:(hd + 1) * V_ROWS, pl.ds(start, CHUNK)]
        acc_ref[slot, hd, :, cols] = (jnp.exp2(m - m_new) * acc_ref[slot, hd, :, cols]
                                      + jnp.dot(vj, p, preferred_element_type=F32))
        m_ref[slot, hd, :, cols] = m_new

    def step(issue=None, absorb=None, causal=False):
        for hd in range(2):
            for half in range(CHUNK // MXU_COLS):
                cols = slice(half * MXU_COLS, (half + 1) * MXU_COLS)
                if issue is not None:
                    issue_unit(*issue, hd, cols)
                if absorb is not None:
                    absorb_unit(*absorb, hd, cols, causal)

    m_ref[...] = jnp.full(m_ref.shape, -jnp.inf, F32)
    acc_ref[...] = jnp.zeros(acc_ref.shape, F32)
    step(issue=(0, 0, own[0]))
    step(issue=(1, 1, own[1]), absorb=(0, 0, own[0]), causal=True)
    step(issue=(0, *past_item(0)), absorb=(1, 1, own[1]), causal=True)

    def pair(j):
        step(issue=(1, *past_item(j + 1)), absorb=(0, *past_item(j)))
        step(issue=(0, *past_item(j + 2)), absorb=(1, *past_item(j + 1)))

    def pairs(t, carry):
        for u in range(PAIRS_PER_TRIP):
            pair(2 * (PAIRS_PER_TRIP * t + u))
        return carry

    assert n_past % 2 == 1
    n_pairs = n_past // 2
    lax.fori_loop(0, n_pairs // PAIRS_PER_TRIP, pairs, 0)
    for u in range(n_pairs - n_pairs % PAIRS_PER_TRIP, n_pairs):
        pair(2 * u)
    step(absorb=(0, *past_item(n_past - 1)))
    for slot in range(2):
        outs = [acc_ref[slot, hd, :HEAD_DIM] / acc_ref[slot, hd, HEAD_DIM:HEAD_DIM + 1]
                for hd in range(2)]
        o_ref[0, slot, 0] = jnp.concatenate(outs, axis=0).T.astype(o_ref.dtype)


def _moba(q_aug, k_aug, v_aug):
    bsz, seq, wide = k_aug.shape
    pairs = wide // (2 * LANES)
    n_chunks = seq // CHUNK
    assert n_chunks % 2 == 0
    q_spec = lambda tile: pl.BlockSpec((1, 2 * LANES, CHUNK), lambda b, p, g: (b, p, tile(g)))
    return pl.pallas_call(
        functools.partial(_moba_kernel, n_chunks),
        grid=(bsz, pairs, n_chunks // 2),
        in_specs=[q_spec(lambda g: g),
                  q_spec(lambda g: n_chunks - 1 - g),
                  pl.BlockSpec((1, seq, 2 * LANES), lambda b, p, g: (b, 0, p)),
                  pl.BlockSpec((1, 2 * V_ROWS, seq), lambda b, p, g: (b, p, 0))],
        out_specs=pl.BlockSpec((1, 2, 1, CHUNK, LANES), lambda b, p, g: (b, 0, g, 0, p)),
        out_shape=jax.ShapeDtypeStruct((bsz, 2, n_chunks // 2, CHUNK, pairs * LANES), BF16),
        scratch_shapes=[pltpu.VMEM((2, 2 * LANES, CHUNK), BF16),
                        pltpu.VMEM((2, 2, CHUNK, CHUNK), BF16),
                        pltpu.VMEM((2, 2, 1, CHUNK), F32),
                        pltpu.VMEM((2, 2, 1, CHUNK), F32),
                        pltpu.VMEM((2, 2, V_ROWS, CHUNK), F32)],
        compiler_params=pltpu.CompilerParams(
            dimension_semantics=("arbitrary", "arbitrary", "arbitrary"),
            vmem_limit_bytes=VMEM_LIMIT),
        name="moba",
    )(q_aug, q_aug, k_aug, v_aug)


def _out_proj_kernel(ya_ref, gz_ref, yc_ref, x_ref, mod_ref, gattn_ref, wout_ref, gfin_ref, o_ref):
    ya = ya_ref[0, 0, 0].astype(F32)
    yn = ya * lax.rsqrt(jnp.mean(ya * ya, axis=-1, keepdims=True) + EPS) * gattn_ref[...]
    yn = (yn * gz_ref[0].astype(F32)).astype(BF16)
    y = (jnp.dot(yn, wout_ref[0:D_ATTN, :], preferred_element_type=F32)
         + jnp.dot(yc_ref[0], wout_ref[D_ATTN:, :], preferred_element_type=F32))
    xo = x_ref[0] + mod_ref[0, 2:3, :] * y
    o_ref[0] = xo * lax.rsqrt(jnp.mean(xo * xo, axis=-1, keepdims=True) + EPS) * gfin_ref[...]


def _out_proj(y_attn, gz, yc, x, mod3, g_attn_out, w_out, g_final):
    bsz, seq, _ = x.shape
    n_tiles = seq // OUT_TILE
    assert OUT_TILE == CHUNK and y_attn.shape == (bsz, 2, n_tiles // 2, CHUNK, D_ATTN)
    row_spec = lambda width: pl.BlockSpec((1, OUT_TILE, width), lambda b, i: (b, i, 0))
    const = lambda shape: pl.BlockSpec(shape, lambda b, i: (0,) * len(shape))
    return pl.pallas_call(
        _out_proj_kernel,
        grid=(bsz, seq // OUT_TILE),
        in_specs=[pl.BlockSpec((1, 1, 1, OUT_TILE, D_ATTN),
                               lambda b, i: (b, jnp.where(i < n_tiles // 2, 0, 1),
                                             jnp.minimum(i, n_tiles - 1 - i), 0, 0)),
                  row_spec(D_ATTN), row_spec(D_CONV), row_spec(D_MODEL),
                  pl.BlockSpec((1, 3, D_MODEL), lambda b, i: (b, 0, 0)),
                  const((1, D_ATTN)), const(w_out.shape), const((1, D_MODEL))],
        out_specs=row_spec(D_MODEL),
        out_shape=jax.ShapeDtypeStruct((bsz, seq, D_MODEL), F32),
        compiler_params=pltpu.CompilerParams(dimension_semantics=("arbitrary", "arbitrary"),
                                             vmem_limit_bytes=VMEM_LIMIT),
        name="out_proj",
    )(y_attn, gz, yc, x, mod3, g_attn_out.reshape(1, D_ATTN), w_out, g_final.reshape(1, D_MODEL))


def kernel(x, c, positions, w_ada, b_ada, g_norm, w_in, w_conv, g_attn_out, g_conv_out, w_out, g_final):
    bsz, seq, _ = x.shape
    assert seq % TILE == 0 and seq % OUT_TILE == 0 and seq // TILE <= HEAD_DIM // 2
    mod3 = _adaln(c, w_ada, b_ada).reshape(bsz, 3, D_MODEL)

    inv_freq = ROPE_THETA ** (-jnp.arange(HALF, dtype=F32) / HALF)
    inv_freq = jnp.tile(inv_freq, LANES // HALF).reshape(1, LANES)

    q_aug, k_aug, v_aug, gz, yc = _in_proj(x, mod3, g_norm, positions, inv_freq,
                                           w_in.astype(BF16), w_conv, g_conv_out)
    y_attn = _moba(q_aug, k_aug, v_aug)
    return _out_proj(y_attn, gz, yc, x, mod3, g_attn_out, w_out.astype(BF16), g_final)
```

```python
import functools

import jax
import jax.numpy as jnp
from jax import lax
from jax.experimental import pallas as pl
from jax.experimental.pallas import tpu as pltpu

D_MODEL = 1024
D_ATTN = 512
D_CONV = 512
N_HEADS = 8
HEAD_DIM = 64
HALF = HEAD_DIM // 2
CONV_WIDTH = 3
MOBA_BLOCK = 256
MOBA_TOPK = 3
ROPE_THETA = 10000.0
EPS = 1e-6
NEG = -1e30
LOG2E = 1.4426950408889634

LANES = 128
V_ROWS = HEAD_DIM + 16
TILE = MOBA_BLOCK
CHUNK = 2 * TILE
TILES_PER_STEP = 2
MXU_COLS = 256
PAIRS_PER_TRIP = 3
OUT_TILE = 512
VMEM_LIMIT = 56 * 1024 * 1024

F32 = jnp.float32
BF16 = jnp.bfloat16


def _silu(z):
    return z * (1.0 / (1.0 + jnp.exp(-z)))


def _split_bf16(a):
    hi = a.astype(BF16)
    lo = (a - hi.astype(F32)).astype(BF16)
    return hi, lo


def _adaln_kernel(c_ref, w_ref, b_ref, o_ref):
    act = _silu(c_ref[...])
    rows = [jnp.sum(act[:, b:b + 1] * w_ref[...], axis=0, keepdims=True)
            for b in range(act.shape[1])]
    o_ref[...] = jnp.concatenate(rows, axis=0) + b_ref[...]


def _adaln(c, w_ada, b_ada):
    bsz = c.shape[0]
    n = w_ada.shape[1]
    bn = D_MODEL
    return pl.pallas_call(
        _adaln_kernel,
        grid=(n // bn,),
        in_specs=[pl.BlockSpec((D_MODEL, bsz), lambda j: (0, 0)),
                  pl.BlockSpec((D_MODEL, bn), lambda j: (0, j)),
                  pl.BlockSpec((1, bn), lambda j: (0, j))],
        out_specs=pl.BlockSpec((bsz, bn), lambda j: (0, j)),
        out_shape=jax.ShapeDtypeStruct((bsz, n), F32),
        compiler_params=pltpu.CompilerParams(dimension_semantics=("arbitrary",),
                                             vmem_limit_bytes=VMEM_LIMIT),
        name="adaln",
    )(c.T, w_ada, b_ada.reshape(1, n))


def _in_proj_kernel(n_blocks, x_ref, mod_ref, gnorm_ref, pos_ref, freq_ref, whi_ref,
                    wconv_ref, gconv_ref,
                    q_ref, k_ref, v_ref, gz_ref, yc_ref,
                    kbd_ref, carry_ref, cos_ref, sin_ref, proj_ref, h_ref):
    step = pl.program_id(1)

    @pl.when(step == 0)
    def _():
        kbd_ref[...] = jnp.zeros_like(kbd_ref)
        carry_ref[...] = jnp.zeros_like(carry_ref)

    n_cols = whi_ref.shape[1]
    col_groups = [(0, 2 * D_ATTN), (2 * D_ATTN, 3 * D_ATTN), (3 * D_ATTN, 4 * D_ATTN),
                  (4 * D_ATTN, n_cols)]

    def norm(r):
        x = x_ref[0, r * TILE:(r + 1) * TILE, :]
        xn = x * lax.rsqrt(jnp.mean(x * x, axis=-1, keepdims=True) + EPS) * gnorm_ref[...]
        h_ref[r] = (xn * (1.0 + mod_ref[0, 1:2, :]) + mod_ref[0, 0:1, :]).astype(BF16)

    def project(r, group):
        lo, hi = col_groups[group]
        for c0 in range(lo, hi, D_ATTN):
            proj_ref[r * TILE:(r + 1) * TILE, c0:c0 + D_ATTN] = jnp.dot(
                h_ref[r], whi_ref[:, c0:c0 + D_ATTN], preferred_element_type=F32)

    posts = [_block_posts(n_blocks, r, step * TILES_PER_STEP + r, pos_ref, freq_ref, wconv_ref,
                          gconv_ref, q_ref, k_ref, v_ref, gz_ref, yc_ref, kbd_ref, carry_ref,
                          cos_ref, sin_ref, proj_ref)
             for r in range(TILES_PER_STEP)]
    stages = [(r, group) for r in range(TILES_PER_STEP) for group in range(len(col_groups))]
    norm(0)
    project(*stages[0])
    for s, (r, group) in enumerate(stages):
        if s + 1 < len(stages):
            if stages[s + 1][1] == 0:
                norm(stages[s + 1][0])
            project(*stages[s + 1])
        posts[r][group]()


def _block_posts(n_blocks, r, i, pos_ref, freq_ref, wconv_ref, gconv_ref,
                 q_ref, k_ref, v_ref, gz_ref, yc_ref, kbd_ref, carry_ref, cos_ref, sin_ref,
                 proj_ref):
    rows = slice(r * TILE, (r + 1) * TILE)

    def proj(c0, c1):
        return proj_ref[rows, c0:c1]

    def post_qk():
        lane_c = lax.broadcasted_iota(jnp.int32, (TILE // 4, LANES), 1)
        ang = pos_ref[0, r * TILE // 4:(r + 1) * TILE // 4, :].astype(F32) * freq_ref[...]
        sign = jnp.where((lane_c % HEAD_DIM) < HALF, -1.0, 1.0)
        for table_ref, table, scale in ((cos_ref, jnp.cos(ang), None), (sin_ref, jnp.sin(ang), sign)):
            for a in range(LANES // HALF):
                z = jnp.where(lane_c // HALF == a, table, 0.0)
                z = z + pltpu.roll(z, 2 * HALF, 1)
                z = z + pltpu.roll(z, HALF, 1)
                table_ref[r, pl.ds(a, TILE // 4, stride=LANES // HALF), :] = (
                    z if scale is None else z * scale)
        cos = cos_ref[r]
        sin = sin_ref[r]

        lane = lax.broadcasted_iota(jnp.int32, (TILE, LANES), 1)
        first_half = (lane % HEAD_DIM) < HALF

        def rope(t):
            partner = jnp.where(first_half, pltpu.roll(t, LANES - HALF, 1), pltpu.roll(t, HALF, 1))
            return t * cos + partner * sin

        n_groups = D_ATTN // LANES
        q_groups = [rope(proj(g * LANES, (g + 1) * LANES)) for g in range(n_groups)]
        k_groups = [rope(proj(D_ATTN + g * LANES, D_ATTN + (g + 1) * LANES)) for g in range(n_groups)]
        q = jnp.concatenate(q_groups, axis=1)
        k = jnp.concatenate(k_groups, axis=1)

        q_hi, q_lo = _split_bf16(q)
        kbd_hi, kbd_lo = _split_bf16(kbd_ref[...])
        nt = (((1,), (1,)), ((), ()))
        scores = (lax.dot_general(kbd_hi, q_hi, nt, preferred_element_type=F32)
                  + lax.dot_general(kbd_lo, q_hi, nt, preferred_element_type=F32)
                  + lax.dot_general(kbd_hi, q_lo, nt, preferred_element_type=F32))
        g3 = scores.reshape(N_HEADS, n_blocks, TILE)
        blk = lax.broadcasted_iota(jnp.int32, g3.shape, 1)
        past = blk < i
        g3 = jnp.where(past, g3, NEG)
        picked = jnp.zeros(g3.shape, jnp.bool_)
        for _ in range(min(MOBA_TOPK, n_blocks - 1)):
            top = jnp.max(g3, axis=1, keepdims=True)
            first = jnp.min(jnp.where(g3 == top, blk, n_blocks), axis=1, keepdims=True)
            hit = blk == first
            picked = jnp.logical_or(picked, hit)
            g3 = jnp.where(hit, -jnp.inf, g3)
        keep = jnp.logical_or(jnp.logical_and(picked, past), blk == i)
        bias = jnp.where(keep, 0.0, NEG)

        k_mean = jnp.mean(k, axis=0, keepdims=True)
        lane_w = lax.broadcasted_iota(jnp.int32, (1, D_ATTN), 1)
        for hd in range(N_HEADS):
            own = (lane_w // HEAD_DIM) == hd
            kbd_ref[pl.ds(hd * n_blocks + i, 1), :] = jnp.where(own, k_mean, 0.0)

        q_t = (q * (HEAD_DIM ** -0.5 * LOG2E)).T
        pad = jnp.zeros((HEAD_DIM - n_blocks, TILE), F32)
        parts = []
        for hd in range(N_HEADS):
            dims = q_t[hd * HEAD_DIM:(hd + 1) * HEAD_DIM]
            parts += [dims, bias[hd], pad] if hd % 2 == 0 else [bias[hd], pad, dims]
        q_ref[0, :, rows] = jnp.concatenate(parts, axis=0).astype(BF16)

        low = lane < HEAD_DIM
        onehot_even = jnp.where(lane == HEAD_DIM + i, 1.0, 0.0)
        onehot_odd = jnp.where(lane == i, 1.0, 0.0)
        for g in range(n_groups):
            even = slice(2 * g * LANES, (2 * g + 1) * LANES)
            odd = slice((2 * g + 1) * LANES, (2 * g + 2) * LANES)
            k_ref[0, rows, even] = jnp.where(low, k_groups[g], onehot_even).astype(BF16)
            k_ref[0, rows, odd] = jnp.where(low, onehot_odd, k_groups[g]).astype(BF16)

    def post_v():
        v_t = proj(2 * D_ATTN, 3 * D_ATTN).T
        ones = jnp.ones((V_ROWS - HEAD_DIM, TILE), F32)
        parts = []
        for hd in range(N_HEADS):
            parts += [v_t[hd * HEAD_DIM:(hd + 1) * HEAD_DIM], ones]
        v_ref[0, :, rows] = jnp.concatenate(parts, axis=0).astype(BF16)

    def post_gz():
        gz_ref[0, rows, :] = _silu(proj(3 * D_ATTN, 4 * D_ATTN)).astype(BF16)

    def post_conv():
        c0 = 4 * D_ATTN
        b_g = proj(c0, c0 + D_CONV)
        u = proj(c0 + D_CONV, c0 + 2 * D_CONV) * proj(c0 + 2 * D_CONV, c0 + 3 * D_CONV)
        z_c = proj(c0 + 3 * D_CONV, c0 + 4 * D_CONV)
        row = lax.broadcasted_iota(jnp.int32, u.shape, 0)
        prev1 = carry_ref[7:8, :]
        prev2 = carry_ref[6:7, :]
        u1 = jnp.where(row == 0, prev1, pltpu.roll(u, 1, 0))
        u2 = jnp.where(row == 0, prev2, jnp.where(row == 1, prev1, pltpu.roll(u, 2, 0)))
        carry_ref[...] = u[TILE - 8:TILE, :]
        y = b_g * (wconv_ref[2:3, :] * u + wconv_ref[1:2, :] * u1 + wconv_ref[0:1, :] * u2)
        y = y * lax.rsqrt(jnp.mean(y * y, axis=-1, keepdims=True) + EPS) * gconv_ref[...]
        yc_ref[0, rows, :] = (y * _silu(z_c)).astype(BF16)

    return [post_qk, post_v, post_gz, post_conv]


def _in_proj(x, mod3, g_norm, positions, inv_freq, w_hi, w_conv, g_conv_out):
    bsz, seq, _ = x.shape
    n_blocks = seq // TILE
    step_rows = TILES_PER_STEP * TILE
    wide = N_HEADS * LANES
    row_spec = lambda width: pl.BlockSpec((1, step_rows, width), lambda b, i: (b, i, 0))
    col_spec = lambda height: pl.BlockSpec((1, height, step_rows), lambda b, i: (b, 0, i))
    const = lambda shape: pl.BlockSpec(shape, lambda b, i: (0,) * len(shape))
    kbd_rows = N_HEADS * n_blocks
    per_row = LANES // HALF
    pos_compact = jnp.repeat(positions.reshape(bsz, seq // per_row, per_row), HALF, axis=2)
    return pl.pallas_call(
        functools.partial(_in_proj_kernel, n_blocks),
        grid=(bsz, n_blocks // TILES_PER_STEP),
        in_specs=[row_spec(D_MODEL),
                  pl.BlockSpec((1, 3, D_MODEL), lambda b, i: (b, 0, 0)),
                  const((1, D_MODEL)),
                  pl.BlockSpec((1, step_rows // per_row, LANES), lambda b, i: (b, i, 0)),
                  const((1, LANES)),
                  const(w_hi.shape),
                  const((CONV_WIDTH, D_CONV)),
                  const((1, D_CONV))],
        out_specs=[col_spec(wide), row_spec(wide), col_spec(N_HEADS * V_ROWS),
                   row_spec(D_ATTN), row_spec(D_CONV)],
        out_shape=[jax.ShapeDtypeStruct((bsz, wide, seq), BF16),
                   jax.ShapeDtypeStruct((bsz, seq, wide), BF16),
                   jax.ShapeDtypeStruct((bsz, N_HEADS * V_ROWS, seq), BF16),
                   jax.ShapeDtypeStruct((bsz, seq, D_ATTN), BF16),
                   jax.ShapeDtypeStruct((bsz, seq, D_CONV), BF16)],
        scratch_shapes=[pltpu.VMEM((kbd_rows, D_ATTN), F32),
                        pltpu.VMEM((8, D_CONV), F32),
                        pltpu.VMEM((TILES_PER_STEP, TILE, LANES), F32),
                        pltpu.VMEM((TILES_PER_STEP, TILE, LANES), F32),
                        pltpu.VMEM((step_rows, w_hi.shape[1]), F32),
                        pltpu.VMEM((TILES_PER_STEP, TILE, D_MODEL), BF16)],
        compiler_params=pltpu.CompilerParams(dimension_semantics=("arbitrary", "arbitrary"),
                                             vmem_limit_bytes=VMEM_LIMIT),
        name="in_proj",
    )(x, mod3, g_norm.reshape(1, D_MODEL), pos_compact, inv_freq,
      w_hi, w_conv, g_conv_out.reshape(1, D_CONV))


def _moba_kernel(n_chunks, qa_ref, qb_ref, k_ref, v_ref, o_ref, q_ref, s_ref, smax_ref, m_ref,
                 acc_ref):
    g = pl.program_id(2)
    own = (g, n_chunks - 1 - g)
    n_past = n_chunks - 1
    q_ref[0] = qa_ref[0]
    q_ref[1] = qb_ref[0]

    def past_item(j):
        in_first = j < own[0]
        return jnp.where(in_first, 0, 1), jnp.where(in_first, j, j - own[0])

    def issue_unit(buf, slot, chunk, hd, cols):
        start = pl.multiple_of(chunk * CHUNK, CHUNK)
        rows = slice(hd * LANES, (hd + 1) * LANES)
        kj = k_ref[0, pl.ds(start, CHUNK), rows]
        s = jnp.dot(kj, q_ref[slot, rows, cols], preferred_element_type=F32)
        s_ref[buf, hd, :, cols] = s.astype(BF16)
        smax_ref[buf, hd, :, cols] = jnp.max(s, axis=0, keepdims=True)

    def absorb_unit(buf, slot, chunk, hd, cols, causal):
        start = pl.multiple_of(chunk * CHUNK, CHUNK)
        s = s_ref[buf, hd, :, cols]
        smax = smax_ref[buf, hd, :, cols]
        if causal:
            key = lax.broadcasted_iota(jnp.int32, s.shape, 0)
            qry = lax.broadcasted_iota(jnp.int32, s.shape, 1) + cols.start
            s = jnp.where(key <= qry, s.astype(F32), NEG)
            smax = jnp.max(s, axis=0, keepdims=True)
        m = m_ref[slot, hd, :, cols]
        m_new = jnp.maximum(m, smax).astype(BF16)
        p = jnp.exp2(s.astype(BF16) - m_new)
        m_new = m_new.astype(F32)
        vj = v_ref[0, hd * V_ROWS:(hd + 1) * V_ROWS, pl.ds(start, CHUNK)]
        acc_ref[slot, hd, :, cols] = (jnp.exp2(m - m_new) * acc_ref[slot, hd, :, cols]
                                      + jnp.dot(vj, p, preferred_element_type=F32))
        m_ref[slot, hd, :, cols] = m_new

    def step(issue=None, absorb=None, causal=False):
        for hd in range(2):
            for half in range(CHUNK // MXU_COLS):
                cols = slice(half * MXU_COLS, (half + 1) * MXU_COLS)
                if issue is not None:
                    issue_unit(*issue, hd, cols)
                if absorb is not None:
                    absorb_unit(*absorb, hd, cols, causal)

    m_ref[...] = jnp.full(m_ref.shape, -jnp.inf, F32)
    acc_ref[...] = jnp.zeros(acc_ref.shape, F32)
    step(issue=(0, 0, own[0]))
    step(issue=(1, 1, own[1]), absorb=(0, 0, own[0]), causal=True)
    step(issue=(0, *past_item(0)), absorb=(1, 1, own[1]), causal=True)

    def pair(j):
        step(issue=(1, *past_item(j + 1)), absorb=(0, *past_item(j)))
        step(issue=(0, *past_item(j + 2)), absorb=(1, *past_item(j + 1)))

    def pairs(t, carry):
        for u in range(PAIRS_PER_TRIP):
            pair(2 * (PAIRS_PER_TRIP * t + u))
        return carry

    assert n_past % 2 == 1
    n_pairs = n_past // 2
    lax.fori_loop(0, n_pairs // PAIRS_PER_TRIP, pairs, 0)
    for u in range(n_pairs - n_pairs % PAIRS_PER_TRIP, n_pairs):
        pair(2 * u)
    step(absorb=(0, *past_item(n_past - 1)))
    for slot in range(2):
        outs = [acc_ref[slot, hd, :HEAD_DIM] / acc_ref[slot, hd, HEAD_DIM:HEAD_DIM + 1]
                for hd in range(2)]
        o_ref[0, slot, 0] = jnp.concatenate(outs, axis=0).T.astype(o_ref.dtype)


def _moba(q_aug, k_aug, v_aug):
    bsz, seq, wide = k_aug.shape
    pairs = wide // (2 * LANES)
    n_chunks = seq // CHUNK
    assert n_chunks % 2 == 0
    q_spec = lambda tile: pl.BlockSpec((1, 2 * LANES, CHUNK), lambda b, p, g: (b, p, tile(g)))
    return pl.pallas_call(
        functools.partial(_moba_kernel, n_chunks),
        grid=(bsz, pairs, n_chunks // 2),
        in_specs=[q_spec(lambda g: g),
                  q_spec(lambda g: n_chunks - 1 - g),
                  pl.BlockSpec((1, seq, 2 * LANES), lambda b, p, g: (b, 0, p)),
                  pl.BlockSpec((1, 2 * V_ROWS, seq), lambda b, p, g: (b, p, 0))],
        out_specs=pl.BlockSpec((1, 2, 1, CHUNK, LANES), lambda b, p, g: (b, 0, g, 0, p)),
        out_shape=jax.ShapeDtypeStruct((bsz, 2, n_chunks // 2, CHUNK, pairs * LANES), BF16),
        scratch_shapes=[pltpu.VMEM((2, 2 * LANES, CHUNK), BF16),
                        pltpu.VMEM((2, 2, CHUNK, CHUNK), BF16),
                        pltpu.VMEM((2, 2, 1, CHUNK), F32),
                        pltpu.VMEM((2, 2, 1, CHUNK), F32),
                        pltpu.VMEM((2, 2, V_ROWS, CHUNK), F32)],
        compiler_params=pltpu.CompilerParams(
            dimension_semantics=("arbitrary", "arbitrary", "arbitrary"),
            vmem_limit_bytes=VMEM_LIMIT),
        name="moba",
    )(q_aug, q_aug, k_aug, v_aug)


def _out_proj_kernel(ya_ref, gz_ref, yc_ref, x_ref, mod_ref, gattn_ref, wout_ref, gfin_ref, o_ref):
    ya = ya_ref[0, 0, 0].astype(F32)
    yn = ya * lax.rsqrt(jnp.mean(ya * ya, axis=-1, keepdims=True) + EPS) * gattn_ref[...]
    yn = (yn * gz_ref[0].astype(F32)).astype(BF16)
    y = (jnp.dot(yn, wout_ref[0:D_ATTN, :], preferred_element_type=F32)
         + jnp.dot(yc_ref[0], wout_ref[D_ATTN:, :], preferred_element_type=F32))
    xo = x_ref[0] + mod_ref[0, 2:3, :] * y
    o_ref[0] = xo * lax.rsqrt(jnp.mean(xo * xo, axis=-1, keepdims=True) + EPS) * gfin_ref[...]


def _out_proj(y_attn, gz, yc, x, mod3, g_attn_out, w_out, g_final):
    bsz, seq, _ = x.shape
    n_tiles = seq // OUT_TILE
    assert OUT_TILE == CHUNK and y_attn.shape == (bsz, 2, n_tiles // 2, CHUNK, D_ATTN)
    row_spec = lambda width: pl.BlockSpec((1, OUT_TILE, width), lambda b, i: (b, i, 0))
    const = lambda shape: pl.BlockSpec(shape, lambda b, i: (0,) * len(shape))
    return pl.pallas_call(
        _out_proj_kernel,
        grid=(bsz, seq // OUT_TILE),
        in_specs=[pl.BlockSpec((1, 1, 1, OUT_TILE, D_ATTN),
                               lambda b, i: (b, jnp.where(i < n_tiles // 2, 0, 1),
                                             jnp.minimum(i, n_tiles - 1 - i), 0, 0)),
                  row_spec(D_ATTN), row_spec(D_CONV), row_spec(D_MODEL),
                  pl.BlockSpec((1, 3, D_MODEL), lambda b, i: (b, 0, 0)),
                  const((1, D_ATTN)), const(w_out.shape), const((1, D_MODEL))],
        out_specs=row_spec(D_MODEL),
        out_shape=jax.ShapeDtypeStruct((bsz, seq, D_MODEL), F32),
        compiler_params=pltpu.CompilerParams(dimension_semantics=("arbitrary", "arbitrary"),
                                             vmem_limit_bytes=VMEM_LIMIT),
        name="out_proj",
    )(y_attn, gz, yc, x, mod3, g_attn_out.reshape(1, D_ATTN), w_out, g_final.reshape(1, D_MODEL))


def kernel(x, c, positions, w_ada, b_ada, g_norm, w_in, w_conv, g_attn_out, g_conv_out, w_out, g_final):
    bsz, seq, _ = x.shape
    assert seq % (TILES_PER_STEP * TILE) == 0 and seq % OUT_TILE == 0 and seq // TILE <= HEAD_DIM // 2
    mod3 = _adaln(c, w_ada, b_ada).reshape(bsz, 3, D_MODEL)

    inv_freq = ROPE_THETA ** (-jnp.arange(HALF, dtype=F32) / HALF)
    inv_freq = jnp.tile(inv_freq, LANES // HALF).reshape(1, LANES)

    q_aug, k_aug, v_aug, gz, yc = _in_proj(x, mod3, g_norm, positions, inv_freq,
                                           w_in.astype(BF16), w_conv, g_conv_out)
    y_attn = _moba(q_aug, k_aug, v_aug)
    return _out_proj(y_attn, gz, yc, x, mod3, g_attn_out, w_out.astype(BF16), g_final)
```

```python
import functools

import jax
import jax.numpy as jnp
from jax import lax
from jax.experimental import pallas as pl
from jax.experimental.pallas import tpu as pltpu

D_MODEL = 1024
D_ATTN = 512
D_CONV = 512
N_HEADS = 8
HEAD_DIM = 64
HALF = HEAD_DIM // 2
CONV_WIDTH = 3
MOBA_BLOCK = 256
MOBA_TOPK = 3
ROPE_THETA = 10000.0
EPS = 1e-6
NEG = -1e30
LOG2E = 1.4426950408889634

LANES = 128
V_ROWS = HEAD_DIM + 16
TILE = MOBA_BLOCK
CHUNK = 2 * TILE
TILES_PER_STEP = 2
MXU_COLS = 256
PAIRS_PER_TRIP = 3
OUT_TILE = 512
VMEM_LIMIT = 56 * 1024 * 1024

F32 = jnp.float32
BF16 = jnp.bfloat16


def _silu(z):
    return z * (1.0 / (1.0 + jnp.exp(-z)))


def _split_bf16(a):
    hi = a.astype(BF16)
    lo = (a - hi.astype(F32)).astype(BF16)
    return hi, lo


def _adaln_kernel(c_ref, w_ref, b_ref, o_ref):
    act = _silu(c_ref[...])
    rows = [jnp.sum(act[:, b:b + 1] * w_ref[...], axis=0, keepdims=True)
            for b in range(act.shape[1])]
    o_ref[...] = jnp.concatenate(rows, axis=0) + b_ref[...]


def _adaln(c, w_ada, b_ada):
    bsz = c.shape[0]
    n = w_ada.shape[1]
    bn = D_MODEL
    return pl.pallas_call(
        _adaln_kernel,
        grid=(n // bn,),
        in_specs=[pl.BlockSpec((D_MODEL, bsz), lambda j: (0, 0)),
                  pl.BlockSpec((D_MODEL, bn), lambda j: (0, j)),
                  pl.BlockSpec((1, bn), lambda j: (0, j))],
        out_specs=pl.BlockSpec((bsz, bn), lambda j: (0, j)),
        out_shape=jax.ShapeDtypeStruct((bsz, n), F32),
        compiler_params=pltpu.CompilerParams(dimension_semantics=("arbitrary",),
                                             vmem_limit_bytes=VMEM_LIMIT),
        name="adaln",
    )(c.T, w_ada, b_ada.reshape(1, n))


def _in_proj_kernel(n_blocks, x_ref, mod_ref, gnorm_ref, pos_ref, freq_ref, whi_ref,
                    wconv_ref, gconv_ref,
                    q_ref, k_ref, v_ref, gz_ref, yc_ref,
                    kbd_ref, carry_ref, cos_ref, sin_ref, proj_ref, h_ref):
    step = pl.program_id(1)

    @pl.when(step == 0)
    def _():
        kbd_ref[...] = jnp.zeros_like(kbd_ref)
        carry_ref[...] = jnp.zeros_like(carry_ref)

    n_cols = whi_ref.shape[1]
    col_groups = [(0, 2 * D_ATTN), (2 * D_ATTN, 3 * D_ATTN), (3 * D_ATTN, 4 * D_ATTN),
                  (4 * D_ATTN, n_cols)]

    def norm(r):
        x = x_ref[0, r * TILE:(r + 1) * TILE, :]
        xn = x * lax.rsqrt(jnp.mean(x * x, axis=-1, keepdims=True) + EPS) * gnorm_ref[...]
        h_ref[r] = (xn * (1.0 + mod_ref[0, 1:2, :]) + mod_ref[0, 0:1, :]).astype(BF16)

    def project(r, group):
        lo, hi = col_groups[group]
        for c0 in range(lo, hi, D_ATTN):
            proj_ref[r * TILE:(r + 1) * TILE, c0:c0 + D_ATTN] = jnp.dot(
                h_ref[r], whi_ref[:, c0:c0 + D_ATTN], preferred_element_type=F32)

    posts = [_block_posts(n_blocks, r, step * TILES_PER_STEP + r, pos_ref, freq_ref, wconv_ref,
                          gconv_ref, q_ref, k_ref, v_ref, gz_ref, yc_ref, kbd_ref, carry_ref,
                          cos_ref, sin_ref, proj_ref)
             for r in range(TILES_PER_STEP)]
    stages = [(r, group) for r in range(TILES_PER_STEP) for group in range(len(col_groups))]
    norm(0)
    project(*stages[0])
    for s, (r, group) in enumerate(stages):
        if s + 1 < len(stages):
            if stages[s + 1][1] == 0:
                norm(stages[s + 1][0])
            project(*stages[s + 1])
        posts[r][group]()


def _block_posts(n_blocks, r, i, pos_ref, freq_ref, wconv_ref, gconv_ref,
                 q_ref, k_ref, v_ref, gz_ref, yc_ref, kbd_ref, carry_ref, cos_ref, sin_ref,
                 proj_ref):
    rows = slice(r * TILE, (r + 1) * TILE)

    def proj(c0, c1):
        return proj_ref[rows, c0:c1]

    def post_qk():
        lane_c = lax.broadcasted_iota(jnp.int32, (TILE // 4, LANES), 1)
        ang = pos_ref[0, r * TILE // 4:(r + 1) * TILE // 4, :].astype(F32) * freq_ref[...]
        sign = jnp.where((lane_c % HEAD_DIM) < HALF, -1.0, 1.0)
        for table_ref, table, scale in ((cos_ref, jnp.cos(ang), None), (sin_ref, jnp.sin(ang), sign)):
            for a in range(LANES // HALF):
                z = jnp.where(lane_c // HALF == a, table, 0.0)
                z = z + pltpu.roll(z, 2 * HALF, 1)
                z = z + pltpu.roll(z, HALF, 1)
                table_ref[r, pl.ds(a, TILE // 4, stride=LANES // HALF), :] = (
                    z if scale is None else z * scale)
        cos = cos_ref[r]
        sin = sin_ref[r]

        lane = lax.broadcasted_iota(jnp.int32, (TILE, LANES), 1)
        first_half = (lane % HEAD_DIM) < HALF

        def rope(t):
            partner = jnp.where(first_half, pltpu.roll(t, LANES - HALF, 1), pltpu.roll(t, HALF, 1))
            return t * cos + partner * sin

        n_groups = D_ATTN // LANES
        q_groups = [rope(proj(g * LANES, (g + 1) * LANES)) for g in range(n_groups)]
        k_groups = [rope(proj(D_ATTN + g * LANES, D_ATTN + (g + 1) * LANES)) for g in range(n_groups)]
        q = jnp.concatenate(q_groups, axis=1)
        k = jnp.concatenate(k_groups, axis=1)

        q_hi, q_lo = _split_bf16(q)
        kbd_hi, kbd_lo = _split_bf16(kbd_ref[...])
        nt = (((1,), (1,)), ((), ()))
        scores = (lax.dot_general(kbd_hi, q_hi, nt, preferred_element_type=F32)
                  + lax.dot_general(kbd_lo, q_hi, nt, preferred_element_type=F32)
                  + lax.dot_general(kbd_hi, q_lo, nt, preferred_element_type=F32))
        g3 = scores.reshape(N_HEADS, n_blocks, TILE)
        blk = lax.broadcasted_iota(jnp.int32, g3.shape, 1)
        past = blk < i
        g3 = jnp.where(past, g3, NEG)
        picked = jnp.zeros(g3.shape, jnp.bool_)
        for _ in range(min(MOBA_TOPK, n_blocks - 1)):
            top = jnp.max(g3, axis=1, keepdims=True)
            first = jnp.min(jnp.where(g3 == top, blk, n_blocks), axis=1, keepdims=True)
            hit = blk == first
            picked = jnp.logical_or(picked, hit)
            g3 = jnp.where(hit, -jnp.inf, g3)
        keep = jnp.logical_or(jnp.logical_and(picked, past), blk == i)
        bias = jnp.where(keep, 0.0, NEG)

        k_mean = jnp.mean(k, axis=0, keepdims=True)
        lane_w = lax.broadcasted_iota(jnp.int32, (1, D_ATTN), 1)
        for hd in range(N_HEADS):
            own = (lane_w // HEAD_DIM) == hd
            kbd_ref[pl.ds(hd * n_blocks + i, 1), :] = jnp.where(own, k_mean, 0.0)

        q_t = (q * (HEAD_DIM ** -0.5 * LOG2E)).T
        pad = jnp.zeros((HEAD_DIM - n_blocks, TILE), F32)
        parts = []
        for hd in range(N_HEADS):
            dims = q_t[hd * HEAD_DIM:(hd + 1) * HEAD_DIM]
            parts += [dims, bias[hd], pad] if hd % 2 == 0 else [bias[hd], pad, dims]
        q_ref[0, :, rows] = jnp.concatenate(parts, axis=0).astype(BF16)

        low = lane < HEAD_DIM
        onehot_even = jnp.where(lane == HEAD_DIM + i, 1.0, 0.0)
        onehot_odd = jnp.where(lane == i, 1.0, 0.0)
        for g in range(n_groups):
            even = slice(2 * g * LANES, (2 * g + 1) * LANES)
            odd = slice((2 * g + 1) * LANES, (2 * g + 2) * LANES)
            k_ref[0, rows, even] = jnp.where(low, k_groups[g], onehot_even).astype(BF16)
            k_ref[0, rows, odd] = jnp.where(low, onehot_odd, k_groups[g]).astype(BF16)

    def post_v():
        v_t = proj(2 * D_ATTN, 3 * D_ATTN).T
        ones = jnp.ones((V_ROWS - HEAD_DIM, TILE), F32)
        parts = []
        for hd in range(N_HEADS):
            parts += [v_t[hd * HEAD_DIM:(hd + 1) * HEAD_DIM], ones]
        v_ref[0, :, rows] = jnp.concatenate(parts, axis=0).astype(BF16)

    def post_gz():
        gz_ref[0, rows, :] = _silu(proj(3 * D_ATTN, 4 * D_ATTN)).astype(BF16)

    def post_conv():
        c0 = 4 * D_ATTN
        b_g = proj(c0, c0 + D_CONV)
        u = proj(c0 + D_CONV, c0 + 2 * D_CONV) * proj(c0 + 2 * D_CONV, c0 + 3 * D_CONV)
        z_c = proj(c0 + 3 * D_CONV, c0 + 4 * D_CONV)
        row = lax.broadcasted_iota(jnp.int32, u.shape, 0)
        prev1 = carry_ref[7:8, :]
        prev2 = carry_ref[6:7, :]
        u1 = jnp.where(row == 0, prev1, pltpu.roll(u, 1, 0))
        u2 = jnp.where(row == 0, prev2, jnp.where(row == 1, prev1, pltpu.roll(u, 2, 0)))
        carry_ref[...] = u[TILE - 8:TILE, :]
        y = b_g * (wconv_ref[2:3, :] * u + wconv_ref[1:2, :] * u1 + wconv_ref[0:1, :] * u2)
        y = y * lax.rsqrt(jnp.mean(y * y, axis=-1, keepdims=True) + EPS) * gconv_ref[...]
        yc_ref[0, rows, :] = (y * _silu(z_c)).astype(BF16)

    return [post_qk, post_v, post_gz, post_conv]


def _in_proj(x, mod3, g_norm, positions, inv_freq, w_hi, w_conv, g_conv_out):
    bsz, seq, _ = x.shape
    n_blocks = seq // TILE
    step_rows = TILES_PER_STEP * TILE
    wide = N_HEADS * LANES
    row_spec = lambda width: pl.BlockSpec((1, step_rows, width), lambda b, i: (b, i, 0))
    col_spec = lambda height: pl.BlockSpec((1, height, step_rows), lambda b, i: (b, 0, i))
    const = lambda shape: pl.BlockSpec(shape, lambda b, i: (0,) * len(shape))
    kbd_rows = N_HEADS * n_blocks
    per_row = LANES // HALF
    pos_compact = jnp.repeat(positions.reshape(bsz, seq // per_row, per_row), HALF, axis=2)
    return pl.pallas_call(
        functools.partial(_in_proj_kernel, n_blocks),
        grid=(bsz, n_blocks // TILES_PER_STEP),
        in_specs=[row_spec(D_MODEL),
                  pl.BlockSpec((1, 3, D_MODEL), lambda b, i: (b, 0, 0)),
                  const((1, D_MODEL)),
                  pl.BlockSpec((1, step_rows // per_row, LANES), lambda b, i: (b, i, 0)),
                  const((1, LANES)),
                  const(w_hi.shape),
                  const((CONV_WIDTH, D_CONV)),
                  const((1, D_CONV))],
        out_specs=[col_spec(wide), row_spec(wide), col_spec(N_HEADS * V_ROWS),
                   row_spec(D_ATTN), row_spec(D_CONV)],
        out_shape=[jax.ShapeDtypeStruct((bsz, wide, seq), BF16),
                   jax.ShapeDtypeStruct((bsz, seq, wide), BF16),
                   jax.ShapeDtypeStruct((bsz, N_HEADS * V_ROWS, seq), BF16),
                   jax.ShapeDtypeStruct((bsz, seq, D_ATTN), BF16),
                   jax.ShapeDtypeStruct((bsz, seq, D_CONV), BF16)],
        scratch_shapes=[pltpu.VMEM((kbd_rows, D_ATTN), F32),
                        pltpu.VMEM((8, D_CONV), F32),
                        pltpu.VMEM((TILES_PER_STEP, TILE, LANES), F32),
                        pltpu.VMEM((TILES_PER_STEP, TILE, LANES), F32),
                        pltpu.VMEM((step_rows, w_hi.shape[1]), F32),
                        pltpu.VMEM((TILES_PER_STEP, TILE, D_MODEL), BF16)],
        compiler_params=pltpu.CompilerParams(dimension_semantics=("arbitrary", "arbitrary"),
                                             vmem_limit_bytes=VMEM_LIMIT),
        name="in_proj",
    )(x, mod3, g_norm.reshape(1, D_MODEL), pos_compact, inv_freq,
      w_hi, w_conv, g_conv_out.reshape(1, D_CONV))


def _moba_kernel(n_chunks, tile_ref, chunk_ref, q_ref, k_ref, v_ref, o_ref, s_ref, smax_ref,
                 m_ref, acc_ref):
    n_past = n_chunks * (n_chunks - 1) // 2

    def own_item(t):
        return t, t

    def past_item(j):
        return tile_ref[j], chunk_ref[j]

    def issue_unit(buf, tile, chunk, hd, half):
        start = pl.multiple_of(chunk * CHUNK, CHUNK)
        q_start = pl.multiple_of(tile * CHUNK + half * MXU_COLS, MXU_COLS)
        rows = slice(hd * LANES, (hd + 1) * LANES)
        cols = slice(half * MXU_COLS, (half + 1) * MXU_COLS)
        kj = k_ref[0, pl.ds(start, CHUNK), rows]
        s = jnp.dot(kj, q_ref[0, rows, pl.ds(q_start, MXU_COLS)],
                    preferred_element_type=F32)
        s_ref[buf, hd, :, cols] = s.astype(BF16)
        smax_ref[buf, hd, :, cols] = jnp.max(s, axis=0, keepdims=True)

    def absorb_unit(buf, tile, chunk, hd, half, causal):
        start = pl.multiple_of(chunk * CHUNK, CHUNK)
        cols = slice(half * MXU_COLS, (half + 1) * MXU_COLS)
        s = s_ref[buf, hd, :, cols]
        smax = smax_ref[buf, hd, :, cols]
        if causal:
            key = lax.broadcasted_iota(jnp.int32, s.shape, 0)
            qry = lax.broadcasted_iota(jnp.int32, s.shape, 1) + cols.start
            s = jnp.where(key <= qry, s.astype(F32), NEG)
            smax = jnp.max(s, axis=0, keepdims=True)
        m = m_ref[tile, hd, :, cols]
        m_new = jnp.maximum(m, smax).astype(BF16)
        p = jnp.exp2(s.astype(BF16) - m_new)
        m_new = m_new.astype(F32)
        vj = v_ref[0, hd * V_ROWS:(hd + 1) * V_ROWS, pl.ds(start, CHUNK)]
        acc_ref[tile, hd, :, cols] = (jnp.exp2(m - m_new) * acc_ref[tile, hd, :, cols]
                                      + jnp.dot(vj, p, preferred_element_type=F32))
        m_ref[tile, hd, :, cols] = m_new

    def step(issue=None, absorb=None, causal=False):
        for hd in range(2):
            for half in range(CHUNK // MXU_COLS):
                if issue is not None:
                    issue_unit(*issue, hd, half)
                if absorb is not None:
                    absorb_unit(*absorb, hd, half, causal)

    def run_pairs(item, n_pairs, causal):
        def pair(u):
            step(issue=(1, *item(2 * u + 1)), absorb=(0, *item(2 * u)), causal=causal)
            step(issue=(0, *item(2 * u + 2)), absorb=(1, *item(2 * u + 1)), causal=causal)

        def trip(t, carry):
            for u in range(PAIRS_PER_TRIP):
                pair(PAIRS_PER_TRIP * t + u)
            return carry

        trips = n_pairs // PAIRS_PER_TRIP
        lax.fori_loop(0, trips, trip, 0)
        for u in range(trips * PAIRS_PER_TRIP, n_pairs):
            pair(u)

    m_ref[...] = jnp.full(m_ref.shape, -jnp.inf, F32)
    acc_ref[...] = jnp.zeros(acc_ref.shape, F32)
    assert n_chunks % 2 == 0 and n_past % 2 == 0
    step(issue=(0, *own_item(0)))
    run_pairs(own_item, (n_chunks - 2) // 2, causal=True)
    step(issue=(1, *own_item(n_chunks - 1)), absorb=(0, *own_item(n_chunks - 2)), causal=True)
    step(issue=(0, *past_item(0)), absorb=(1, *own_item(n_chunks - 1)), causal=True)
    run_pairs(past_item, (n_past - 2) // 2, causal=False)
    step(issue=(1, *past_item(n_past - 1)), absorb=(0, *past_item(n_past - 2)))
    step(absorb=(1, *past_item(n_past - 1)))

    def finish(tile, carry):
        outs = [acc_ref[tile, hd, :HEAD_DIM] / acc_ref[tile, hd, HEAD_DIM:HEAD_DIM + 1]
                for hd in range(2)]
        start = pl.multiple_of(tile * CHUNK, CHUNK)
        o_ref[0, pl.ds(start, CHUNK), :] = jnp.concatenate(outs, axis=0).T.astype(o_ref.dtype)
        return carry

    lax.fori_loop(0, n_chunks, finish, 0)


def _moba(q_aug, k_aug, v_aug):
    bsz, seq, wide = k_aug.shape
    pairs = wide // (2 * LANES)
    n_chunks = seq // CHUNK
    past = [(t, c) for t in range(n_chunks) for c in range(t)]
    item_tile = jnp.asarray([t for t, _ in past], jnp.int32)
    item_chunk = jnp.asarray([c for _, c in past], jnp.int32)
    return pl.pallas_call(
        functools.partial(_moba_kernel, n_chunks),
        grid_spec=pltpu.PrefetchScalarGridSpec(
            num_scalar_prefetch=2,
            grid=(bsz, pairs),
            in_specs=[pl.BlockSpec((1, 2 * LANES, seq), lambda b, p, *_: (b, p, 0)),
                      pl.BlockSpec((1, seq, 2 * LANES), lambda b, p, *_: (b, 0, p)),
                      pl.BlockSpec((1, 2 * V_ROWS, seq), lambda b, p, *_: (b, p, 0))],
            out_specs=pl.BlockSpec((1, seq, LANES), lambda b, p, *_: (b, 0, p)),
            scratch_shapes=[pltpu.VMEM((2, 2, CHUNK, CHUNK), BF16),
                            pltpu.VMEM((2, 2, 1, CHUNK), F32),
                            pltpu.VMEM((n_chunks, 2, 1, CHUNK), F32),
                            pltpu.VMEM((n_chunks, 2, V_ROWS, CHUNK), F32)]),
        out_shape=jax.ShapeDtypeStruct((bsz, seq, pairs * LANES), BF16),
        compiler_params=pltpu.CompilerParams(
            dimension_semantics=("arbitrary", "arbitrary"),
            vmem_limit_bytes=VMEM_LIMIT),
        name="moba",
    )(item_tile, item_chunk, q_aug, k_aug, v_aug)


def _out_proj_kernel(ya_ref, gz_ref, yc_ref, x_ref, mod_ref, gattn_ref, wout_ref, gfin_ref, o_ref):
    ya = ya_ref[0].astype(F32)
    yn = ya * lax.rsqrt(jnp.mean(ya * ya, axis=-1, keepdims=True) + EPS) * gattn_ref[...]
    yn = (yn * gz_ref[0].astype(F32)).astype(BF16)
    y = (jnp.dot(yn, wout_ref[0:D_ATTN, :], preferred_element_type=F32)
         + jnp.dot(yc_ref[0], wout_ref[D_ATTN:, :], preferred_element_type=F32))
    xo = x_ref[0] + mod_ref[0, 2:3, :] * y
    o_ref[0] = xo * lax.rsqrt(jnp.mean(xo * xo, axis=-1, keepdims=True) + EPS) * gfin_ref[...]


def _out_proj(y_attn, gz, yc, x, mod3, g_attn_out, w_out, g_final):
    bsz, seq, _ = x.shape
    row_spec = lambda width: pl.BlockSpec((1, OUT_TILE, width), lambda b, i: (b, i, 0))
    const = lambda shape: pl.BlockSpec(shape, lambda b, i: (0,) * len(shape))
    return pl.pallas_call(
        _out_proj_kernel,
        grid=(bsz, seq // OUT_TILE),
        in_specs=[row_spec(D_ATTN), row_spec(D_ATTN), row_spec(D_CONV), row_spec(D_MODEL),
                  pl.BlockSpec((1, 3, D_MODEL), lambda b, i: (b, 0, 0)),
                  const((1, D_ATTN)), const(w_out.shape), const((1, D_MODEL))],
        out_specs=row_spec(D_MODEL),
        out_shape=jax.ShapeDtypeStruct((bsz, seq, D_MODEL), F32),
        compiler_params=pltpu.CompilerParams(dimension_semantics=("arbitrary", "arbitrary"),
                                             vmem_limit_bytes=VMEM_LIMIT),
        name="out_proj",
    )(y_attn, gz, yc, x, mod3, g_attn_out.reshape(1, D_ATTN), w_out, g_final.reshape(1, D_MODEL))


def kernel(x, c, positions, w_ada, b_ada, g_norm, w_in, w_conv, g_attn_out, g_conv_out, w_out, g_final):
    bsz, seq, _ = x.shape
    assert seq % (TILES_PER_STEP * TILE) == 0 and seq % OUT_TILE == 0 and seq // TILE <= HEAD_DIM // 2
    mod3 = _adaln(c, w_ada, b_ada).reshape(bsz, 3, D_MODEL)

    inv_freq = ROPE_THETA ** (-jnp.arange(HALF, dtype=F32) / HALF)
    inv_freq = jnp.tile(inv_freq, LANES // HALF).reshape(1, LANES)

    q_aug, k_aug, v_aug, gz, yc = _in_proj(x, mod3, g_norm, positions, inv_freq,
                                           w_in.astype(BF16), w_conv, g_conv_out)
    y_attn = _moba(q_aug, k_aug, v_aug)
    return _out_proj(y_attn, gz, yc, x, mod3, g_attn_out, w_out.astype(BF16), g_final)
```

```python
import functools

import jax
import jax.numpy as jnp
from jax import lax
from jax.experimental import pallas as pl
from jax.experimental.pallas import tpu as pltpu

D_MODEL = 1024
D_ATTN = 512
D_CONV = 512
N_HEADS = 8
HEAD_DIM = 64
HALF = HEAD_DIM // 2
CONV_WIDTH = 3
MOBA_BLOCK = 256
MOBA_TOPK = 3
ROPE_THETA = 10000.0
EPS = 1e-6
NEG = -1e30
LOG2E = 1.4426950408889634

LANES = 128
V_ROWS = HEAD_DIM + 16
TILE = MOBA_BLOCK
CHUNK = 2 * TILE
TILES_PER_STEP = 2
MXU_COLS = 256
PAIRS_PER_TRIP = 3
OUT_TILE = 1024
VMEM_LIMIT = 56 * 1024 * 1024

F32 = jnp.float32
BF16 = jnp.bfloat16


def _silu(z):
    return z * (1.0 / (1.0 + jnp.exp(-z)))


def _split_bf16(a):
    hi = a.astype(BF16)
    lo = (a - hi.astype(F32)).astype(BF16)
    return hi, lo


def _adaln_kernel(c_ref, w_ref, b_ref, o_ref):
    act = _silu(c_ref[...])
    rows = [jnp.sum(act[:, b:b + 1] * w_ref[...], axis=0, keepdims=True)
            for b in range(act.shape[1])]
    o_ref[...] = jnp.concatenate(rows, axis=0) + b_ref[...]


def _adaln(c, w_ada, b_ada):
    bsz = c.shape[0]
    n = w_ada.shape[1]
    bn = D_MODEL
    return pl.pallas_call(
        _adaln_kernel,
        grid=(n // bn,),
        in_specs=[pl.BlockSpec((D_MODEL, bsz), lambda j: (0, 0)),
                  pl.BlockSpec((D_MODEL, bn), lambda j: (0, j)),
                  pl.BlockSpec((1, bn), lambda j: (0, j))],
        out_specs=pl.BlockSpec((bsz, bn), lambda j: (0, j)),
        out_shape=jax.ShapeDtypeStruct((bsz, n), F32),
        compiler_params=pltpu.CompilerParams(dimension_semantics=("arbitrary",),
                                             vmem_limit_bytes=VMEM_LIMIT),
        name="adaln",
    )(c.T, w_ada, b_ada.reshape(1, n))


def _in_proj_kernel(n_blocks, x_ref, mod_ref, gnorm_ref, pos_ref, freq_ref, whi_ref,
                    wconv_ref, gconv_ref,
                    q_ref, k_ref, v_ref, gz_ref, yc_ref,
                    kbd_ref, carry_ref, cos_ref, sin_ref, proj_ref, h_ref):
    step = pl.program_id(1)

    @pl.when(step == 0)
    def _():
        kbd_ref[...] = jnp.zeros_like(kbd_ref)
        carry_ref[...] = jnp.zeros_like(carry_ref)

    n_cols = whi_ref.shape[1]
    col_groups = [(0, 2 * D_ATTN), (2 * D_ATTN, 3 * D_ATTN), (3 * D_ATTN, 4 * D_ATTN),
                  (4 * D_ATTN, n_cols)]

    def norm(r):
        x = x_ref[0, r * TILE:(r + 1) * TILE, :]
        xn = x * lax.rsqrt(jnp.mean(x * x, axis=-1, keepdims=True) + EPS) * gnorm_ref[...]
        h_ref[r] = (xn * (1.0 + mod_ref[0, 1:2, :]) + mod_ref[0, 0:1, :]).astype(BF16)

    def project(r, group):
        lo, hi = col_groups[group]
        for c0 in range(lo, hi, D_ATTN):
            proj_ref[r * TILE:(r + 1) * TILE, c0:c0 + D_ATTN] = jnp.dot(
                h_ref[r], whi_ref[:, c0:c0 + D_ATTN], preferred_element_type=F32)

    posts = [_block_posts(n_blocks, r, step * TILES_PER_STEP + r, pos_ref, freq_ref, wconv_ref,
                          gconv_ref, q_ref, k_ref, v_ref, gz_ref, yc_ref, kbd_ref, carry_ref,
                          cos_ref, sin_ref, proj_ref)
             for r in range(TILES_PER_STEP)]
    stages = [(r, group) for r in range(TILES_PER_STEP) for group in range(len(col_groups))]
    norm(0)
    project(*stages[0])
    for s, (r, group) in enumerate(stages):
        if s + 1 < len(stages):
            if stages[s + 1][1] == 0:
                norm(stages[s + 1][0])
            project(*stages[s + 1])
        posts[r][group]()


def _block_posts(n_blocks, r, i, pos_ref, freq_ref, wconv_ref, gconv_ref,
                 q_ref, k_ref, v_ref, gz_ref, yc_ref, kbd_ref, carry_ref, cos_ref, sin_ref,
                 proj_ref):
    rows = slice(r * TILE, (r + 1) * TILE)

    def proj(c0, c1):
        return proj_ref[rows, c0:c1]

    def post_qk():
        lane_c = lax.broadcasted_iota(jnp.int32, (TILE // 4, LANES), 1)
        ang = pos_ref[0, r * TILE // 4:(r + 1) * TILE // 4, :].astype(F32) * freq_ref[...]
        sign = jnp.where((lane_c % HEAD_DIM) < HALF, -1.0, 1.0)
        for table_ref, table, scale in ((cos_ref, jnp.cos(ang), None), (sin_ref, jnp.sin(ang), sign)):
            for a in range(LANES // HALF):
                z = jnp.where(lane_c // HALF == a, table, 0.0)
                z = z + pltpu.roll(z, 2 * HALF, 1)
                z = z + pltpu.roll(z, HALF, 1)
                table_ref[r, pl.ds(a, TILE // 4, stride=LANES // HALF), :] = (
                    z if scale is None else z * scale)
        cos = cos_ref[r]
        sin = sin_ref[r]

        lane = lax.broadcasted_iota(jnp.int32, (TILE, LANES), 1)
        first_half = (lane % HEAD_DIM) < HALF

        def rope(t):
            partner = jnp.where(first_half, pltpu.roll(t, LANES - HALF, 1), pltpu.roll(t, HALF, 1))
            return t * cos + partner * sin

        n_groups = D_ATTN // LANES
        q_groups = [rope(proj(g * LANES, (g + 1) * LANES)) for g in range(n_groups)]
        k_groups = [rope(proj(D_ATTN + g * LANES, D_ATTN + (g + 1) * LANES)) for g in range(n_groups)]
        q = jnp.concatenate(q_groups, axis=1)
        k = jnp.concatenate(k_groups, axis=1)

        nt = (((1,), (1,)), ((), ()))
        scores = lax.dot_general(kbd_ref[...].astype(BF16), q.astype(BF16), nt,
                                 preferred_element_type=F32)
        g3 = scores.reshape(N_HEADS, n_blocks, TILE)
        blk = lax.broadcasted_iota(jnp.int32, g3.shape, 1)
        past = blk < i
        g3 = jnp.where(past, g3, NEG)
        picked = jnp.zeros(g3.shape, jnp.bool_)
        for _ in range(min(MOBA_TOPK, n_blocks - 1)):
            top = jnp.max(g3, axis=1, keepdims=True)
            first = jnp.min(jnp.where(g3 == top, blk, n_blocks), axis=1, keepdims=True)
            hit = blk == first
            picked = jnp.logical_or(picked, hit)
            g3 = jnp.where(hit, -jnp.inf, g3)
        keep = jnp.logical_or(jnp.logical_and(picked, past), blk == i)
        bias = jnp.where(keep, 0.0, NEG)

        k_mean = jnp.mean(k, axis=0, keepdims=True)
        lane_w = lax.broadcasted_iota(jnp.int32, (1, D_ATTN), 1)
        for hd in range(N_HEADS):
            own = (lane_w // HEAD_DIM) == hd
            kbd_ref[pl.ds(hd * n_blocks + i, 1), :] = jnp.where(own, k_mean, 0.0)

        q_t = (q * (HEAD_DIM ** -0.5 * LOG2E)).T
        pad = jnp.zeros((HEAD_DIM - n_blocks, TILE), F32)
        parts = []
        for hd in range(N_HEADS):
            dims = q_t[hd * HEAD_DIM:(hd + 1) * HEAD_DIM]
            parts += [dims, bias[hd], pad] if hd % 2 == 0 else [bias[hd], pad, dims]
        q_ref[0, :, rows] = jnp.concatenate(parts, axis=0).astype(BF16)

        low = lane < HEAD_DIM
        onehot_even = jnp.where(lane == HEAD_DIM + i, 1.0, 0.0)
        onehot_odd = jnp.where(lane == i, 1.0, 0.0)
        for g in range(n_groups):
            even = slice(2 * g * LANES, (2 * g + 1) * LANES)
            odd = slice((2 * g + 1) * LANES, (2 * g + 2) * LANES)
            k_ref[0, rows, even] = jnp.where(low, k_groups[g], onehot_even).astype(BF16)
            k_ref[0, rows, odd] = jnp.where(low, onehot_odd, k_groups[g]).astype(BF16)

    def post_v():
        v_t = proj(2 * D_ATTN, 3 * D_ATTN).T
        ones = jnp.ones((V_ROWS - HEAD_DIM, TILE), F32)
        parts = []
        for hd in range(N_HEADS):
            parts += [v_t[hd * HEAD_DIM:(hd + 1) * HEAD_DIM], ones]
        v_ref[0, :, rows] = jnp.concatenate(parts, axis=0).astype(BF16)

    def post_gz():
        gz_ref[0, rows, :] = _silu(proj(3 * D_ATTN, 4 * D_ATTN)).astype(BF16)

    def post_conv():
        c0 = 4 * D_ATTN
        b_g = proj(c0, c0 + D_CONV)
        u = proj(c0 + D_CONV, c0 + 2 * D_CONV) * proj(c0 + 2 * D_CONV, c0 + 3 * D_CONV)
        z_c = proj(c0 + 3 * D_CONV, c0 + 4 * D_CONV)
        row = lax.broadcasted_iota(jnp.int32, u.shape, 0)
        prev1 = carry_ref[7:8, :]
        prev2 = carry_ref[6:7, :]
        u1 = jnp.where(row == 0, prev1, pltpu.roll(u, 1, 0))
        u2 = jnp.where(row == 0, prev2, jnp.where(row == 1, prev1, pltpu.roll(u, 2, 0)))
        carry_ref[...] = u[TILE - 8:TILE, :]
        y = b_g * (wconv_ref[2:3, :] * u + wconv_ref[1:2, :] * u1 + wconv_ref[0:1, :] * u2)
        y = y * lax.rsqrt(jnp.mean(y * y, axis=-1, keepdims=True) + EPS) * gconv_ref[...]
        yc_ref[0, rows, :] = (y * _silu(z_c)).astype(BF16)

    return [post_qk, post_v, post_gz, post_conv]


def _in_proj(x, mod3, g_norm, positions, inv_freq, w_hi, w_conv, g_conv_out):
    bsz, seq, _ = x.shape
    n_blocks = seq // TILE
    step_rows = TILES_PER_STEP * TILE
    wide = N_HEADS * LANES
    row_spec = lambda width: pl.BlockSpec((1, step_rows, width), lambda b, i: (b, i, 0))
    col_spec = lambda height: pl.BlockSpec((1, height, step_rows), lambda b, i: (b, 0, i))
    const = lambda shape: pl.BlockSpec(shape, lambda b, i: (0,) * len(shape))
    kbd_rows = N_HEADS * n_blocks
    per_row = LANES // HALF
    pos_compact = jnp.repeat(positions.reshape(bsz, seq // per_row, per_row), HALF, axis=2)
    return pl.pallas_call(
        functools.partial(_in_proj_kernel, n_blocks),
        grid=(bsz, n_blocks // TILES_PER_STEP),
        in_specs=[row_spec(D_MODEL),
                  pl.BlockSpec((1, 3, D_MODEL), lambda b, i: (b, 0, 0)),
                  const((1, D_MODEL)),
                  pl.BlockSpec((1, step_rows // per_row, LANES), lambda b, i: (b, i, 0)),
                  const((1, LANES)),
                  const(w_hi.shape),
                  const((CONV_WIDTH, D_CONV)),
                  const((1, D_CONV))],
        out_specs=[col_spec(wide), row_spec(wide), col_spec(N_HEADS * V_ROWS),
                   row_spec(D_ATTN), row_spec(D_CONV)],
        out_shape=[jax.ShapeDtypeStruct((bsz, wide, seq), BF16),
                   jax.ShapeDtypeStruct((bsz, seq, wide), BF16),
                   jax.ShapeDtypeStruct((bsz, N_HEADS * V_ROWS, seq), BF16),
                   jax.ShapeDtypeStruct((bsz, seq, D_ATTN), BF16),
                   jax.ShapeDtypeStruct((bsz, seq, D_CONV), BF16)],
        scratch_shapes=[pltpu.VMEM((kbd_rows, D_ATTN), F32),
                        pltpu.VMEM((8, D_CONV), F32),
                        pltpu.VMEM((TILES_PER_STEP, TILE, LANES), F32),
                        pltpu.VMEM((TILES_PER_STEP, TILE, LANES), F32),
                        pltpu.VMEM((step_rows, w_hi.shape[1]), F32),
                        pltpu.VMEM((TILES_PER_STEP, TILE, D_MODEL), BF16)],
        compiler_params=pltpu.CompilerParams(dimension_semantics=("arbitrary", "arbitrary"),
                                             vmem_limit_bytes=VMEM_LIMIT),
        name="in_proj",
    )(x, mod3, g_norm.reshape(1, D_MODEL), pos_compact, inv_freq,
      w_hi, w_conv, g_conv_out.reshape(1, D_CONV))


def _moba_kernel(n_chunks, tile_ref, chunk_ref, q_ref, k_ref, v_ref, o_ref, s_ref, smax_ref,
                 m_ref, acc_ref):
    n_past = n_chunks * (n_chunks - 1) // 2

    def own_item(t):
        return t, t

    def past_item(j):
        return tile_ref[j], chunk_ref[j]

    def issue_unit(buf, tile, chunk, hd, half):
        start = pl.multiple_of(chunk * CHUNK, CHUNK)
        q_start = pl.multiple_of(tile * CHUNK + half * MXU_COLS, MXU_COLS)
        rows = slice(hd * LANES, (hd + 1) * LANES)
        cols = slice(half * MXU_COLS, (half + 1) * MXU_COLS)
        kj = k_ref[0, pl.ds(start, CHUNK), rows]
        s = jnp.dot(kj, q_ref[0, rows, pl.ds(q_start, MXU_COLS)],
                    preferred_element_type=F32)
        s_ref[buf, hd, :, cols] = s.astype(BF16)
        smax_ref[buf, hd, :, cols] = jnp.max(s, axis=0, keepdims=True)

    def absorb_unit(buf, tile, chunk, hd, half, causal):
        start = pl.multiple_of(chunk * CHUNK, CHUNK)
        cols = slice(half * MXU_COLS, (half + 1) * MXU_COLS)
        s = s_ref[buf, hd, :, cols]
        smax = smax_ref[buf, hd, :, cols]
        if causal:
            key = lax.broadcasted_iota(jnp.int32, s.shape, 0)
            qry = lax.broadcasted_iota(jnp.int32, s.shape, 1) + cols.start
            s = jnp.where(key <= qry, s.astype(F32), NEG)
            smax = jnp.max(s, axis=0, keepdims=True)
        m = m_ref[tile, hd, :, cols]
        m_new = jnp.maximum(m, smax).astype(BF16)
        p = jnp.exp2(s.astype(BF16) - m_new)
        m_new = m_new.astype(F32)
        vj = v_ref[0, hd * V_ROWS:(hd + 1) * V_ROWS, pl.ds(start, CHUNK)]
        acc_ref[tile, hd, :, cols] = (jnp.exp2(m - m_new) * acc_ref[tile, hd, :, cols]
                                      + jnp.dot(vj, p, preferred_element_type=F32))
        m_ref[tile, hd, :, cols] = m_new

    def step(issue=None, absorb=None, causal=False):
        for hd in range(2):
            for half in range(CHUNK // MXU_COLS):
                if issue is not None:
                    issue_unit(*issue, hd, half)
                if absorb is not None:
                    absorb_unit(*absorb, hd, half, causal)

    def run_pairs(item, n_pairs, causal):
        def pair(u):
            step(issue=(1, *item(2 * u + 1)), absorb=(0, *item(2 * u)), causal=causal)
            step(issue=(0, *item(2 * u + 2)), absorb=(1, *item(2 * u + 1)), causal=causal)

        def trip(t, carry):
            for u in range(PAIRS_PER_TRIP):
                pair(PAIRS_PER_TRIP * t + u)
            return carry

        trips = n_pairs // PAIRS_PER_TRIP
        lax.fori_loop(0, trips, trip, 0)
        for u in range(trips * PAIRS_PER_TRIP, n_pairs):
            pair(u)

    m_ref[...] = jnp.full(m_ref.shape, -jnp.inf, F32)
    acc_ref[...] = jnp.zeros(acc_ref.shape, F32)
    assert n_chunks % 2 == 0 and n_past % 2 == 0
    step(issue=(0, *own_item(0)))
    run_pairs(own_item, (n_chunks - 2) // 2, causal=True)
    step(issue=(1, *own_item(n_chunks - 1)), absorb=(0, *own_item(n_chunks - 2)), causal=True)
    step(issue=(0, *past_item(0)), absorb=(1, *own_item(n_chunks - 1)), causal=True)
    run_pairs(past_item, (n_past - 2) // 2, causal=False)
    step(issue=(1, *past_item(n_past - 1)), absorb=(0, *past_item(n_past - 2)))
    step(absorb=(1, *past_item(n_past - 1)))

    def finish(tile, carry):
        outs = [acc_ref[tile, hd, :HEAD_DIM] / acc_ref[tile, hd, HEAD_DIM:HEAD_DIM + 1]
                for hd in range(2)]
        start = pl.multiple_of(tile * CHUNK, CHUNK)
        o_ref[0, pl.ds(start, CHUNK), :] = jnp.concatenate(outs, axis=0).T.astype(o_ref.dtype)
        return carry

    lax.fori_loop(0, n_chunks, finish, 0)


def _moba(q_aug, k_aug, v_aug):
    bsz, seq, wide = k_aug.shape
    pairs = wide // (2 * LANES)
    n_chunks = seq // CHUNK
    past = [(t, c) for t in range(n_chunks) for c in range(t)]
    item_tile = jnp.asarray([t for t, _ in past], jnp.int32)
    item_chunk = jnp.asarray([c for _, c in past], jnp.int32)
    return pl.pallas_call(
        functools.partial(_moba_kernel, n_chunks),
        grid_spec=pltpu.PrefetchScalarGridSpec(
            num_scalar_prefetch=2,
            grid=(bsz, pairs),
            in_specs=[pl.BlockSpec((1, 2 * LANES, seq), lambda b, p, *_: (b, p, 0)),
                      pl.BlockSpec((1, seq, 2 * LANES), lambda b, p, *_: (b, 0, p)),
                      pl.BlockSpec((1, 2 * V_ROWS, seq), lambda b, p, *_: (b, p, 0))],
            out_specs=pl.BlockSpec((1, seq, LANES), lambda b, p, *_: (b, 0, p)),
            scratch_shapes=[pltpu.VMEM((2, 2, CHUNK, CHUNK), BF16),
                            pltpu.VMEM((2, 2, 1, CHUNK), F32),
                            pltpu.VMEM((n_chunks, 2, 1, CHUNK), F32),
                            pltpu.VMEM((n_chunks, 2, V_ROWS, CHUNK), F32)]),
        out_shape=jax.ShapeDtypeStruct((bsz, seq, pairs * LANES), BF16),
        compiler_params=pltpu.CompilerParams(
            dimension_semantics=("arbitrary", "arbitrary"),
            vmem_limit_bytes=VMEM_LIMIT),
        name="moba",
    )(item_tile, item_chunk, q_aug, k_aug, v_aug)


def _out_proj_kernel(ya_ref, gz_ref, yc_ref, x_ref, mod_ref, gattn_ref, wout_ref, gfin_ref, o_ref):
    ya = ya_ref[0].astype(F32)
    yn = ya * lax.rsqrt(jnp.mean(ya * ya, axis=-1, keepdims=True) + EPS) * gattn_ref[...]
    yn = (yn * gz_ref[0].astype(F32)).astype(BF16)
    y = (jnp.dot(yn, wout_ref[0:D_ATTN, :], preferred_element_type=F32)
         + jnp.dot(yc_ref[0], wout_ref[D_ATTN:, :], preferred_element_type=F32))
    xo = x_ref[0] + mod_ref[0, 2:3, :] * y
    o_ref[0] = xo * lax.rsqrt(jnp.mean(xo * xo, axis=-1, keepdims=True) + EPS) * gfin_ref[...]


def _out_proj(y_attn, gz, yc, x, mod3, g_attn_out, w_out, g_final):
    bsz, seq, _ = x.shape
    row_spec = lambda width: pl.BlockSpec((1, OUT_TILE, width), lambda b, i: (b, i, 0))
    const = lambda shape: pl.BlockSpec(shape, lambda b, i: (0,) * len(shape))
    return pl.pallas_call(
        _out_proj_kernel,
        grid=(bsz, seq // OUT_TILE),
        in_specs=[row_spec(D_ATTN), row_spec(D_ATTN), row_spec(D_CONV), row_spec(D_MODEL),
                  pl.BlockSpec((1, 3, D_MODEL), lambda b, i: (b, 0, 0)),
                  const((1, D_ATTN)), const(w_out.shape), const((1, D_MODEL))],
        out_specs=row_spec(D_MODEL),
        out_shape=jax.ShapeDtypeStruct((bsz, seq, D_MODEL), F32),
        compiler_params=pltpu.CompilerParams(dimension_semantics=("arbitrary", "arbitrary"),
                                             vmem_limit_bytes=VMEM_LIMIT),
        name="out_proj",
    )(y_attn, gz, yc, x, mod3, g_attn_out.reshape(1, D_ATTN), w_out, g_final.reshape(1, D_MODEL))


def kernel(x, c, positions, w_ada, b_ada, g_norm, w_in, w_conv, g_attn_out, g_conv_out, w_out, g_final):
    bsz, seq, _ = x.shape
    assert seq % (TILES_PER_STEP * TILE) == 0 and seq % OUT_TILE == 0 and seq // TILE <= HEAD_DIM // 2
    mod3 = _adaln(c, w_ada, b_ada).reshape(bsz, 3, D_MODEL)

    inv_freq = ROPE_THETA ** (-jnp.arange(HALF, dtype=F32) / HALF)
    inv_freq = jnp.tile(inv_freq, LANES // HALF).reshape(1, LANES)

    q_aug, k_aug, v_aug, gz, yc = _in_proj(x, mod3, g_norm, positions, inv_freq,
                                           w_in.astype(BF16), w_conv, g_conv_out)
    y_attn = _moba(q_aug, k_aug, v_aug)
    return _out_proj(y_attn, gz, yc, x, mod3, g_attn_out, w_out.astype(BF16), g_final)
```

```python
import functools

import jax
import jax.numpy as jnp
from jax import lax
from jax.experimental import pallas as pl
from jax.experimental.pallas import tpu as pltpu

D_MODEL = 1024
D_ATTN = 512
D_CONV = 512
N_HEADS = 8
HEAD_DIM = 64
HALF = HEAD_DIM // 2
CONV_WIDTH = 3
MOBA_BLOCK = 256
MOBA_TOPK = 3
ROPE_THETA = 10000.0
EPS = 1e-6
NEG = -1e30
LOG2E = 1.4426950408889634

LANES = 128
V_ROWS = HEAD_DIM + 16
TILE = MOBA_BLOCK
CHUNK = 2 * TILE
TILES_PER_STEP = 2
MXU_COLS = 256
PAIRS_PER_TRIP = 5
OUT_TILE = 1024
VMEM_LIMIT = 56 * 1024 * 1024

F32 = jnp.float32
BF16 = jnp.bfloat16


def _silu(z):
    return z * (1.0 / (1.0 + jnp.exp(-z)))


def _split_bf16(a):
    hi = a.astype(BF16)
    lo = (a - hi.astype(F32)).astype(BF16)
    return hi, lo


def _adaln_kernel(c_ref, w_ref, b_ref, o_ref):
    act = _silu(c_ref[...])
    rows = [jnp.sum(act[:, b:b + 1] * w_ref[...], axis=0, keepdims=True)
            for b in range(act.shape[1])]
    o_ref[...] = jnp.concatenate(rows, axis=0) + b_ref[...]


def _adaln(c, w_ada, b_ada):
    bsz = c.shape[0]
    n = w_ada.shape[1]
    bn = D_MODEL
    return pl.pallas_call(
        _adaln_kernel,
        grid=(n // bn,),
        in_specs=[pl.BlockSpec((D_MODEL, bsz), lambda j: (0, 0)),
                  pl.BlockSpec((D_MODEL, bn), lambda j: (0, j)),
                  pl.BlockSpec((1, bn), lambda j: (0, j))],
        out_specs=pl.BlockSpec((bsz, bn), lambda j: (0, j)),
        out_shape=jax.ShapeDtypeStruct((bsz, n), F32),
        compiler_params=pltpu.CompilerParams(dimension_semantics=("arbitrary",),
                                             vmem_limit_bytes=VMEM_LIMIT),
        name="adaln",
    )(c.T, w_ada, b_ada.reshape(1, n))


def _in_proj_kernel(n_blocks, x_ref, mod_ref, gnorm_ref, pos_ref, freq_ref, whi_ref,
                    wconv_ref, gconv_ref,
                    q_ref, k_ref, v_ref, gz_ref, yc_ref,
                    kbd_ref, carry_ref, cos_ref, sin_ref, proj_ref, h_ref):
    step = pl.program_id(1)

    @pl.when(step == 0)
    def _():
        kbd_ref[...] = jnp.zeros_like(kbd_ref)
        carry_ref[...] = jnp.zeros_like(carry_ref)

    n_cols = whi_ref.shape[1]
    col_groups = [(0, 2 * D_ATTN), (2 * D_ATTN, 3 * D_ATTN), (3 * D_ATTN, 4 * D_ATTN),
                  (4 * D_ATTN, n_cols)]

    def norm(r):
        x = x_ref[0, r * TILE:(r + 1) * TILE, :]
        xn = x * lax.rsqrt(jnp.mean(x * x, axis=-1, keepdims=True) + EPS) * gnorm_ref[...]
        h_ref[r] = (xn * (1.0 + mod_ref[0, 1:2, :]) + mod_ref[0, 0:1, :]).astype(BF16)

    def project(r, group):
        lo, hi = col_groups[group]
        for c0 in range(lo, hi, D_ATTN):
            proj_ref[r * TILE:(r + 1) * TILE, c0:c0 + D_ATTN] = jnp.dot(
                h_ref[r], whi_ref[:, c0:c0 + D_ATTN], preferred_element_type=F32)

    posts = [_block_posts(n_blocks, r, step * TILES_PER_STEP + r, pos_ref, freq_ref, wconv_ref,
                          gconv_ref, q_ref, k_ref, v_ref, gz_ref, yc_ref, kbd_ref, carry_ref,
                          cos_ref, sin_ref, proj_ref)
             for r in range(TILES_PER_STEP)]
    stages = [(r, group) for r in range(TILES_PER_STEP) for group in range(len(col_groups))]
    norm(0)
    project(*stages[0])
    for s, (r, group) in enumerate(stages):
        if s + 1 < len(stages):
            if stages[s + 1][1] == 0:
                norm(stages[s + 1][0])
            project(*stages[s + 1])
        posts[r][group]()


def _block_posts(n_blocks, r, i, pos_ref, freq_ref, wconv_ref, gconv_ref,
                 q_ref, k_ref, v_ref, gz_ref, yc_ref, kbd_ref, carry_ref, cos_ref, sin_ref,
                 proj_ref):
    rows = slice(r * TILE, (r + 1) * TILE)

    def proj(c0, c1):
        return proj_ref[rows, c0:c1]

    def post_qk():
        lane_c = lax.broadcasted_iota(jnp.int32, (TILE // 4, LANES), 1)
        ang = pos_ref[0, r * TILE // 4:(r + 1) * TILE // 4, :].astype(F32) * freq_ref[...]
        sign = jnp.where((lane_c % HEAD_DIM) < HALF, -1.0, 1.0)
        for table_ref, table, scale in ((cos_ref, jnp.cos(ang), None), (sin_ref, jnp.sin(ang), sign)):
            for a in range(LANES // HALF):
                z = jnp.where(lane_c // HALF == a, table, 0.0)
                z = z + pltpu.roll(z, 2 * HALF, 1)
                z = z + pltpu.roll(z, HALF, 1)
                table_ref[r, pl.ds(a, TILE // 4, stride=LANES // HALF), :] = (
                    z if scale is None else z * scale)
        cos = cos_ref[r]
        sin = sin_ref[r]

        lane = lax.broadcasted_iota(jnp.int32, (TILE, LANES), 1)
        first_half = (lane % HEAD_DIM) < HALF

        def rope(t):
            partner = jnp.where(first_half, pltpu.roll(t, LANES - HALF, 1), pltpu.roll(t, HALF, 1))
            return t * cos + partner * sin

        n_groups = D_ATTN // LANES
        q_groups = [rope(proj(g * LANES, (g + 1) * LANES)) for g in range(n_groups)]
        k_groups = [rope(proj(D_ATTN + g * LANES, D_ATTN + (g + 1) * LANES)) for g in range(n_groups)]
        q = jnp.concatenate(q_groups, axis=1)
        k = jnp.concatenate(k_groups, axis=1)

        nt = (((1,), (1,)), ((), ()))
        scores = lax.dot_general(kbd_ref[...].astype(BF16), q.astype(BF16), nt,
                                 preferred_element_type=F32)
        g3 = scores.reshape(N_HEADS, n_blocks, TILE)
        blk = lax.broadcasted_iota(jnp.int32, g3.shape, 1)
        past = blk < i
        g3 = jnp.where(past, g3, NEG)
        picked = jnp.zeros(g3.shape, jnp.bool_)
        for _ in range(min(MOBA_TOPK, n_blocks - 1)):
            top = jnp.max(g3, axis=1, keepdims=True)
            first = jnp.min(jnp.where(g3 == top, blk, n_blocks), axis=1, keepdims=True)
            hit = blk == first
            picked = jnp.logical_or(picked, hit)
            g3 = jnp.where(hit, -jnp.inf, g3)
        keep = jnp.logical_or(jnp.logical_and(picked, past), blk == i)
        bias = jnp.where(keep, 0.0, NEG)

        k_mean = jnp.mean(k, axis=0, keepdims=True)
        lane_w = lax.broadcasted_iota(jnp.int32, (1, D_ATTN), 1)
        for hd in range(N_HEADS):
            own = (lane_w // HEAD_DIM) == hd
            kbd_ref[pl.ds(hd * n_blocks + i, 1), :] = jnp.where(own, k_mean, 0.0)

        q_t = (q * (HEAD_DIM ** -0.5 * LOG2E)).T
        pad = jnp.zeros((HEAD_DIM - n_blocks, TILE), F32)
        parts = []
        for hd in range(N_HEADS):
            dims = q_t[hd * HEAD_DIM:(hd + 1) * HEAD_DIM]
            parts += [dims, bias[hd], pad] if hd % 2 == 0 else [bias[hd], pad, dims]
        q_ref[0, :, rows] = jnp.concatenate(parts, axis=0).astype(BF16)

        low = lane < HEAD_DIM
        onehot_even = jnp.where(lane == HEAD_DIM + i, 1.0, 0.0)
        onehot_odd = jnp.where(lane == i, 1.0, 0.0)
        for g in range(n_groups):
            even = slice(2 * g * LANES, (2 * g + 1) * LANES)
            odd = slice((2 * g + 1) * LANES, (2 * g + 2) * LANES)
            k_ref[0, rows, even] = jnp.where(low, k_groups[g], onehot_even).astype(BF16)
            k_ref[0, rows, odd] = jnp.where(low, onehot_odd, k_groups[g]).astype(BF16)

    def post_v():
        v_t = proj(2 * D_ATTN, 3 * D_ATTN).T
        ones = jnp.ones((V_ROWS - HEAD_DIM, TILE), F32)
        parts = []
        for hd in range(N_HEADS):
            parts += [v_t[hd * HEAD_DIM:(hd + 1) * HEAD_DIM], ones]
        v_ref[0, :, rows] = jnp.concatenate(parts, axis=0).astype(BF16)

    def post_gz():
        gz_ref[0, rows, :] = _silu(proj(3 * D_ATTN, 4 * D_ATTN)).astype(BF16)

    def post_conv():
        c0 = 4 * D_ATTN
        b_g = proj(c0, c0 + D_CONV)
        u = proj(c0 + D_CONV, c0 + 2 * D_CONV) * proj(c0 + 2 * D_CONV, c0 + 3 * D_CONV)
        z_c = proj(c0 + 3 * D_CONV, c0 + 4 * D_CONV)
        row = lax.broadcasted_iota(jnp.int32, u.shape, 0)
        prev1 = carry_ref[7:8, :]
        prev2 = carry_ref[6:7, :]
        u1 = jnp.where(row == 0, prev1, pltpu.roll(u, 1, 0))
        u2 = jnp.where(row == 0, prev2, jnp.where(row == 1, prev1, pltpu.roll(u, 2, 0)))
        carry_ref[...] = u[TILE - 8:TILE, :]
        y = b_g * (wconv_ref[2:3, :] * u + wconv_ref[1:2, :] * u1 + wconv_ref[0:1, :] * u2)
        y = y * lax.rsqrt(jnp.mean(y * y, axis=-1, keepdims=True) + EPS) * gconv_ref[...]
        yc_ref[0, rows, :] = (y * _silu(z_c)).astype(BF16)

    return [post_qk, post_v, post_gz, post_conv]


def _in_proj(x, mod3, g_norm, positions, inv_freq, w_hi, w_conv, g_conv_out):
    bsz, seq, _ = x.shape
    n_blocks = seq // TILE
    step_rows = TILES_PER_STEP * TILE
    wide = N_HEADS * LANES
    row_spec = lambda width: pl.BlockSpec((1, step_rows, width), lambda b, i: (b, i, 0))
    col_spec = lambda height: pl.BlockSpec((1, height, step_rows), lambda b, i: (b, 0, i))
    const = lambda shape: pl.BlockSpec(shape, lambda b, i: (0,) * len(shape))
    kbd_rows = N_HEADS * n_blocks
    per_row = LANES // HALF
    pos_compact = jnp.repeat(positions.reshape(bsz, seq // per_row, per_row), HALF, axis=2)
    return pl.pallas_call(
        functools.partial(_in_proj_kernel, n_blocks),
        grid=(bsz, n_blocks // TILES_PER_STEP),
        in_specs=[row_spec(D_MODEL),
                  pl.BlockSpec((1, 3, D_MODEL), lambda b, i: (b, 0, 0)),
                  const((1, D_MODEL)),
                  pl.BlockSpec((1, step_rows // per_row, LANES), lambda b, i: (b, i, 0)),
                  const((1, LANES)),
                  const(w_hi.shape),
                  const((CONV_WIDTH, D_CONV)),
                  const((1, D_CONV))],
        out_specs=[col_spec(wide), row_spec(wide), col_spec(N_HEADS * V_ROWS),
                   row_spec(D_ATTN), row_spec(D_CONV)],
        out_shape=[jax.ShapeDtypeStruct((bsz, wide, seq), BF16),
                   jax.ShapeDtypeStruct((bsz, seq, wide), BF16),
                   jax.ShapeDtypeStruct((bsz, N_HEADS * V_ROWS, seq), BF16),
                   jax.ShapeDtypeStruct((bsz, seq, D_ATTN), BF16),
                   jax.ShapeDtypeStruct((bsz, seq, D_CONV), BF16)],
        scratch_shapes=[pltpu.VMEM((kbd_rows, D_ATTN), F32),
                        pltpu.VMEM((8, D_CONV), F32),
                        pltpu.VMEM((TILES_PER_STEP, TILE, LANES), F32),
                        pltpu.VMEM((TILES_PER_STEP, TILE, LANES), F32),
                        pltpu.VMEM((step_rows, w_hi.shape[1]), F32),
                        pltpu.VMEM((TILES_PER_STEP, TILE, D_MODEL), BF16)],
        compiler_params=pltpu.CompilerParams(dimension_semantics=("arbitrary", "arbitrary"),
                                             vmem_limit_bytes=VMEM_LIMIT),
        name="in_proj",
    )(x, mod3, g_norm.reshape(1, D_MODEL), pos_compact, inv_freq,
      w_hi, w_conv, g_conv_out.reshape(1, D_CONV))


def _moba_kernel(n_chunks, tile_ref, chunk_ref, q_ref, k_ref, v_ref, o_ref, s_ref, smax_ref,
                 m_ref, acc_ref):
    n_past = n_chunks * (n_chunks - 1) // 2

    def own_item(t):
        return t, t

    def past_item(j):
        return tile_ref[j], chunk_ref[j]

    def issue_unit(buf, tile, chunk, hd, half):
        start = pl.multiple_of(chunk * CHUNK, CHUNK)
        q_start = pl.multiple_of(tile * CHUNK + half * MXU_COLS, MXU_COLS)
        rows = slice(hd * LANES, (hd + 1) * LANES)
        cols = slice(half * MXU_COLS, (half + 1) * MXU_COLS)
        kj = k_ref[0, pl.ds(start, CHUNK), rows]
        s = jnp.dot(kj, q_ref[0, rows, pl.ds(q_start, MXU_COLS)],
                    preferred_element_type=F32)
        s_ref[buf, hd, :, cols] = s.astype(BF16)
        smax_ref[buf, hd, :, cols] = jnp.max(s, axis=0, keepdims=True)

    def absorb_unit(buf, tile, chunk, hd, half, causal):
        start = pl.multiple_of(chunk * CHUNK, CHUNK)
        cols = slice(half * MXU_COLS, (half + 1) * MXU_COLS)
        s = s_ref[buf, hd, :, cols]
        smax = smax_ref[buf, hd, :, cols]
        if causal:
            key = lax.broadcasted_iota(jnp.int32, s.shape, 0)
            qry = lax.broadcasted_iota(jnp.int32, s.shape, 1) + cols.start
            s = jnp.where(key <= qry, s.astype(F32), NEG)
            smax = jnp.max(s, axis=0, keepdims=True)
        m = m_ref[tile, hd, :, cols]
        m_new = jnp.maximum(m, smax).astype(BF16)
        p = jnp.exp2(s.astype(BF16) - m_new)
        m_new = m_new.astype(F32)
        vj = v_ref[0, hd * V_ROWS:(hd + 1) * V_ROWS, pl.ds(start, CHUNK)]
        acc_ref[tile, hd, :, cols] = (jnp.exp2(m - m_new) * acc_ref[tile, hd, :, cols]
                                      + jnp.dot(vj, p, preferred_element_type=F32))
        m_ref[tile, hd, :, cols] = m_new

    def step(issue=None, absorb=None, causal=False):
        for hd in range(2):
            for half in range(CHUNK // MXU_COLS):
                if issue is not None:
                    issue_unit(*issue, hd, half)
                if absorb is not None:
                    absorb_unit(*absorb, hd, half, causal)

    def run_pairs(item, n_pairs, causal):
        def pair(u):
            step(issue=(1, *item(2 * u + 1)), absorb=(0, *item(2 * u)), causal=causal)
            step(issue=(0, *item(2 * u + 2)), absorb=(1, *item(2 * u + 1)), causal=causal)

        def trip(t, carry):
            for u in range(PAIRS_PER_TRIP):
                pair(PAIRS_PER_TRIP * t + u)
            return carry

        trips = n_pairs // PAIRS_PER_TRIP
        lax.fori_loop(0, trips, trip, 0)
        for u in range(trips * PAIRS_PER_TRIP, n_pairs):
            pair(u)

    m_ref[...] = jnp.full(m_ref.shape, -jnp.inf, F32)
    acc_ref[...] = jnp.zeros(acc_ref.shape, F32)
    assert n_chunks % 2 == 0 and n_past % 2 == 0
    step(issue=(0, *own_item(0)))
    run_pairs(own_item, (n_chunks - 2) // 2, causal=True)
    step(issue=(1, *own_item(n_chunks - 1)), absorb=(0, *own_item(n_chunks - 2)), causal=True)
    step(issue=(0, *past_item(0)), absorb=(1, *own_item(n_chunks - 1)), causal=True)
    run_pairs(past_item, (n_past - 2) // 2, causal=False)
    step(issue=(1, *past_item(n_past - 1)), absorb=(0, *past_item(n_past - 2)))
    step(absorb=(1, *past_item(n_past - 1)))

    def finish(tile, carry):
        outs = [acc_ref[tile, hd, :HEAD_DIM] / acc_ref[tile, hd, HEAD_DIM:HEAD_DIM + 1]
                for hd in range(2)]
        start = pl.multiple_of(tile * CHUNK, CHUNK)
        o_ref[0, pl.ds(start, CHUNK), :] = jnp.concatenate(outs, axis=0).T.astype(o_ref.dtype)
        return carry

    lax.fori_loop(0, n_chunks, finish, 0)


def _moba(q_aug, k_aug, v_aug):
    bsz, seq, wide = k_aug.shape
    pairs = wide // (2 * LANES)
    n_chunks = seq // CHUNK
    past = [(t, c) for t in range(n_chunks) for c in range(t)]
    item_tile = jnp.asarray([t for t, _ in past], jnp.int32)
    item_chunk = jnp.asarray([c for _, c in past], jnp.int32)
    return pl.pallas_call(
        functools.partial(_moba_kernel, n_chunks),
        grid_spec=pltpu.PrefetchScalarGridSpec(
            num_scalar_prefetch=2,
            grid=(bsz, pairs),
            in_specs=[pl.BlockSpec((1, 2 * LANES, seq), lambda b, p, *_: (b, p, 0)),
                      pl.BlockSpec((1, seq, 2 * LANES), lambda b, p, *_: (b, 0, p)),
                      pl.BlockSpec((1, 2 * V_ROWS, seq), lambda b, p, *_: (b, p, 0))],
            out_specs=pl.BlockSpec((1, seq, LANES), lambda b, p, *_: (b, 0, p)),
            scratch_shapes=[pltpu.VMEM((2, 2, CHUNK, CHUNK), BF16),
                            pltpu.VMEM((2, 2, 1, CHUNK), F32),
                            pltpu.VMEM((n_chunks, 2, 1, CHUNK), F32),
                            pltpu.VMEM((n_chunks, 2, V_ROWS, CHUNK), F32)]),
        out_shape=jax.ShapeDtypeStruct((bsz, seq, pairs * LANES), BF16),
        compiler_params=pltpu.CompilerParams(
            dimension_semantics=("arbitrary", "arbitrary"),
            vmem_limit_bytes=VMEM_LIMIT),
        name="moba",
    )(item_tile, item_chunk, q_aug, k_aug, v_aug)


def _out_proj_kernel(ya_ref, gz_ref, yc_ref, x_ref, mod_ref, gattn_ref, wout_ref, gfin_ref, o_ref):
    ya = ya_ref[0].astype(F32)
    yn = ya * lax.rsqrt(jnp.mean(ya * ya, axis=-1, keepdims=True) + EPS) * gattn_ref[...]
    yn = (yn * gz_ref[0].astype(F32)).astype(BF16)
    y = (jnp.dot(yn, wout_ref[0:D_ATTN, :], preferred_element_type=F32)
         + jnp.dot(yc_ref[0], wout_ref[D_ATTN:, :], preferred_element_type=F32))
    xo = x_ref[0] + mod_ref[0, 2:3, :] * y
    o_ref[0] = xo * lax.rsqrt(jnp.mean(xo * xo, axis=-1, keepdims=True) + EPS) * gfin_ref[...]


def _out_proj(y_attn, gz, yc, x, mod3, g_attn_out, w_out, g_final):
    bsz, seq, _ = x.shape
    row_spec = lambda width: pl.BlockSpec((1, OUT_TILE, width), lambda b, i: (b, i, 0))
    const = lambda shape: pl.BlockSpec(shape, lambda b, i: (0,) * len(shape))
    return pl.pallas_call(
        _out_proj_kernel,
        grid=(bsz, seq // OUT_TILE),
        in_specs=[row_spec(D_ATTN), row_spec(D_ATTN), row_spec(D_CONV), row_spec(D_MODEL),
                  pl.BlockSpec((1, 3, D_MODEL), lambda b, i: (b, 0, 0)),
                  const((1, D_ATTN)), const(w_out.shape), const((1, D_MODEL))],
        out_specs=row_spec(D_MODEL),
        out_shape=jax.ShapeDtypeStruct((bsz, seq, D_MODEL), F32),
        compiler_params=pltpu.CompilerParams(dimension_semantics=("arbitrary", "arbitrary"),
                                             vmem_limit_bytes=VMEM_LIMIT),
        name="out_proj",
    )(y_attn, gz, yc, x, mod3, g_attn_out.reshape(1, D_ATTN), w_out, g_final.reshape(1, D_MODEL))


def kernel(x, c, positions, w_ada, b_ada, g_norm, w_in, w_conv, g_attn_out, g_conv_out, w_out, g_final):
    bsz, seq, _ = x.shape
    assert seq % (TILES_PER_STEP * TILE) == 0 and seq % OUT_TILE == 0 and seq // TILE <= HEAD_DIM // 2
    mod3 = _adaln(c, w_ada, b_ada).reshape(bsz, 3, D_MODEL)

    inv_freq = ROPE_THETA ** (-jnp.arange(HALF, dtype=F32) / HALF)
    inv_freq = jnp.tile(inv_freq, LANES // HALF).reshape(1, LANES)

    q_aug, k_aug, v_aug, gz, yc = _in_proj(x, mod3, g_norm, positions, inv_freq,
                                           w_in.astype(BF16), w_conv, g_conv_out)
    y_attn = _moba(q_aug, k_aug, v_aug)
    return _out_proj(y_attn, gz, yc, x, mod3, g_attn_out, w_out.astype(BF16), g_final)
```

```python
import functools

import jax
import jax.numpy as jnp
from jax import lax
from jax.experimental import pallas as pl
from jax.experimental.pallas import tpu as pltpu

D_MODEL = 1024
D_ATTN = 512
D_CONV = 512
N_HEADS = 8
HEAD_DIM = 64
HALF = HEAD_DIM // 2
CONV_WIDTH = 3
MOBA_BLOCK = 256
MOBA_TOPK = 3
ROPE_THETA = 10000.0
EPS = 1e-6
NEG = -1e30
LOG2E = 1.4426950408889634

LANES = 128
V_ROWS = HEAD_DIM + 16
TILE = MOBA_BLOCK
CHUNK = 2 * TILE
TILES_PER_STEP = 2
MXU_COLS = 256
OWN_PAIRS_PER_TRIP = 7
PAST_PAIRS_PER_TRIP = 5
OUT_TILE = 1024
VMEM_LIMIT = 56 * 1024 * 1024

F32 = jnp.float32
BF16 = jnp.bfloat16


def _silu(z):
    return z * (1.0 / (1.0 + jnp.exp(-z)))


def _split_bf16(a):
    hi = a.astype(BF16)
    lo = (a - hi.astype(F32)).astype(BF16)
    return hi, lo


def _adaln_kernel(c_ref, w_ref, b_ref, o_ref):
    act = _silu(c_ref[...])
    rows = [jnp.sum(act[:, b:b + 1] * w_ref[...], axis=0, keepdims=True)
            for b in range(act.shape[1])]
    o_ref[...] = jnp.concatenate(rows, axis=0) + b_ref[...]


def _adaln(c, w_ada, b_ada):
    bsz = c.shape[0]
    n = w_ada.shape[1]
    bn = D_MODEL
    return pl.pallas_call(
        _adaln_kernel,
        grid=(n // bn,),
        in_specs=[pl.BlockSpec((D_MODEL, bsz), lambda j: (0, 0)),
                  pl.BlockSpec((D_MODEL, bn), lambda j: (0, j)),
                  pl.BlockSpec((1, bn), lambda j: (0, j))],
        out_specs=pl.BlockSpec((bsz, bn), lambda j: (0, j)),
        out_shape=jax.ShapeDtypeStruct((bsz, n), F32),
        compiler_params=pltpu.CompilerParams(dimension_semantics=("arbitrary",),
                                             vmem_limit_bytes=VMEM_LIMIT),
        name="adaln",
    )(c.T, w_ada, b_ada.reshape(1, n))


def _in_proj_kernel(n_blocks, x_ref, mod_ref, gnorm_ref, pos_ref, freq_ref, whi_ref,
                    wconv_ref, gconv_ref,
                    q_ref, k_ref, v_ref, gz_ref, yc_ref,
                    kbd_ref, carry_ref, cos_ref, sin_ref, proj_ref, h_ref):
    step = pl.program_id(1)

    @pl.when(step == 0)
    def _():
        kbd_ref[...] = jnp.zeros_like(kbd_ref)
        carry_ref[...] = jnp.zeros_like(carry_ref)

    n_cols = whi_ref.shape[1]
    col_groups = [(0, 2 * D_ATTN), (2 * D_ATTN, 3 * D_ATTN), (3 * D_ATTN, 4 * D_ATTN),
                  (4 * D_ATTN, n_cols)]

    def norm(r):
        x = x_ref[0, r * TILE:(r + 1) * TILE, :]
        xn = x * lax.rsqrt(jnp.mean(x * x, axis=-1, keepdims=True) + EPS) * gnorm_ref[...]
        h_ref[r] = (xn * (1.0 + mod_ref[0, 1:2, :]) + mod_ref[0, 0:1, :]).astype(BF16)

    def project(r, group):
        lo, hi = col_groups[group]
        for c0 in range(lo, hi, D_ATTN):
            proj_ref[r * TILE:(r + 1) * TILE, c0:c0 + D_ATTN] = jnp.dot(
                h_ref[r], whi_ref[:, c0:c0 + D_ATTN], preferred_element_type=F32)

    posts = [_block_posts(n_blocks, r, step * TILES_PER_STEP + r, pos_ref, freq_ref, wconv_ref,
                          gconv_ref, q_ref, k_ref, v_ref, gz_ref, yc_ref, kbd_ref, carry_ref,
                          cos_ref, sin_ref, proj_ref)
             for r in range(TILES_PER_STEP)]
    stages = [(r, group) for r in range(TILES_PER_STEP) for group in range(len(col_groups))]
    norm(0)
    project(*stages[0])
    for s, (r, group) in enumerate(stages):
        if s + 1 < len(stages):
            if stages[s + 1][1] == 0:
                norm(stages[s + 1][0])
            project(*stages[s + 1])
        posts[r][group]()


def _block_posts(n_blocks, r, i, pos_ref, freq_ref, wconv_ref, gconv_ref,
                 q_ref, k_ref, v_ref, gz_ref, yc_ref, kbd_ref, carry_ref, cos_ref, sin_ref,
                 proj_ref):
    rows = slice(r * TILE, (r + 1) * TILE)

    def proj(c0, c1):
        return proj_ref[rows, c0:c1]

    def post_qk():
        lane_c = lax.broadcasted_iota(jnp.int32, (TILE // 4, LANES), 1)
        ang = pos_ref[0, r * TILE // 4:(r + 1) * TILE // 4, :].astype(F32) * freq_ref[...]
        sign = jnp.where((lane_c % HEAD_DIM) < HALF, -1.0, 1.0)
        for table_ref, table, scale in ((cos_ref, jnp.cos(ang), None), (sin_ref, jnp.sin(ang), sign)):
            for a in range(LANES // HALF):
                z = jnp.where(lane_c // HALF == a, table, 0.0)
                z = z + pltpu.roll(z, 2 * HALF, 1)
                z = z + pltpu.roll(z, HALF, 1)
                table_ref[r, pl.ds(a, TILE // 4, stride=LANES // HALF), :] = (
                    z if scale is None else z * scale)
        cos = cos_ref[r]
        sin = sin_ref[r]

        lane = lax.broadcasted_iota(jnp.int32, (TILE, LANES), 1)
        first_half = (lane % HEAD_DIM) < HALF

        def rope(t):
            partner = jnp.where(first_half, pltpu.roll(t, LANES - HALF, 1), pltpu.roll(t, HALF, 1))
            return t * cos + partner * sin

        n_groups = D_ATTN // LANES
        q_groups = [rope(proj(g * LANES, (g + 1) * LANES)) for g in range(n_groups)]
        k_groups = [rope(proj(D_ATTN + g * LANES, D_ATTN + (g + 1) * LANES)) for g in range(n_groups)]
        q = jnp.concatenate(q_groups, axis=1)
        k = jnp.concatenate(k_groups, axis=1)

        nt = (((1,), (1,)), ((), ()))
        scores = lax.dot_general(kbd_ref[...].astype(BF16), q.astype(BF16), nt,
                                 preferred_element_type=F32)
        g3 = scores.reshape(N_HEADS, n_blocks, TILE)
        blk = lax.broadcasted_iota(jnp.int32, g3.shape, 1)
        past = blk < i
        g3 = jnp.where(past, g3, NEG)
        picked = jnp.zeros(g3.shape, jnp.bool_)
        for _ in range(min(MOBA_TOPK, n_blocks - 1)):
            top = jnp.max(g3, axis=1, keepdims=True)
            first = jnp.min(jnp.where(g3 == top, blk, n_blocks), axis=1, keepdims=True)
            hit = blk == first
            picked = jnp.logical_or(picked, hit)
            g3 = jnp.where(hit, -jnp.inf, g3)
        keep = jnp.logical_or(jnp.logical_and(picked, past), blk == i)
        bias = jnp.where(keep, 0.0, NEG)

        k_mean = jnp.mean(k, axis=0, keepdims=True)
        lane_w = lax.broadcasted_iota(jnp.int32, (1, D_ATTN), 1)
        for hd in range(N_HEADS):
            own = (lane_w // HEAD_DIM) == hd
            kbd_ref[pl.ds(hd * n_blocks + i, 1), :] = jnp.where(own, k_mean, 0.0)

        q_t = (q * (HEAD_DIM ** -0.5 * LOG2E)).T
        pad = jnp.zeros((HEAD_DIM - n_blocks, TILE), F32)
        parts = []
        for hd in range(N_HEADS):
            dims = q_t[hd * HEAD_DIM:(hd + 1) * HEAD_DIM]
            parts += [dims, bias[hd], pad] if hd % 2 == 0 else [bias[hd], pad, dims]
        q_ref[0, :, rows] = jnp.concatenate(parts, axis=0).astype(BF16)

        low = lane < HEAD_DIM
        onehot_even = jnp.where(lane == HEAD_DIM + i, 1.0, 0.0)
        onehot_odd = jnp.where(lane == i, 1.0, 0.0)
        for g in range(n_groups):
            even = slice(2 * g * LANES, (2 * g + 1) * LANES)
            odd = slice((2 * g + 1) * LANES, (2 * g + 2) * LANES)
            k_ref[0, rows, even] = jnp.where(low, k_groups[g], onehot_even).astype(BF16)
            k_ref[0, rows, odd] = jnp.where(low, onehot_odd, k_groups[g]).astype(BF16)

    def post_v():
        v_t = proj(2 * D_ATTN, 3 * D_ATTN).T
        ones = jnp.ones((V_ROWS - HEAD_DIM, TILE), F32)
        parts = []
        for hd in range(N_HEADS):
            parts += [v_t[hd * HEAD_DIM:(hd + 1) * HEAD_DIM], ones]
        v_ref[0, :, rows] = jnp.concatenate(parts, axis=0).astype(BF16)

    def post_gz():
        gz_ref[0, rows, :] = _silu(proj(3 * D_ATTN, 4 * D_ATTN)).astype(BF16)

    def post_conv():
        c0 = 4 * D_ATTN
        b_g = proj(c0, c0 + D_CONV)
        u = proj(c0 + D_CONV, c0 + 2 * D_CONV) * proj(c0 + 2 * D_CONV, c0 + 3 * D_CONV)
        z_c = proj(c0 + 3 * D_CONV, c0 + 4 * D_CONV)
        row = lax.broadcasted_iota(jnp.int32, u.shape, 0)
        prev1 = carry_ref[7:8, :]
        prev2 = carry_ref[6:7, :]
        u1 = jnp.where(row == 0, prev1, pltpu.roll(u, 1, 0))
        u2 = jnp.where(row == 0, prev2, jnp.where(row == 1, prev1, pltpu.roll(u, 2, 0)))
        carry_ref[...] = u[TILE - 8:TILE, :]
        y = b_g * (wconv_ref[2:3, :] * u + wconv_ref[1:2, :] * u1 + wconv_ref[0:1, :] * u2)
        y = y * lax.rsqrt(jnp.mean(y * y, axis=-1, keepdims=True) + EPS) * gconv_ref[...]
        yc_ref[0, rows, :] = (y * _silu(z_c)).astype(BF16)

    return [post_qk, post_v, post_gz, post_conv]


def _in_proj(x, mod3, g_norm, positions, inv_freq, w_hi, w_conv, g_conv_out):
    bsz, seq, _ = x.shape
    n_blocks = seq // TILE
    step_rows = TILES_PER_STEP * TILE
    wide = N_HEADS * LANES
    row_spec = lambda width: pl.BlockSpec((1, step_rows, width), lambda b, i: (b, i, 0))
    col_spec = lambda height: pl.BlockSpec((1, height, step_rows), lambda b, i: (b, 0, i))
    const = lambda shape: pl.BlockSpec(shape, lambda b, i: (0,) * len(shape))
    kbd_rows = N_HEADS * n_blocks
    per_row = LANES // HALF
    pos_compact = jnp.repeat(positions.reshape(bsz, seq // per_row, per_row), HALF, axis=2)
    return pl.pallas_call(
        functools.partial(_in_proj_kernel, n_blocks),
        grid=(bsz, n_blocks // TILES_PER_STEP),
        in_specs=[row_spec(D_MODEL),
                  pl.BlockSpec((1, 3, D_MODEL), lambda b, i: (b, 0, 0)),
                  const((1, D_MODEL)),
                  pl.BlockSpec((1, step_rows // per_row, LANES), lambda b, i: (b, i, 0)),
                  const((1, LANES)),
                  const(w_hi.shape),
                  const((CONV_WIDTH, D_CONV)),
                  const((1, D_CONV))],
        out_specs=[col_spec(wide), row_spec(wide), col_spec(N_HEADS * V_ROWS),
                   row_spec(D_ATTN), row_spec(D_CONV)],
        out_shape=[jax.ShapeDtypeStruct((bsz, wide, seq), BF16),
                   jax.ShapeDtypeStruct((bsz, seq, wide), BF16),
                   jax.ShapeDtypeStruct((bsz, N_HEADS * V_ROWS, seq), BF16),
                   jax.ShapeDtypeStruct((bsz, seq, D_ATTN), BF16),
                   jax.ShapeDtypeStruct((bsz, seq, D_CONV), BF16)],
        scratch_shapes=[pltpu.VMEM((kbd_rows, D_ATTN), F32),
                        pltpu.VMEM((8, D_CONV), F32),
                        pltpu.VMEM((TILES_PER_STEP, TILE, LANES), F32),
                        pltpu.VMEM((TILES_PER_STEP, TILE, LANES), F32),
                        pltpu.VMEM((step_rows, w_hi.shape[1]), F32),
                        pltpu.VMEM((TILES_PER_STEP, TILE, D_MODEL), BF16)],
        compiler_params=pltpu.CompilerParams(dimension_semantics=("arbitrary", "arbitrary"),
                                             vmem_limit_bytes=VMEM_LIMIT),
        name="in_proj",
    )(x, mod3, g_norm.reshape(1, D_MODEL), pos_compact, inv_freq,
      w_hi, w_conv, g_conv_out.reshape(1, D_CONV))


def _moba_kernel(n_chunks, tile_ref, chunk_ref, q_ref, k_ref, v_ref, o_ref, s_ref, smax_ref,
                 m_ref, acc_ref):
    n_past = n_chunks * (n_chunks - 1) // 2

    def own_item(t):
        return t, t

    def past_item(j):
        return tile_ref[j], chunk_ref[j]

    def own_keys(half):
        return (half + 1) * MXU_COLS

    def issue_unit(buf, tile, chunk, hd, half, own):
        n_keys = own_keys(half) if own else CHUNK
        start = pl.multiple_of(chunk * CHUNK, CHUNK)
        q_start = pl.multiple_of(tile * CHUNK + half * MXU_COLS, MXU_COLS)
        rows = slice(hd * LANES, (hd + 1) * LANES)
        cols = slice(half * MXU_COLS, (half + 1) * MXU_COLS)
        kj = k_ref[0, pl.ds(start, n_keys), rows]
        s = jnp.dot(kj, q_ref[0, rows, pl.ds(q_start, MXU_COLS)],
                    preferred_element_type=F32)
        s_ref[buf, hd, :n_keys, cols] = s.astype(BF16)
        if not own:
            smax_ref[buf, hd, :, cols] = jnp.max(s, axis=0, keepdims=True)

    def absorb_unit(buf, tile, chunk, hd, half, own):
        n_keys = own_keys(half) if own else CHUNK
        start = pl.multiple_of(chunk * CHUNK, CHUNK)
        cols = slice(half * MXU_COLS, (half + 1) * MXU_COLS)
        m = m_ref[tile, hd, :, cols]
        if own:
            diag = s_ref[buf, hd, n_keys - MXU_COLS:n_keys, cols].astype(F32)
            key = lax.broadcasted_iota(jnp.int32, diag.shape, 0)
            qry = lax.broadcasted_iota(jnp.int32, diag.shape, 1)
            diag = jnp.where(key <= qry, diag, NEG)
            blocks = [s_ref[buf, hd, b * MXU_COLS:(b + 1) * MXU_COLS, cols] for b in range(half)]
            blocks.append(diag.astype(BF16))
            smax = jnp.max(diag, axis=0, keepdims=True)
            for block in blocks[:-1]:
                smax = jnp.maximum(smax, jnp.max(block, axis=0, keepdims=True).astype(F32))
        else:
            blocks = [s_ref[buf, hd, :, cols]]
            smax = smax_ref[buf, hd, :, cols]
        m_new = jnp.maximum(m, smax).astype(BF16)
        p = [jnp.exp2(block - m_new) for block in blocks]
        p = p[0] if len(p) == 1 else jnp.concatenate(p, axis=0)
        m_new = m_new.astype(F32)
        vj = v_ref[0, hd * V_ROWS:(hd + 1) * V_ROWS, pl.ds(start, n_keys)]
        acc_ref[tile, hd, :, cols] = (jnp.exp2(m - m_new) * acc_ref[tile, hd, :, cols]
                                      + jnp.dot(vj, p, preferred_element_type=F32))
        m_ref[tile, hd, :, cols] = m_new

    def step(issue=None, absorb=None, issue_own=False, absorb_own=False):
        for hd in range(2):
            for half in range(CHUNK // MXU_COLS):
                if issue is not None:
                    issue_unit(*issue, hd, half, issue_own)
                if absorb is not None:
                    absorb_unit(*absorb, hd, half, absorb_own)

    def run_pairs(item, n_pairs, pairs_per_trip, own):
        def pair(u):
            step(issue=(1, *item(2 * u + 1)), absorb=(0, *item(2 * u)),
                 issue_own=own, absorb_own=own)
            step(issue=(0, *item(2 * u + 2)), absorb=(1, *item(2 * u + 1)),
                 issue_own=own, absorb_own=own)

        def trip(t, carry):
            for u in range(pairs_per_trip):
                pair(pairs_per_trip * t + u)
            return carry

        trips = n_pairs // pairs_per_trip
        lax.fori_loop(0, trips, trip, 0)
        for u in range(trips * pairs_per_trip, n_pairs):
            pair(u)

    m_ref[...] = jnp.full(m_ref.shape, -jnp.inf, F32)
    acc_ref[...] = jnp.zeros(acc_ref.shape, F32)
    assert n_chunks % 2 == 0 and n_past % 2 == 0
    step(issue=(0, *own_item(0)), issue_own=True)
    run_pairs(own_item, (n_chunks - 2) // 2, OWN_PAIRS_PER_TRIP, own=True)
    step(issue=(1, *own_item(n_chunks - 1)), absorb=(0, *own_item(n_chunks - 2)),
         issue_own=True, absorb_own=True)
    step(issue=(0, *past_item(0)), absorb=(1, *own_item(n_chunks - 1)), absorb_own=True)
    run_pairs(past_item, (n_past - 2) // 2, PAST_PAIRS_PER_TRIP, own=False)
    step(issue=(1, *past_item(n_past - 1)), absorb=(0, *past_item(n_past - 2)))
    step(absorb=(1, *past_item(n_past - 1)))

    def finish(tile, carry):
        outs = [acc_ref[tile, hd, :HEAD_DIM] / acc_ref[tile, hd, HEAD_DIM:HEAD_DIM + 1]
                for hd in range(2)]
        start = pl.multiple_of(tile * CHUNK, CHUNK)
        o_ref[0, pl.ds(start, CHUNK), :] = jnp.concatenate(outs, axis=0).T.astype(o_ref.dtype)
        return carry

    lax.fori_loop(0, n_chunks, finish, 0)


def _moba(q_aug, k_aug, v_aug):
    bsz, seq, wide = k_aug.shape
    pairs = wide // (2 * LANES)
    n_chunks = seq // CHUNK
    past = [(t, c) for t in range(n_chunks) for c in range(t)]
    item_tile = jnp.asarray([t for t, _ in past], jnp.int32)
    item_chunk = jnp.asarray([c for _, c in past], jnp.int32)
    return pl.pallas_call(
        functools.partial(_moba_kernel, n_chunks),
        grid_spec=pltpu.PrefetchScalarGridSpec(
            num_scalar_prefetch=2,
            grid=(bsz, pairs),
            in_specs=[pl.BlockSpec((1, 2 * LANES, seq), lambda b, p, *_: (b, p, 0)),
                      pl.BlockSpec((1, seq, 2 * LANES), lambda b, p, *_: (b, 0, p)),
                      pl.BlockSpec((1, 2 * V_ROWS, seq), lambda b, p, *_: (b, p, 0))],
            out_specs=pl.BlockSpec((1, seq, LANES), lambda b, p, *_: (b, 0, p)),
            scratch_shapes=[pltpu.VMEM((2, 2, CHUNK, CHUNK), BF16),
                            pltpu.VMEM((2, 2, 1, CHUNK), F32),
                            pltpu.VMEM((n_chunks, 2, 1, CHUNK), F32),
                            pltpu.VMEM((n_chunks, 2, V_ROWS, CHUNK), F32)]),
        out_shape=jax.ShapeDtypeStruct((bsz, seq, pairs * LANES), BF16),
        compiler_params=pltpu.CompilerParams(
            dimension_semantics=("arbitrary", "arbitrary"),
            vmem_limit_bytes=VMEM_LIMIT),
        name="moba",
    )(item_tile, item_chunk, q_aug, k_aug, v_aug)


def _out_proj_kernel(ya_ref, gz_ref, yc_ref, x_ref, mod_ref, gattn_ref, wout_ref, gfin_ref, o_ref):
    ya = ya_ref[0].astype(F32)
    yn = ya * lax.rsqrt(jnp.mean(ya * ya, axis=-1, keepdims=True) + EPS) * gattn_ref[...]
    yn = (yn * gz_ref[0].astype(F32)).astype(BF16)
    y = (jnp.dot(yn, wout_ref[0:D_ATTN, :], preferred_element_type=F32)
         + jnp.dot(yc_ref[0], wout_ref[D_ATTN:, :], preferred_element_type=F32))
    xo = x_ref[0] + mod_ref[0, 2:3, :] * y
    o_ref[0] = xo * lax.rsqrt(jnp.mean(xo * xo, axis=-1, keepdims=True) + EPS) * gfin_ref[...]


def _out_proj(y_attn, gz, yc, x, mod3, g_attn_out, w_out, g_final):
    bsz, seq, _ = x.shape
    row_spec = lambda width: pl.BlockSpec((1, OUT_TILE, width), lambda b, i: (b, i, 0))
    const = lambda shape: pl.BlockSpec(shape, lambda b, i: (0,) * len(shape))
    return pl.pallas_call(
        _out_proj_kernel,
        grid=(bsz, seq // OUT_TILE),
        in_specs=[row_spec(D_ATTN), row_spec(D_ATTN), row_spec(D_CONV), row_spec(D_MODEL),
                  pl.BlockSpec((1, 3, D_MODEL), lambda b, i: (b, 0, 0)),
                  const((1, D_ATTN)), const(w_out.shape), const((1, D_MODEL))],
        out_specs=row_spec(D_MODEL),
        out_shape=jax.ShapeDtypeStruct((bsz, seq, D_MODEL), F32),
        compiler_params=pltpu.CompilerParams(dimension_semantics=("arbitrary", "arbitrary"),
                                             vmem_limit_bytes=VMEM_LIMIT),
        name="out_proj",
    )(y_attn, gz, yc, x, mod3, g_attn_out.reshape(1, D_ATTN), w_out, g_final.reshape(1, D_MODEL))


def kernel(x, c, positions, w_ada, b_ada, g_norm, w_in, w_conv, g_attn_out, g_conv_out, w_out, g_final):
    bsz, seq, _ = x.shape
    assert seq % (TILES_PER_STEP * TILE) == 0 and seq % OUT_TILE == 0 and seq // TILE <= HEAD_DIM // 2
    mod3 = _adaln(c, w_ada, b_ada).reshape(bsz, 3, D_MODEL)

    inv_freq = ROPE_THETA ** (-jnp.arange(HALF, dtype=F32) / HALF)
    inv_freq = jnp.tile(inv_freq, LANES // HALF).reshape(1, LANES)

    q_aug, k_aug, v_aug, gz, yc = _in_proj(x, mod3, g_norm, positions, inv_freq,
                                           w_in.astype(BF16), w_conv, g_conv_out)
    y_attn = _moba(q_aug, k_aug, v_aug)
    return _out_proj(y_attn, gz, yc, x, mod3, g_attn_out, w_out.astype(BF16), g_final)
```

```python
import functools

import jax
import jax.numpy as jnp
from jax import lax
from jax.experimental import pallas as pl
from jax.experimental.pallas import tpu as pltpu

D_MODEL = 1024
D_ATTN = 512
D_CONV = 512
N_HEADS = 8
HEAD_DIM = 64
HALF = HEAD_DIM // 2
CONV_WIDTH = 3
MOBA_BLOCK = 256
MOBA_TOPK = 3
ROPE_THETA = 10000.0
EPS = 1e-6
NEG = -1e30
LOG2E = 1.4426950408889634

LANES = 128
V_ROWS = HEAD_DIM + 16
TILE = MOBA_BLOCK
CHUNK = 2 * TILE
TILES_PER_STEP = 2
MXU_COLS = 256
OWN_PAIRS_PER_TRIP = 7
PAST_PAIRS_PER_TRIP = 5
OUT_TILE = 1024
VMEM_LIMIT = 56 * 1024 * 1024

F32 = jnp.float32
BF16 = jnp.bfloat16


def _silu(z):
    return z * (1.0 / (1.0 + jnp.exp(-z)))


def _split_bf16(a):
    hi = a.astype(BF16)
    lo = (a - hi.astype(F32)).astype(BF16)
    return hi, lo


def _adaln_kernel(c_ref, w_ref, b_ref, o_ref):
    act = _silu(c_ref[...])
    rows = [jnp.sum(act[:, b:b + 1] * w_ref[...], axis=0, keepdims=True)
            for b in range(act.shape[1])]
    o_ref[...] = jnp.concatenate(rows, axis=0) + b_ref[...]


def _adaln(c, w_ada, b_ada):
    bsz = c.shape[0]
    n = w_ada.shape[1]
    bn = D_MODEL
    return pl.pallas_call(
        _adaln_kernel,
        grid=(n // bn,),
        in_specs=[pl.BlockSpec((D_MODEL, bsz), lambda j: (0, 0)),
                  pl.BlockSpec((D_MODEL, bn), lambda j: (0, j)),
                  pl.BlockSpec((1, bn), lambda j: (0, j))],
        out_specs=pl.BlockSpec((bsz, bn), lambda j: (0, j)),
        out_shape=jax.ShapeDtypeStruct((bsz, n), F32),
        compiler_params=pltpu.CompilerParams(dimension_semantics=("arbitrary",),
                                             vmem_limit_bytes=VMEM_LIMIT),
        name="adaln",
    )(c.T, w_ada, b_ada.reshape(1, n))


def _in_proj_kernel(n_blocks, x_ref, mod_ref, gnorm_ref, pos_ref, freq_ref, whi_ref,
                    wconv_ref, gconv_ref,
                    q_ref, k_ref, v_ref, gz_ref, yc_ref,
                    kbd_ref, carry_ref, cos_ref, sin_ref, proj_ref, h_ref):
    step = pl.program_id(1)

    @pl.when(step == 0)
    def _():
        kbd_ref[...] = jnp.zeros_like(kbd_ref)
        carry_ref[...] = jnp.zeros_like(carry_ref)

    n_cols = whi_ref.shape[0] * D_ATTN
    col_groups = [(0, 2 * D_ATTN), (2 * D_ATTN, 3 * D_ATTN), (3 * D_ATTN, 4 * D_ATTN),
                  (4 * D_ATTN, n_cols)]

    def norm(r):
        x = x_ref[0, r * TILE:(r + 1) * TILE, :]
        xn = x * lax.rsqrt(jnp.mean(x * x, axis=-1, keepdims=True) + EPS) * gnorm_ref[...]
        h_ref[r] = (xn * (1.0 + mod_ref[0, 1:2, :]) + mod_ref[0, 0:1, :]).astype(BF16)

    def project(r, group):
        lo, hi = col_groups[group]
        for c0 in range(lo, hi, D_ATTN):
            proj_ref[r * TILE:(r + 1) * TILE, c0:c0 + D_ATTN] = jnp.dot(
                h_ref[r], whi_ref[c0 // D_ATTN], preferred_element_type=F32)

    posts = [_block_posts(n_blocks, r, step * TILES_PER_STEP + r, pos_ref, freq_ref, wconv_ref,
                          gconv_ref, q_ref, k_ref, v_ref, gz_ref, yc_ref, kbd_ref, carry_ref,
                          cos_ref, sin_ref, proj_ref)
             for r in range(TILES_PER_STEP)]
    stages = [(r, group) for r in range(TILES_PER_STEP) for group in range(len(col_groups))]
    norm(0)
    project(*stages[0])
    for s, (r, group) in enumerate(stages):
        if s + 1 < len(stages):
            if stages[s + 1][1] == 0:
                norm(stages[s + 1][0])
            project(*stages[s + 1])
        posts[r][group]()


def _block_posts(n_blocks, r, i, pos_ref, freq_ref, wconv_ref, gconv_ref,
                 q_ref, k_ref, v_ref, gz_ref, yc_ref, kbd_ref, carry_ref, cos_ref, sin_ref,
                 proj_ref):
    rows = slice(r * TILE, (r + 1) * TILE)

    def proj(c0, c1):
        return proj_ref[rows, c0:c1]

    def post_qk():
        lane_c = lax.broadcasted_iota(jnp.int32, (TILE // 4, LANES), 1)
        ang = pos_ref[0, r * TILE // 4:(r + 1) * TILE // 4, :].astype(F32) * freq_ref[...]
        sign = jnp.where((lane_c % HEAD_DIM) < HALF, -1.0, 1.0)
        for table_ref, table, scale in ((cos_ref, jnp.cos(ang), None), (sin_ref, jnp.sin(ang), sign)):
            for a in range(LANES // HALF):
                z = jnp.where(lane_c // HALF == a, table, 0.0)
                z = z + pltpu.roll(z, 2 * HALF, 1)
                z = z + pltpu.roll(z, HALF, 1)
                table_ref[r, pl.ds(a, TILE // 4, stride=LANES // HALF), :] = (
                    z if scale is None else z * scale)
        cos = cos_ref[r]
        sin = sin_ref[r]

        lane = lax.broadcasted_iota(jnp.int32, (TILE, LANES), 1)
        first_half = (lane % HEAD_DIM) < HALF

        def rope(t):
            partner = jnp.where(first_half, pltpu.roll(t, LANES - HALF, 1), pltpu.roll(t, HALF, 1))
            return t * cos + partner * sin

        n_groups = D_ATTN // LANES
        q_groups = [rope(proj(g * LANES, (g + 1) * LANES)) for g in range(n_groups)]
        k_groups = [rope(proj(D_ATTN + g * LANES, D_ATTN + (g + 1) * LANES)) for g in range(n_groups)]
        q = jnp.concatenate(q_groups, axis=1)
        k = jnp.concatenate(k_groups, axis=1)

        nt = (((1,), (1,)), ((), ()))
        scores = lax.dot_general(kbd_ref[...].astype(BF16), q.astype(BF16), nt,
                                 preferred_element_type=F32)
        g3 = scores.reshape(N_HEADS, n_blocks, TILE)
        blk = lax.broadcasted_iota(jnp.int32, g3.shape, 1)
        past = blk < i
        g3 = jnp.where(past, g3, NEG)
        picked = jnp.zeros(g3.shape, jnp.bool_)
        for _ in range(min(MOBA_TOPK, n_blocks - 1)):
            top = jnp.max(g3, axis=1, keepdims=True)
            first = jnp.min(jnp.where(g3 == top, blk, n_blocks), axis=1, keepdims=True)
            hit = blk == first
            picked = jnp.logical_or(picked, hit)
            g3 = jnp.where(hit, -jnp.inf, g3)
        keep = jnp.logical_or(jnp.logical_and(picked, past), blk == i)
        bias = jnp.where(keep, 0.0, NEG)

        k_mean = jnp.mean(k, axis=0, keepdims=True)
        lane_w = lax.broadcasted_iota(jnp.int32, (1, D_ATTN), 1)
        for hd in range(N_HEADS):
            own = (lane_w // HEAD_DIM) == hd
            kbd_ref[pl.ds(hd * n_blocks + i, 1), :] = jnp.where(own, k_mean, 0.0)

        q_t = (q * (HEAD_DIM ** -0.5 * LOG2E)).T
        pad = jnp.zeros((HEAD_DIM - n_blocks, TILE), F32)
        parts = []
        for hd in range(N_HEADS):
            dims = q_t[hd * HEAD_DIM:(hd + 1) * HEAD_DIM]
            parts += [dims, bias[hd], pad] if hd % 2 == 0 else [bias[hd], pad, dims]
        q_ref[0, :, rows] = jnp.concatenate(parts, axis=0).astype(BF16)

        low = lane < HEAD_DIM
        onehot_even = jnp.where(lane == HEAD_DIM + i, 1.0, 0.0)
        onehot_odd = jnp.where(lane == i, 1.0, 0.0)
        for g in range(n_groups):
            even = slice(2 * g * LANES, (2 * g + 1) * LANES)
            odd = slice((2 * g + 1) * LANES, (2 * g + 2) * LANES)
            k_ref[0, rows, even] = jnp.where(low, k_groups[g], onehot_even).astype(BF16)
            k_ref[0, rows, odd] = jnp.where(low, onehot_odd, k_groups[g]).astype(BF16)

    def post_v():
        v_t = proj(2 * D_ATTN, 3 * D_ATTN).T
        ones = jnp.ones((V_ROWS - HEAD_DIM, TILE), F32)
        parts = []
        for hd in range(N_HEADS):
            parts += [v_t[hd * HEAD_DIM:(hd + 1) * HEAD_DIM], ones]
        v_ref[0, :, rows] = jnp.concatenate(parts, axis=0).astype(BF16)

    def post_gz():
        gz_ref[0, rows, :] = _silu(proj(3 * D_ATTN, 4 * D_ATTN)).astype(BF16)

    def post_conv():
        c0 = 4 * D_ATTN
        b_g = proj(c0, c0 + D_CONV)
        u = proj(c0 + D_CONV, c0 + 2 * D_CONV) * proj(c0 + 2 * D_CONV, c0 + 3 * D_CONV)
        z_c = proj(c0 + 3 * D_CONV, c0 + 4 * D_CONV)
        row = lax.broadcasted_iota(jnp.int32, u.shape, 0)
        prev1 = carry_ref[7:8, :]
        prev2 = carry_ref[6:7, :]
        u1 = jnp.where(row == 0, prev1, pltpu.roll(u, 1, 0))
        u2 = jnp.where(row == 0, prev2, jnp.where(row == 1, prev1, pltpu.roll(u, 2, 0)))
        carry_ref[...] = u[TILE - 8:TILE, :]
        y = b_g * (wconv_ref[2:3, :] * u + wconv_ref[1:2, :] * u1 + wconv_ref[0:1, :] * u2)
        y = y * lax.rsqrt(jnp.mean(y * y, axis=-1, keepdims=True) + EPS) * gconv_ref[...]
        yc_ref[0, rows, :] = (y * _silu(z_c)).astype(BF16)

    return [post_qk, post_v, post_gz, post_conv]


def _in_proj(x, mod3, g_norm, positions, inv_freq, w_hi, w_conv, g_conv_out):
    bsz, seq, _ = x.shape
    n_blocks = seq // TILE
    step_rows = TILES_PER_STEP * TILE
    wide = N_HEADS * LANES
    row_spec = lambda width: pl.BlockSpec((1, step_rows, width), lambda b, i: (b, i, 0))
    col_spec = lambda height: pl.BlockSpec((1, height, step_rows), lambda b, i: (b, 0, i))
    const = lambda shape: pl.BlockSpec(shape, lambda b, i: (0,) * len(shape))
    kbd_rows = N_HEADS * n_blocks
    per_row = LANES // HALF
    pos_compact = jnp.repeat(positions.reshape(bsz, seq // per_row, per_row), HALF, axis=2)
    return pl.pallas_call(
        functools.partial(_in_proj_kernel, n_blocks),
        grid=(bsz, n_blocks // TILES_PER_STEP),
        in_specs=[row_spec(D_MODEL),
                  pl.BlockSpec((1, 3, D_MODEL), lambda b, i: (b, 0, 0)),
                  const((1, D_MODEL)),
                  pl.BlockSpec((1, step_rows // per_row, LANES), lambda b, i: (b, i, 0)),
                  const((1, LANES)),
                  const(w_hi.shape),
                  const((CONV_WIDTH, D_CONV)),
                  const((1, D_CONV))],
        out_specs=[col_spec(wide), row_spec(wide), col_spec(N_HEADS * V_ROWS),
                   row_spec(D_ATTN), row_spec(D_CONV)],
        out_shape=[jax.ShapeDtypeStruct((bsz, wide, seq), BF16),
                   jax.ShapeDtypeStruct((bsz, seq, wide), BF16),
                   jax.ShapeDtypeStruct((bsz, N_HEADS * V_ROWS, seq), BF16),
                   jax.ShapeDtypeStruct((bsz, seq, D_ATTN), BF16),
                   jax.ShapeDtypeStruct((bsz, seq, D_CONV), BF16)],
        scratch_shapes=[pltpu.VMEM((kbd_rows, D_ATTN), F32),
                        pltpu.VMEM((8, D_CONV), F32),
                        pltpu.VMEM((TILES_PER_STEP, TILE, LANES), F32),
                        pltpu.VMEM((TILES_PER_STEP, TILE, LANES), F32),
                        pltpu.VMEM((step_rows, w_hi.shape[0] * w_hi.shape[2]), F32),
                        pltpu.VMEM((TILES_PER_STEP, TILE, D_MODEL), BF16)],
        compiler_params=pltpu.CompilerParams(dimension_semantics=("arbitrary", "arbitrary"),
                                             vmem_limit_bytes=VMEM_LIMIT),
        name="in_proj",
    )(x, mod3, g_norm.reshape(1, D_MODEL), pos_compact, inv_freq,
      w_hi, w_conv, g_conv_out.reshape(1, D_CONV))


def _moba_kernel(n_chunks, tile_ref, chunk_ref, q_ref, k_ref, v_ref, o_ref, s_ref, smax_ref,
                 m_ref, acc_ref):
    n_past = n_chunks * (n_chunks - 1) // 2

    def own_item(t):
        return t, t

    def past_item(j):
        return tile_ref[j], chunk_ref[j]

    def own_keys(half):
        return (half + 1) * MXU_COLS

    def issue_unit(buf, tile, chunk, hd, half, own):
        n_keys = own_keys(half) if own else CHUNK
        start = pl.multiple_of(chunk * CHUNK, CHUNK)
        q_start = pl.multiple_of(tile * CHUNK + half * MXU_COLS, MXU_COLS)
        rows = slice(hd * LANES, (hd + 1) * LANES)
        cols = slice(half * MXU_COLS, (half + 1) * MXU_COLS)
        kj = k_ref[0, pl.ds(start, n_keys), rows]
        s = jnp.dot(kj, q_ref[0, rows, pl.ds(q_start, MXU_COLS)],
                    preferred_element_type=F32)
        s_ref[buf, hd, :n_keys, cols] = s.astype(BF16)
        if not own:
            smax_ref[buf, hd, :, cols] = jnp.max(s, axis=0, keepdims=True)

    def absorb_unit(buf, tile, chunk, hd, half, own):
        n_keys = own_keys(half) if own else CHUNK
        start = pl.multiple_of(chunk * CHUNK, CHUNK)
        cols = slice(half * MXU_COLS, (half + 1) * MXU_COLS)
        m = m_ref[tile, hd, :, cols]
        if own:
            diag = s_ref[buf, hd, n_keys - MXU_COLS:n_keys, cols].astype(F32)
            key = lax.broadcasted_iota(jnp.int32, diag.shape, 0)
            qry = lax.broadcasted_iota(jnp.int32, diag.shape, 1)
            diag = jnp.where(key <= qry, diag, NEG)
            blocks = [s_ref[buf, hd, b * MXU_COLS:(b + 1) * MXU_COLS, cols] for b in range(half)]
            blocks.append(diag.astype(BF16))
            smax = jnp.max(diag, axis=0, keepdims=True)
            for block in blocks[:-1]:
                smax = jnp.maximum(smax, jnp.max(block, axis=0, keepdims=True).astype(F32))
        else:
            blocks = [s_ref[buf, hd, :, cols]]
            smax = smax_ref[buf, hd, :, cols]
        m_new = jnp.maximum(m, smax).astype(BF16)
        p = [jnp.exp2(block - m_new) for block in blocks]
        p = p[0] if len(p) == 1 else jnp.concatenate(p, axis=0)
        m_new = m_new.astype(F32)
        vj = v_ref[0, hd * V_ROWS:(hd + 1) * V_ROWS, pl.ds(start, n_keys)]
        acc_ref[tile, hd, :, cols] = (jnp.exp2(m - m_new) * acc_ref[tile, hd, :, cols]
                                      + jnp.dot(vj, p, preferred_element_type=F32))
        m_ref[tile, hd, :, cols] = m_new

    def step(issue=None, absorb=None, issue_own=False, absorb_own=False):
        for hd in range(2):
            for half in range(CHUNK // MXU_COLS):
                if issue is not None:
                    issue_unit(*issue, hd, half, issue_own)
                if absorb is not None:
                    absorb_unit(*absorb, hd, half, absorb_own)

    def run_pairs(item, n_pairs, pairs_per_trip, own):
        def pair(u):
            step(issue=(1, *item(2 * u + 1)), absorb=(0, *item(2 * u)),
                 issue_own=own, absorb_own=own)
            step(issue=(0, *item(2 * u + 2)), absorb=(1, *item(2 * u + 1)),
                 issue_own=own, absorb_own=own)

        def trip(t, carry):
            for u in range(pairs_per_trip):
                pair(pairs_per_trip * t + u)
            return carry

        trips = n_pairs // pairs_per_trip
        lax.fori_loop(0, trips, trip, 0)
        for u in range(trips * pairs_per_trip, n_pairs):
            pair(u)

    m_ref[...] = jnp.full(m_ref.shape, -jnp.inf, F32)
    acc_ref[...] = jnp.zeros(acc_ref.shape, F32)
    assert n_chunks % 2 == 0 and n_past % 2 == 0
    step(issue=(0, *own_item(0)), issue_own=True)
    run_pairs(own_item, (n_chunks - 2) // 2, OWN_PAIRS_PER_TRIP, own=True)
    step(issue=(1, *own_item(n_chunks - 1)), absorb=(0, *own_item(n_chunks - 2)),
         issue_own=True, absorb_own=True)
    step(issue=(0, *past_item(0)), absorb=(1, *own_item(n_chunks - 1)), absorb_own=True)
    run_pairs(past_item, (n_past - 2) // 2, PAST_PAIRS_PER_TRIP, own=False)
    step(issue=(1, *past_item(n_past - 1)), absorb=(0, *past_item(n_past - 2)))
    step(absorb=(1, *past_item(n_past - 1)))

    def finish(tile, carry):
        outs = [acc_ref[tile, hd, :HEAD_DIM] / acc_ref[tile, hd, HEAD_DIM:HEAD_DIM + 1]
                for hd in range(2)]
        start = pl.multiple_of(tile * CHUNK, CHUNK)
        o_ref[0, pl.ds(start, CHUNK), :] = jnp.concatenate(outs, axis=0).T.astype(o_ref.dtype)
        return carry

    lax.fori_loop(0, n_chunks, finish, 0)


def _moba(q_aug, k_aug, v_aug):
    bsz, seq, wide = k_aug.shape
    pairs = wide // (2 * LANES)
    n_chunks = seq // CHUNK
    past = [(t, c) for t in range(n_chunks) for c in range(t)]
    item_tile = jnp.asarray([t for t, _ in past], jnp.int32)
    item_chunk = jnp.asarray([c for _, c in past], jnp.int32)
    return pl.pallas_call(
        functools.partial(_moba_kernel, n_chunks),
        grid_spec=pltpu.PrefetchScalarGridSpec(
            num_scalar_prefetch=2,
            grid=(bsz, pairs),
            in_specs=[pl.BlockSpec((1, 2 * LANES, seq), lambda b, p, *_: (b, p, 0)),
                      pl.BlockSpec((1, seq, 2 * LANES), lambda b, p, *_: (b, 0, p)),
                      pl.BlockSpec((1, 2 * V_ROWS, seq), lambda b, p, *_: (b, p, 0))],
            out_specs=pl.BlockSpec((1, seq, LANES), lambda b, p, *_: (b, 0, p)),
            scratch_shapes=[pltpu.VMEM((2, 2, CHUNK, CHUNK), BF16),
                            pltpu.VMEM((2, 2, 1, CHUNK), F32),
                            pltpu.VMEM((n_chunks, 2, 1, CHUNK), F32),
                            pltpu.VMEM((n_chunks, 2, V_ROWS, CHUNK), F32)]),
        out_shape=jax.ShapeDtypeStruct((bsz, seq, pairs * LANES), BF16),
        compiler_params=pltpu.CompilerParams(
            dimension_semantics=("arbitrary", "arbitrary"),
            vmem_limit_bytes=VMEM_LIMIT),
        name="moba",
    )(item_tile, item_chunk, q_aug, k_aug, v_aug)


def _out_proj_kernel(ya_ref, gz_ref, yc_ref, x_ref, mod_ref, gattn_ref, wout_ref, gfin_ref, o_ref):
    ya = ya_ref[0].astype(F32)
    yn = ya * lax.rsqrt(jnp.mean(ya * ya, axis=-1, keepdims=True) + EPS) * gattn_ref[...]
    yn = (yn * gz_ref[0].astype(F32)).astype(BF16)
    y = (jnp.dot(yn, wout_ref[0:D_ATTN, :], preferred_element_type=F32)
         + jnp.dot(yc_ref[0], wout_ref[D_ATTN:, :], preferred_element_type=F32))
    xo = x_ref[0] + mod_ref[0, 2:3, :] * y
    o_ref[0] = xo * lax.rsqrt(jnp.mean(xo * xo, axis=-1, keepdims=True) + EPS) * gfin_ref[...]


def _out_proj(y_attn, gz, yc, x, mod3, g_attn_out, w_out, g_final):
    bsz, seq, _ = x.shape
    row_spec = lambda width: pl.BlockSpec((1, OUT_TILE, width), lambda b, i: (b, i, 0))
    const = lambda shape: pl.BlockSpec(shape, lambda b, i: (0,) * len(shape))
    return pl.pallas_call(
        _out_proj_kernel,
        grid=(bsz, seq // OUT_TILE),
        in_specs=[row_spec(D_ATTN), row_spec(D_ATTN), row_spec(D_CONV), row_spec(D_MODEL),
                  pl.BlockSpec((1, 3, D_MODEL), lambda b, i: (b, 0, 0)),
                  const((1, D_ATTN)), const(w_out.shape), const((1, D_MODEL))],
        out_specs=row_spec(D_MODEL),
        out_shape=jax.ShapeDtypeStruct((bsz, seq, D_MODEL), F32),
        compiler_params=pltpu.CompilerParams(dimension_semantics=("arbitrary", "arbitrary"),
                                             vmem_limit_bytes=VMEM_LIMIT),
        name="out_proj",
    )(y_attn, gz, yc, x, mod3, g_attn_out.reshape(1, D_ATTN), w_out, g_final.reshape(1, D_MODEL))


def kernel(x, c, positions, w_ada, b_ada, g_norm, w_in, w_conv, g_attn_out, g_conv_out, w_out, g_final):
    bsz, seq, _ = x.shape
    assert seq % (TILES_PER_STEP * TILE) == 0 and seq % OUT_TILE == 0 and seq // TILE <= HEAD_DIM // 2
    mod3 = _adaln(c, w_ada, b_ada).reshape(bsz, 3, D_MODEL)

    inv_freq = ROPE_THETA ** (-jnp.arange(HALF, dtype=F32) / HALF)
    inv_freq = jnp.tile(inv_freq, LANES // HALF).reshape(1, LANES)

    w_groups = w_in.reshape(D_MODEL, -1, D_ATTN).transpose(1, 0, 2).astype(BF16)
    q_aug, k_aug, v_aug, gz, yc = _in_proj(x, mod3, g_norm, positions, inv_freq,
                                           w_groups, w_conv, g_conv_out)
    y_attn = _moba(q_aug, k_aug, v_aug)
    return _out_proj(y_attn, gz, yc, x, mod3, g_attn_out, w_out.astype(BF16), g_final)
```

```python
import functools

import jax
import jax.numpy as jnp
from jax import lax
from jax.experimental import pallas as pl
from jax.experimental.pallas import tpu as pltpu

D_MODEL = 1024
D_ATTN = 512
D_CONV = 512
N_HEADS = 8
HEAD_DIM = 64
HALF = HEAD_DIM // 2
CONV_WIDTH = 3
MOBA_BLOCK = 256
MOBA_TOPK = 3
ROPE_THETA = 10000.0
EPS = 1e-6
NEG = -1e30
LOG2E = 1.4426950408889634

LANES = 128
V_ROWS = HEAD_DIM + 16
TILE = MOBA_BLOCK
CHUNK = 2 * TILE
TILES_PER_STEP = 2
MXU_COLS = 256
OWN_PAIRS_PER_TRIP = 7
PAST_PAIRS_PER_TRIP = 5
OUT_TILE = 1024
VMEM_LIMIT = 56 * 1024 * 1024

F32 = jnp.float32
BF16 = jnp.bfloat16


def _silu(z):
    return z * (1.0 / (1.0 + jnp.exp(-z)))


def _split_bf16(a):
    hi = a.astype(BF16)
    lo = (a - hi.astype(F32)).astype(BF16)
    return hi, lo


def _adaln_kernel(c_ref, w_ref, b_ref, o_ref):
    act = _silu(c_ref[...])
    rows = [jnp.sum(act[:, b:b + 1] * w_ref[...], axis=0, keepdims=True)
            for b in range(act.shape[1])]
    o_ref[...] = jnp.concatenate(rows, axis=0) + b_ref[...]


def _adaln(c, w_ada, b_ada):
    bsz = c.shape[0]
    n = w_ada.shape[1]
    bn = D_MODEL
    return pl.pallas_call(
        _adaln_kernel,
        grid=(n // bn,),
        in_specs=[pl.BlockSpec((D_MODEL, bsz), lambda j: (0, 0)),
                  pl.BlockSpec((D_MODEL, bn), lambda j: (0, j)),
                  pl.BlockSpec((1, bn), lambda j: (0, j))],
        out_specs=pl.BlockSpec((bsz, bn), lambda j: (0, j)),
        out_shape=jax.ShapeDtypeStruct((bsz, n), F32),
        compiler_params=pltpu.CompilerParams(dimension_semantics=("arbitrary",),
                                             vmem_limit_bytes=VMEM_LIMIT),
        name="adaln",
    )(c.T, w_ada, b_ada.reshape(1, n))


def _in_proj_kernel(n_blocks, n_groups, x_ref, mod_ref, gnorm_ref, pos_ref, freq_ref, *refs):
    w_refs = refs[:n_groups]
    (wconv_ref, gconv_ref, q_ref, k_ref, v_ref, gz_ref, yc_ref,
     kbd_ref, carry_ref, cos_ref, sin_ref, proj_ref, h_ref) = refs[n_groups:]
    step = pl.program_id(1)

    @pl.when(step == 0)
    def _():
        kbd_ref[...] = jnp.zeros_like(kbd_ref)
        carry_ref[...] = jnp.zeros_like(carry_ref)

    n_cols = n_groups * D_ATTN
    col_groups = [(0, 2 * D_ATTN), (2 * D_ATTN, 3 * D_ATTN), (3 * D_ATTN, 4 * D_ATTN),
                  (4 * D_ATTN, n_cols)]

    def norm(r):
        x = x_ref[0, r * TILE:(r + 1) * TILE, :]
        xn = x * lax.rsqrt(jnp.mean(x * x, axis=-1, keepdims=True) + EPS) * gnorm_ref[...]
        h_ref[r] = (xn * (1.0 + mod_ref[0, 1:2, :]) + mod_ref[0, 0:1, :]).astype(BF16)

    def project(r, group):
        lo, hi = col_groups[group]
        for c0 in range(lo, hi, D_ATTN):
            proj_ref[r * TILE:(r + 1) * TILE, c0:c0 + D_ATTN] = jnp.dot(
                h_ref[r], w_refs[c0 // D_ATTN][...], preferred_element_type=F32)

    posts = [_block_posts(n_blocks, r, step * TILES_PER_STEP + r, pos_ref, freq_ref, wconv_ref,
                          gconv_ref, q_ref, k_ref, v_ref, gz_ref, yc_ref, kbd_ref, carry_ref,
                          cos_ref, sin_ref, proj_ref)
             for r in range(TILES_PER_STEP)]
    stages = [(r, group) for r in range(TILES_PER_STEP) for group in range(len(col_groups))]
    norm(0)
    project(*stages[0])
    for s, (r, group) in enumerate(stages):
        if s + 1 < len(stages):
            if stages[s + 1][1] == 0:
                norm(stages[s + 1][0])
            project(*stages[s + 1])
        posts[r][group]()


def _block_posts(n_blocks, r, i, pos_ref, freq_ref, wconv_ref, gconv_ref,
                 q_ref, k_ref, v_ref, gz_ref, yc_ref, kbd_ref, carry_ref, cos_ref, sin_ref,
                 proj_ref):
    rows = slice(r * TILE, (r + 1) * TILE)

    def proj(c0, c1):
        return proj_ref[rows, c0:c1]

    def post_qk():
        lane_c = lax.broadcasted_iota(jnp.int32, (TILE // 4, LANES), 1)
        ang = pos_ref[0, r * TILE // 4:(r + 1) * TILE // 4, :].astype(F32) * freq_ref[...]
        sign = jnp.where((lane_c % HEAD_DIM) < HALF, -1.0, 1.0)
        for table_ref, table, scale in ((cos_ref, jnp.cos(ang), None), (sin_ref, jnp.sin(ang), sign)):
            for a in range(LANES // HALF):
                z = jnp.where(lane_c // HALF == a, table, 0.0)
                z = z + pltpu.roll(z, 2 * HALF, 1)
                z = z + pltpu.roll(z, HALF, 1)
                table_ref[r, pl.ds(a, TILE // 4, stride=LANES // HALF), :] = (
                    z if scale is None else z * scale)
        cos = cos_ref[r]
        sin = sin_ref[r]

        lane = lax.broadcasted_iota(jnp.int32, (TILE, LANES), 1)
        first_half = (lane % HEAD_DIM) < HALF

        def rope(t):
            partner = jnp.where(first_half, pltpu.roll(t, LANES - HALF, 1), pltpu.roll(t, HALF, 1))
            return t * cos + partner * sin

        n_groups = D_ATTN // LANES
        q_groups = [rope(proj(g * LANES, (g + 1) * LANES)) for g in range(n_groups)]
        k_groups = [rope(proj(D_ATTN + g * LANES, D_ATTN + (g + 1) * LANES)) for g in range(n_groups)]
        q = jnp.concatenate(q_groups, axis=1)
        k = jnp.concatenate(k_groups, axis=1)

        nt = (((1,), (1,)), ((), ()))
        scores = lax.dot_general(kbd_ref[...].astype(BF16), q.astype(BF16), nt,
                                 preferred_element_type=F32)
        g3 = scores.reshape(N_HEADS, n_blocks, TILE)
        blk = lax.broadcasted_iota(jnp.int32, g3.shape, 1)
        past = blk < i
        g3 = jnp.where(past, g3, NEG)
        picked = jnp.zeros(g3.shape, jnp.bool_)
        for _ in range(min(MOBA_TOPK, n_blocks - 1)):
            top = jnp.max(g3, axis=1, keepdims=True)
            first = jnp.min(jnp.where(g3 == top, blk, n_blocks), axis=1, keepdims=True)
            hit = blk == first
            picked = jnp.logical_or(picked, hit)
            g3 = jnp.where(hit, -jnp.inf, g3)
        keep = jnp.logical_or(jnp.logical_and(picked, past), blk == i)
        bias = jnp.where(keep, 0.0, NEG)

        k_mean = jnp.mean(k, axis=0, keepdims=True)
        lane_w = lax.broadcasted_iota(jnp.int32, (1, D_ATTN), 1)
        for hd in range(N_HEADS):
            own = (lane_w // HEAD_DIM) == hd
            kbd_ref[pl.ds(hd * n_blocks + i, 1), :] = jnp.where(own, k_mean, 0.0)

        q_t = (q * (HEAD_DIM ** -0.5 * LOG2E)).T
        pad = jnp.zeros((HEAD_DIM - n_blocks, TILE), F32)
        parts = []
        for hd in range(N_HEADS):
            dims = q_t[hd * HEAD_DIM:(hd + 1) * HEAD_DIM]
            parts += [dims, bias[hd], pad] if hd % 2 == 0 else [bias[hd], pad, dims]
        q_ref[0, :, rows] = jnp.concatenate(parts, axis=0).astype(BF16)

        low = lane < HEAD_DIM
        onehot_even = jnp.where(lane == HEAD_DIM + i, 1.0, 0.0)
        onehot_odd = jnp.where(lane == i, 1.0, 0.0)
        for g in range(n_groups):
            even = slice(2 * g * LANES, (2 * g + 1) * LANES)
            odd = slice((2 * g + 1) * LANES, (2 * g + 2) * LANES)
            k_ref[0, rows, even] = jnp.where(low, k_groups[g], onehot_even).astype(BF16)
            k_ref[0, rows, odd] = jnp.where(low, onehot_odd, k_groups[g]).astype(BF16)

    def post_v():
        v_t = proj(2 * D_ATTN, 3 * D_ATTN).T
        ones = jnp.ones((V_ROWS - HEAD_DIM, TILE), F32)
        parts = []
        for hd in range(N_HEADS):
            parts += [v_t[hd * HEAD_DIM:(hd + 1) * HEAD_DIM], ones]
        v_ref[0, :, rows] = jnp.concatenate(parts, axis=0).astype(BF16)

    def post_gz():
        gz_ref[0, rows, :] = _silu(proj(3 * D_ATTN, 4 * D_ATTN)).astype(BF16)

    def post_conv():
        c0 = 4 * D_ATTN
        b_g = proj(c0, c0 + D_CONV)
        u = proj(c0 + D_CONV, c0 + 2 * D_CONV) * proj(c0 + 2 * D_CONV, c0 + 3 * D_CONV)
        z_c = proj(c0 + 3 * D_CONV, c0 + 4 * D_CONV)
        row = lax.broadcasted_iota(jnp.int32, u.shape, 0)
        prev1 = carry_ref[7:8, :]
        prev2 = carry_ref[6:7, :]
        u1 = jnp.where(row == 0, prev1, pltpu.roll(u, 1, 0))
        u2 = jnp.where(row == 0, prev2, jnp.where(row == 1, prev1, pltpu.roll(u, 2, 0)))
        carry_ref[...] = u[TILE - 8:TILE, :]
        y = b_g * (wconv_ref[2:3, :] * u + wconv_ref[1:2, :] * u1 + wconv_ref[0:1, :] * u2)
        y = y * lax.rsqrt(jnp.mean(y * y, axis=-1, keepdims=True) + EPS) * gconv_ref[...]
        yc_ref[0, rows, :] = (y * _silu(z_c)).astype(BF16)

    return [post_qk, post_v, post_gz, post_conv]


def _in_proj(x, mod3, g_norm, positions, inv_freq, w_hi, w_conv, g_conv_out):
    bsz, seq, _ = x.shape
    n_blocks = seq // TILE
    step_rows = TILES_PER_STEP * TILE
    wide = N_HEADS * LANES
    row_spec = lambda width: pl.BlockSpec((1, step_rows, width), lambda b, i: (b, i, 0))
    col_spec = lambda height: pl.BlockSpec((1, height, step_rows), lambda b, i: (b, 0, i))
    const = lambda shape: pl.BlockSpec(shape, lambda b, i: (0,) * len(shape))
    kbd_rows = N_HEADS * n_blocks
    n_groups = w_hi.shape[1] // D_ATTN
    per_row = LANES // HALF
    pos_compact = jnp.repeat(positions.reshape(bsz, seq // per_row, per_row), HALF, axis=2)
    return pl.pallas_call(
        functools.partial(_in_proj_kernel, n_blocks, n_groups),
        grid=(bsz, n_blocks // TILES_PER_STEP),
        in_specs=[row_spec(D_MODEL),
                  pl.BlockSpec((1, 3, D_MODEL), lambda b, i: (b, 0, 0)),
                  const((1, D_MODEL)),
                  pl.BlockSpec((1, step_rows // per_row, LANES), lambda b, i: (b, i, 0)),
                  const((1, LANES)),
                  *[pl.BlockSpec((D_MODEL, D_ATTN), lambda b, i, g=g: (0, g))
                    for g in range(n_groups)],
                  const((CONV_WIDTH, D_CONV)),
                  const((1, D_CONV))],
        out_specs=[col_spec(wide), row_spec(wide), col_spec(N_HEADS * V_ROWS),
                   row_spec(D_ATTN), row_spec(D_CONV)],
        out_shape=[jax.ShapeDtypeStruct((bsz, wide, seq), BF16),
                   jax.ShapeDtypeStruct((bsz, seq, wide), BF16),
                   jax.ShapeDtypeStruct((bsz, N_HEADS * V_ROWS, seq), BF16),
                   jax.ShapeDtypeStruct((bsz, seq, D_ATTN), BF16),
                   jax.ShapeDtypeStruct((bsz, seq, D_CONV), BF16)],
        scratch_shapes=[pltpu.VMEM((kbd_rows, D_ATTN), F32),
                        pltpu.VMEM((8, D_CONV), F32),
                        pltpu.VMEM((TILES_PER_STEP, TILE, LANES), F32),
                        pltpu.VMEM((TILES_PER_STEP, TILE, LANES), F32),
                        pltpu.VMEM((step_rows, w_hi.shape[1]), F32),
                        pltpu.VMEM((TILES_PER_STEP, TILE, D_MODEL), BF16)],
        compiler_params=pltpu.CompilerParams(dimension_semantics=("arbitrary", "arbitrary"),
                                             vmem_limit_bytes=VMEM_LIMIT),
        name="in_proj",
    )(x, mod3, g_norm.reshape(1, D_MODEL), pos_compact, inv_freq,
      *([w_hi] * n_groups), w_conv, g_conv_out.reshape(1, D_CONV))


def _moba_kernel(n_chunks, tile_ref, chunk_ref, q_ref, k_ref, v_ref, o_ref, s_ref, smax_ref,
                 m_ref, acc_ref):
    n_past = n_chunks * (n_chunks - 1) // 2

    def own_item(t):
        return t, t

    def past_item(j):
        return tile_ref[j], chunk_ref[j]

    def own_keys(half):
        return (half + 1) * MXU_COLS

    def issue_unit(buf, tile, chunk, hd, half, own):
        n_keys = own_keys(half) if own else CHUNK
        start = pl.multiple_of(chunk * CHUNK, CHUNK)
        q_start = pl.multiple_of(tile * CHUNK + half * MXU_COLS, MXU_COLS)
        rows = slice(hd * LANES, (hd + 1) * LANES)
        cols = slice(half * MXU_COLS, (half + 1) * MXU_COLS)
        kj = k_ref[0, pl.ds(start, n_keys), rows]
        s = jnp.dot(kj, q_ref[0, rows, pl.ds(q_start, MXU_COLS)],
                    preferred_element_type=F32)
        s_ref[buf, hd, :n_keys, cols] = s.astype(BF16)
        if not own:
            smax_ref[buf, hd, :, cols] = jnp.max(s, axis=0, keepdims=True)

    def absorb_unit(buf, tile, chunk, hd, half, own):
        n_keys = own_keys(half) if own else CHUNK
        start = pl.multiple_of(chunk * CHUNK, CHUNK)
        cols = slice(half * MXU_COLS, (half + 1) * MXU_COLS)
        m = m_ref[tile, hd, :, cols]
        if own:
            diag = s_ref[buf, hd, n_keys - MXU_COLS:n_keys, cols].astype(F32)
            key = lax.broadcasted_iota(jnp.int32, diag.shape, 0)
            qry = lax.broadcasted_iota(jnp.int32, diag.shape, 1)
            diag = jnp.where(key <= qry, diag, NEG)
            blocks = [s_ref[buf, hd, b * MXU_COLS:(b + 1) * MXU_COLS, cols] for b in range(half)]
            blocks.append(diag.astype(BF16))
            smax = jnp.max(diag, axis=0, keepdims=True)
            for block in blocks[:-1]:
                smax = jnp.maximum(smax, jnp.max(block, axis=0, keepdims=True).astype(F32))
        else:
            blocks = [s_ref[buf, hd, :, cols]]
            smax = smax_ref[buf, hd, :, cols]
        m_new = jnp.maximum(m, smax).astype(BF16)
        p = [jnp.exp2(block - m_new) for block in blocks]
        p = p[0] if len(p) == 1 else jnp.concatenate(p, axis=0)
        m_new = m_new.astype(F32)
        vj = v_ref[0, hd * V_ROWS:(hd + 1) * V_ROWS, pl.ds(start, n_keys)]
        acc_ref[tile, hd, :, cols] = (jnp.exp2(m - m_new) * acc_ref[tile, hd, :, cols]
                                      + jnp.dot(vj, p, preferred_element_type=F32))
        m_ref[tile, hd, :, cols] = m_new

    def step(issue=None, absorb=None, issue_own=False, absorb_own=False):
        for hd in range(2):
            for half in range(CHUNK // MXU_COLS):
                if issue is not None:
                    issue_unit(*issue, hd, half, issue_own)
                if absorb is not None:
                    absorb_unit(*absorb, hd, half, absorb_own)

    def run_pairs(item, n_pairs, pairs_per_trip, own):
        def pair(u):
            step(issue=(1, *item(2 * u + 1)), absorb=(0, *item(2 * u)),
                 issue_own=own, absorb_own=own)
            step(issue=(0, *item(2 * u + 2)), absorb=(1, *item(2 * u + 1)),
                 issue_own=own, absorb_own=own)

        def trip(t, carry):
            for u in range(pairs_per_trip):
                pair(pairs_per_trip * t + u)
            return carry

        trips = n_pairs // pairs_per_trip
        lax.fori_loop(0, trips, trip, 0)
        for u in range(trips * pairs_per_trip, n_pairs):
            pair(u)

    m_ref[...] = jnp.full(m_ref.shape, -jnp.inf, F32)
    acc_ref[...] = jnp.zeros(acc_ref.shape, F32)
    assert n_chunks % 2 == 0 and n_past % 2 == 0
    step(issue=(0, *own_item(0)), issue_own=True)
    run_pairs(own_item, (n_chunks - 2) // 2, OWN_PAIRS_PER_TRIP, own=True)
    step(issue=(1, *own_item(n_chunks - 1)), absorb=(0, *own_item(n_chunks - 2)),
         issue_own=True, absorb_own=True)
    step(issue=(0, *past_item(0)), absorb=(1, *own_item(n_chunks - 1)), absorb_own=True)
    run_pairs(past_item, (n_past - 2) // 2, PAST_PAIRS_PER_TRIP, own=False)
    step(issue=(1, *past_item(n_past - 1)), absorb=(0, *past_item(n_past - 2)))
    step(absorb=(1, *past_item(n_past - 1)))

    def finish(tile, carry):
        outs = [acc_ref[tile, hd, :HEAD_DIM] / acc_ref[tile, hd, HEAD_DIM:HEAD_DIM + 1]
                for hd in range(2)]
        start = pl.multiple_of(tile * CHUNK, CHUNK)
        o_ref[0, pl.ds(start, CHUNK), :] = jnp.concatenate(outs, axis=0).T.astype(o_ref.dtype)
        return carry

    lax.fori_loop(0, n_chunks, finish, 0)


def _moba(q_aug, k_aug, v_aug):
    bsz, seq, wide = k_aug.shape
    pairs = wide // (2 * LANES)
    n_chunks = seq // CHUNK
    past = [(t, c) for t in range(n_chunks) for c in range(t)]
    item_tile = jnp.asarray([t for t, _ in past], jnp.int32)
    item_chunk = jnp.asarray([c for _, c in past], jnp.int32)
    return pl.pallas_call(
        functools.partial(_moba_kernel, n_chunks),
        grid_spec=pltpu.PrefetchScalarGridSpec(
            num_scalar_prefetch=2,
            grid=(bsz, pairs),
            in_specs=[pl.BlockSpec((1, 2 * LANES, seq), lambda b, p, *_: (b, p, 0)),
                      pl.BlockSpec((1, seq, 2 * LANES), lambda b, p, *_: (b, 0, p)),
                      pl.BlockSpec((1, 2 * V_ROWS, seq), lambda b, p, *_: (b, p, 0))],
            out_specs=pl.BlockSpec((1, seq, LANES), lambda b, p, *_: (b, 0, p)),
            scratch_shapes=[pltpu.VMEM((2, 2, CHUNK, CHUNK), BF16),
                            pltpu.VMEM((2, 2, 1, CHUNK), F32),
                            pltpu.VMEM((n_chunks, 2, 1, CHUNK), F32),
                            pltpu.VMEM((n_chunks, 2, V_ROWS, CHUNK), F32)]),
        out_shape=jax.ShapeDtypeStruct((bsz, seq, pairs * LANES), BF16),
        compiler_params=pltpu.CompilerParams(
            dimension_semantics=("arbitrary", "arbitrary"),
            vmem_limit_bytes=VMEM_LIMIT),
        name="moba",
    )(item_tile, item_chunk, q_aug, k_aug, v_aug)


def _out_proj_kernel(ya_ref, gz_ref, yc_ref, x_ref, mod_ref, gattn_ref, wout_ref, gfin_ref, o_ref):
    ya = ya_ref[0].astype(F32)
    yn = ya * lax.rsqrt(jnp.mean(ya * ya, axis=-1, keepdims=True) + EPS) * gattn_ref[...]
    yn = (yn * gz_ref[0].astype(F32)).astype(BF16)
    y = (jnp.dot(yn, wout_ref[0:D_ATTN, :], preferred_element_type=F32)
         + jnp.dot(yc_ref[0], wout_ref[D_ATTN:, :], preferred_element_type=F32))
    xo = x_ref[0] + mod_ref[0, 2:3, :] * y
    o_ref[0] = xo * lax.rsqrt(jnp.mean(xo * xo, axis=-1, keepdims=True) + EPS) * gfin_ref[...]


def _out_proj(y_attn, gz, yc, x, mod3, g_attn_out, w_out, g_final):
    bsz, seq, _ = x.shape
    row_spec = lambda width: pl.BlockSpec((1, OUT_TILE, width), lambda b, i: (b, i, 0))
    const = lambda shape: pl.BlockSpec(shape, lambda b, i: (0,) * len(shape))
    return pl.pallas_call(
        _out_proj_kernel,
        grid=(bsz, seq // OUT_TILE),
        in_specs=[row_spec(D_ATTN), row_spec(D_ATTN), row_spec(D_CONV), row_spec(D_MODEL),
                  pl.BlockSpec((1, 3, D_MODEL), lambda b, i: (b, 0, 0)),
                  const((1, D_ATTN)), const(w_out.shape), const((1, D_MODEL))],
        out_specs=row_spec(D_MODEL),
        out_shape=jax.ShapeDtypeStruct((bsz, seq, D_MODEL), F32),
        compiler_params=pltpu.CompilerParams(dimension_semantics=("arbitrary", "arbitrary"),
                                             vmem_limit_bytes=VMEM_LIMIT),
        name="out_proj",
    )(y_attn, gz, yc, x, mod3, g_attn_out.reshape(1, D_ATTN), w_out, g_final.reshape(1, D_MODEL))


def kernel(x, c, positions, w_ada, b_ada, g_norm, w_in, w_conv, g_attn_out, g_conv_out, w_out, g_final):
    bsz, seq, _ = x.shape
    assert seq % (TILES_PER_STEP * TILE) == 0 and seq % OUT_TILE == 0 and seq // TILE <= HEAD_DIM // 2
    mod3 = _adaln(c, w_ada, b_ada).reshape(bsz, 3, D_MODEL)

    inv_freq = ROPE_THETA ** (-jnp.arange(HALF, dtype=F32) / HALF)
    inv_freq = jnp.tile(inv_freq, LANES // HALF).reshape(1, LANES)

    q_aug, k_aug, v_aug, gz, yc = _in_proj(x, mod3, g_norm, positions, inv_freq,
                                           w_in.astype(BF16), w_conv, g_conv_out)
    y_attn = _moba(q_aug, k_aug, v_aug)
    return _out_proj(y_attn, gz, yc, x, mod3, g_attn_out, w_out.astype(BF16), g_final)
```

```python
import functools

import jax
import jax.numpy as jnp
from jax import lax
from jax.experimental import pallas as pl
from jax.experimental.pallas import tpu as pltpu

D_MODEL = 1024
D_ATTN = 512
D_CONV = 512
N_HEADS = 8
HEAD_DIM = 64
HALF = HEAD_DIM // 2
CONV_WIDTH = 3
MOBA_BLOCK = 256
MOBA_TOPK = 3
ROPE_THETA = 10000.0
EPS = 1e-6
NEG = -1e30
LOG2E = 1.4426950408889634

LANES = 128
V_ROWS = HEAD_DIM + 16
TILE = MOBA_BLOCK
CHUNK = 2 * TILE
TILES_PER_STEP = 2
MXU_COLS = 256
OWN_PAIRS_PER_TRIP = 7
PAST_PAIRS_PER_TRIP = 5
OUT_TILE = 1024
VMEM_LIMIT = 56 * 1024 * 1024

F32 = jnp.float32
BF16 = jnp.bfloat16


def _silu(z):
    return z * (1.0 / (1.0 + jnp.exp(-z)))


def _split_bf16(a):
    hi = a.astype(BF16)
    lo = (a - hi.astype(F32)).astype(BF16)
    return hi, lo


def _adaln_kernel(c_ref, w_ref, b_ref, o_ref):
    act = _silu(c_ref[...])
    rows = [jnp.sum(act[:, b:b + 1] * w_ref[...], axis=0, keepdims=True)
            for b in range(act.shape[1])]
    o_ref[...] = jnp.concatenate(rows, axis=0) + b_ref[...]


def _adaln(c, w_ada, b_ada):
    bsz = c.shape[0]
    n = w_ada.shape[1]
    bn = D_MODEL
    return pl.pallas_call(
        _adaln_kernel,
        grid=(n // bn,),
        in_specs=[pl.BlockSpec((D_MODEL, bsz), lambda j: (0, 0)),
                  pl.BlockSpec((D_MODEL, bn), lambda j: (0, j)),
                  pl.BlockSpec((1, bn), lambda j: (0, j))],
        out_specs=pl.BlockSpec((bsz, bn), lambda j: (0, j)),
        out_shape=jax.ShapeDtypeStruct((bsz, n), F32),
        compiler_params=pltpu.CompilerParams(dimension_semantics=("arbitrary",),
                                             vmem_limit_bytes=VMEM_LIMIT),
        name="adaln",
    )(c.T, w_ada, b_ada.reshape(1, n))


def _in_proj_kernel(n_blocks, n_groups, x_ref, mod_ref, gnorm_ref, pos_ref, freq_ref, *refs):
    w_refs = refs[:n_groups]
    (wconv_ref, gconv_ref, q_ref, k_ref, v_ref, gz_ref, yc_ref,
     kbd_ref, carry_ref, cos_ref, sin_ref, proj_ref, h_ref, wbf_ref) = refs[n_groups:]
    step = pl.program_id(1)

    @pl.when(step == 0)
    def _():
        kbd_ref[...] = jnp.zeros_like(kbd_ref)
        carry_ref[...] = jnp.zeros_like(carry_ref)

    @pl.when(jnp.logical_and(pl.program_id(0) == 0, step == 0))
    def _():
        for g in range(n_groups):
            wbf_ref[g] = w_refs[g][...].astype(BF16)

    n_cols = n_groups * D_ATTN
    col_groups = [(0, 2 * D_ATTN), (2 * D_ATTN, 3 * D_ATTN), (3 * D_ATTN, 4 * D_ATTN),
                  (4 * D_ATTN, n_cols)]

    def norm(r):
        x = x_ref[0, r * TILE:(r + 1) * TILE, :]
        xn = x * lax.rsqrt(jnp.mean(x * x, axis=-1, keepdims=True) + EPS) * gnorm_ref[...]
        h_ref[r] = (xn * (1.0 + mod_ref[0, 1:2, :]) + mod_ref[0, 0:1, :]).astype(BF16)

    def project(r, group):
        lo, hi = col_groups[group]
        for c0 in range(lo, hi, D_ATTN):
            proj_ref[r * TILE:(r + 1) * TILE, c0:c0 + D_ATTN] = jnp.dot(
                h_ref[r], wbf_ref[c0 // D_ATTN], preferred_element_type=F32)

    posts = [_block_posts(n_blocks, r, step * TILES_PER_STEP + r, pos_ref, freq_ref, wconv_ref,
                          gconv_ref, q_ref, k_ref, v_ref, gz_ref, yc_ref, kbd_ref, carry_ref,
                          cos_ref, sin_ref, proj_ref)
             for r in range(TILES_PER_STEP)]
    stages = [(r, group) for r in range(TILES_PER_STEP) for group in range(len(col_groups))]
    norm(0)
    project(*stages[0])
    for s, (r, group) in enumerate(stages):
        if s + 1 < len(stages):
            if stages[s + 1][1] == 0:
                norm(stages[s + 1][0])
            project(*stages[s + 1])
        posts[r][group]()


def _block_posts(n_blocks, r, i, pos_ref, freq_ref, wconv_ref, gconv_ref,
                 q_ref, k_ref, v_ref, gz_ref, yc_ref, kbd_ref, carry_ref, cos_ref, sin_ref,
                 proj_ref):
    rows = slice(r * TILE, (r + 1) * TILE)

    def proj(c0, c1):
        return proj_ref[rows, c0:c1]

    def post_qk():
        lane_c = lax.broadcasted_iota(jnp.int32, (TILE // 4, LANES), 1)
        ang = pos_ref[0, r * TILE // 4:(r + 1) * TILE // 4, :].astype(F32) * freq_ref[...]
        sign = jnp.where((lane_c % HEAD_DIM) < HALF, -1.0, 1.0)
        for table_ref, table, scale in ((cos_ref, jnp.cos(ang), None), (sin_ref, jnp.sin(ang), sign)):
            for a in range(LANES // HALF):
                z = jnp.where(lane_c // HALF == a, table, 0.0)
                z = z + pltpu.roll(z, 2 * HALF, 1)
                z = z + pltpu.roll(z, HALF, 1)
                table_ref[r, pl.ds(a, TILE // 4, stride=LANES // HALF), :] = (
                    z if scale is None else z * scale)
        cos = cos_ref[r]
        sin = sin_ref[r]

        lane = lax.broadcasted_iota(jnp.int32, (TILE, LANES), 1)
        first_half = (lane % HEAD_DIM) < HALF

        def rope(t):
            partner = jnp.where(first_half, pltpu.roll(t, LANES - HALF, 1), pltpu.roll(t, HALF, 1))
            return t * cos + partner * sin

        n_groups = D_ATTN // LANES
        q_groups = [rope(proj(g * LANES, (g + 1) * LANES)) for g in range(n_groups)]
        k_groups = [rope(proj(D_ATTN + g * LANES, D_ATTN + (g + 1) * LANES)) for g in range(n_groups)]
        q = jnp.concatenate(q_groups, axis=1)
        k = jnp.concatenate(k_groups, axis=1)

        nt = (((1,), (1,)), ((), ()))
        scores = lax.dot_general(kbd_ref[...].astype(BF16), q.astype(BF16), nt,
                                 preferred_element_type=F32)
        g3 = scores.reshape(N_HEADS, n_blocks, TILE)
        blk = lax.broadcasted_iota(jnp.int32, g3.shape, 1)
        past = blk < i
        g3 = jnp.where(past, g3, NEG)
        picked = jnp.zeros(g3.shape, jnp.bool_)
        for _ in range(min(MOBA_TOPK, n_blocks - 1)):
            top = jnp.max(g3, axis=1, keepdims=True)
            first = jnp.min(jnp.where(g3 == top, blk, n_blocks), axis=1, keepdims=True)
            hit = blk == first
            picked = jnp.logical_or(picked, hit)
            g3 = jnp.where(hit, -jnp.inf, g3)
        keep = jnp.logical_or(jnp.logical_and(picked, past), blk == i)
        bias = jnp.where(keep, 0.0, NEG)

        k_mean = jnp.mean(k, axis=0, keepdims=True)
        lane_w = lax.broadcasted_iota(jnp.int32, (1, D_ATTN), 1)
        for hd in range(N_HEADS):
            own = (lane_w // HEAD_DIM) == hd
            kbd_ref[pl.ds(hd * n_blocks + i, 1), :] = jnp.where(own, k_mean, 0.0)

        q_t = (q * (HEAD_DIM ** -0.5 * LOG2E)).T
        pad = jnp.zeros((HEAD_DIM - n_blocks, TILE), F32)
        parts = []
        for hd in range(N_HEADS):
            dims = q_t[hd * HEAD_DIM:(hd + 1) * HEAD_DIM]
            parts += [dims, bias[hd], pad] if hd % 2 == 0 else [bias[hd], pad, dims]
        q_ref[0, :, rows] = jnp.concatenate(parts, axis=0).astype(BF16)

        low = lane < HEAD_DIM
        onehot_even = jnp.where(lane == HEAD_DIM + i, 1.0, 0.0)
        onehot_odd = jnp.where(lane == i, 1.0, 0.0)
        for g in range(n_groups):
            even = slice(2 * g * LANES, (2 * g + 1) * LANES)
            odd = slice((2 * g + 1) * LANES, (2 * g + 2) * LANES)
            k_ref[0, rows, even] = jnp.where(low, k_groups[g], onehot_even).astype(BF16)
            k_ref[0, rows, odd] = jnp.where(low, onehot_odd, k_groups[g]).astype(BF16)

    def post_v():
        v_t = proj(2 * D_ATTN, 3 * D_ATTN).T
        ones = jnp.ones((V_ROWS - HEAD_DIM, TILE), F32)
        parts = []
        for hd in range(N_HEADS):
            parts += [v_t[hd * HEAD_DIM:(hd + 1) * HEAD_DIM], ones]
        v_ref[0, :, rows] = jnp.concatenate(parts, axis=0).astype(BF16)

    def post_gz():
        gz_ref[0, rows, :] = _silu(proj(3 * D_ATTN, 4 * D_ATTN)).astype(BF16)

    def post_conv():
        c0 = 4 * D_ATTN
        b_g = proj(c0, c0 + D_CONV)
        u = proj(c0 + D_CONV, c0 + 2 * D_CONV) * proj(c0 + 2 * D_CONV, c0 + 3 * D_CONV)
        z_c = proj(c0 + 3 * D_CONV, c0 + 4 * D_CONV)
        row = lax.broadcasted_iota(jnp.int32, u.shape, 0)
        prev1 = carry_ref[7:8, :]
        prev2 = carry_ref[6:7, :]
        u1 = jnp.where(row == 0, prev1, pltpu.roll(u, 1, 0))
        u2 = jnp.where(row == 0, prev2, jnp.where(row == 1, prev1, pltpu.roll(u, 2, 0)))
        carry_ref[...] = u[TILE - 8:TILE, :]
        y = b_g * (wconv_ref[2:3, :] * u + wconv_ref[1:2, :] * u1 + wconv_ref[0:1, :] * u2)
        y = y * lax.rsqrt(jnp.mean(y * y, axis=-1, keepdims=True) + EPS) * gconv_ref[...]
        yc_ref[0, rows, :] = (y * _silu(z_c)).astype(BF16)

    return [post_qk, post_v, post_gz, post_conv]


def _in_proj(x, mod3, g_norm, positions, inv_freq, w_hi, w_conv, g_conv_out):
    bsz, seq, _ = x.shape
    n_blocks = seq // TILE
    step_rows = TILES_PER_STEP * TILE
    wide = N_HEADS * LANES
    row_spec = lambda width: pl.BlockSpec((1, step_rows, width), lambda b, i: (b, i, 0))
    col_spec = lambda height: pl.BlockSpec((1, height, step_rows), lambda b, i: (b, 0, i))
    const = lambda shape: pl.BlockSpec(shape, lambda b, i: (0,) * len(shape))
    kbd_rows = N_HEADS * n_blocks
    n_groups = w_hi.shape[1] // D_ATTN
    per_row = LANES // HALF
    pos_compact = jnp.repeat(positions.reshape(bsz, seq // per_row, per_row), HALF, axis=2)
    return pl.pallas_call(
        functools.partial(_in_proj_kernel, n_blocks, n_groups),
        grid=(bsz, n_blocks // TILES_PER_STEP),
        in_specs=[row_spec(D_MODEL),
                  pl.BlockSpec((1, 3, D_MODEL), lambda b, i: (b, 0, 0)),
                  const((1, D_MODEL)),
                  pl.BlockSpec((1, step_rows // per_row, LANES), lambda b, i: (b, i, 0)),
                  const((1, LANES)),
                  *[pl.BlockSpec((D_MODEL, D_ATTN), lambda b, i, g=g: (0, g),
                                 pipeline_mode=pl.Buffered(1))
                    for g in range(n_groups)],
                  const((CONV_WIDTH, D_CONV)),
                  const((1, D_CONV))],
        out_specs=[col_spec(wide), row_spec(wide), col_spec(N_HEADS * V_ROWS),
                   row_spec(D_ATTN), row_spec(D_CONV)],
        out_shape=[jax.ShapeDtypeStruct((bsz, wide, seq), BF16),
                   jax.ShapeDtypeStruct((bsz, seq, wide), BF16),
                   jax.ShapeDtypeStruct((bsz, N_HEADS * V_ROWS, seq), BF16),
                   jax.ShapeDtypeStruct((bsz, seq, D_ATTN), BF16),
                   jax.ShapeDtypeStruct((bsz, seq, D_CONV), BF16)],
        scratch_shapes=[pltpu.VMEM((kbd_rows, D_ATTN), F32),
                        pltpu.VMEM((8, D_CONV), F32),
                        pltpu.VMEM((TILES_PER_STEP, TILE, LANES), F32),
                        pltpu.VMEM((TILES_PER_STEP, TILE, LANES), F32),
                        pltpu.VMEM((step_rows, w_hi.shape[1]), F32),
                        pltpu.VMEM((TILES_PER_STEP, TILE, D_MODEL), BF16),
                        pltpu.VMEM((n_groups, D_MODEL, D_ATTN), BF16)],
        compiler_params=pltpu.CompilerParams(dimension_semantics=("arbitrary", "arbitrary"),
                                             vmem_limit_bytes=VMEM_LIMIT),
        name="in_proj",
    )(x, mod3, g_norm.reshape(1, D_MODEL), pos_compact, inv_freq,
      *([w_hi] * n_groups), w_conv, g_conv_out.reshape(1, D_CONV))


def _moba_kernel(n_chunks, tile_ref, chunk_ref, q_ref, k_ref, v_ref, o_ref, s_ref, smax_ref,
                 m_ref, acc_ref):
    n_past = n_chunks * (n_chunks - 1) // 2

    def own_item(t):
        return t, t

    def past_item(j):
        return tile_ref[j], chunk_ref[j]

    def own_keys(half):
        return (half + 1) * MXU_COLS

    def issue_unit(buf, tile, chunk, hd, half, own):
        n_keys = own_keys(half) if own else CHUNK
        start = pl.multiple_of(chunk * CHUNK, CHUNK)
        q_start = pl.multiple_of(tile * CHUNK + half * MXU_COLS, MXU_COLS)
        rows = slice(hd * LANES, (hd + 1) * LANES)
        cols = slice(half * MXU_COLS, (half + 1) * MXU_COLS)
        kj = k_ref[0, pl.ds(start, n_keys), rows]
        s = jnp.dot(kj, q_ref[0, rows, pl.ds(q_start, MXU_COLS)],
                    preferred_element_type=F32)
        s_ref[buf, hd, :n_keys, cols] = s.astype(BF16)
        if not own:
            smax_ref[buf, hd, :, cols] = jnp.max(s, axis=0, keepdims=True)

    def absorb_unit(buf, tile, chunk, hd, half, own):
        n_keys = own_keys(half) if own else CHUNK
        start = pl.multiple_of(chunk * CHUNK, CHUNK)
        cols = slice(half * MXU_COLS, (half + 1) * MXU_COLS)
        m = m_ref[tile, hd, :, cols]
        if own:
            diag = s_ref[buf, hd, n_keys - MXU_COLS:n_keys, cols].astype(F32)
            key = lax.broadcasted_iota(jnp.int32, diag.shape, 0)
            qry = lax.broadcasted_iota(jnp.int32, diag.shape, 1)
            diag = jnp.where(key <= qry, diag, NEG)
            blocks = [s_ref[buf, hd, b * MXU_COLS:(b + 1) * MXU_COLS, cols] for b in range(half)]
            blocks.append(diag.astype(BF16))
            smax = jnp.max(diag, axis=0, keepdims=True)
            for block in blocks[:-1]:
                smax = jnp.maximum(smax, jnp.max(block, axis=0, keepdims=True).astype(F32))
        else:
            blocks = [s_ref[buf, hd, :, cols]]
            smax = smax_ref[buf, hd, :, cols]
        m_new = jnp.maximum(m, smax).astype(BF16)
        p = [jnp.exp2(block - m_new) for block in blocks]
        p = p[0] if len(p) == 1 else jnp.concatenate(p, axis=0)
        m_new = m_new.astype(F32)
        vj = v_ref[0, hd * V_ROWS:(hd + 1) * V_ROWS, pl.ds(start, n_keys)]
        acc_ref[tile, hd, :, cols] = (jnp.exp2(m - m_new) * acc_ref[tile, hd, :, cols]
                                      + jnp.dot(vj, p, preferred_element_type=F32))
        m_ref[tile, hd, :, cols] = m_new

    def step(issue=None, absorb=None, issue_own=False, absorb_own=False):
        for hd in range(2):
            for half in range(CHUNK // MXU_COLS):
                if issue is not None:
                    issue_unit(*issue, hd, half, issue_own)
                if absorb is not None:
                    absorb_unit(*absorb, hd, half, absorb_own)

    def run_pairs(item, n_pairs, pairs_per_trip, own):
        def pair(u):
            step(issue=(1, *item(2 * u + 1)), absorb=(0, *item(2 * u)),
                 issue_own=own, absorb_own=own)
            step(issue=(0, *item(2 * u + 2)), absorb=(1, *item(2 * u + 1)),
                 issue_own=own, absorb_own=own)

        def trip(t, carry):
            for u in range(pairs_per_trip):
                pair(pairs_per_trip * t + u)
            return carry

        trips = n_pairs // pairs_per_trip
        lax.fori_loop(0, trips, trip, 0)
        for u in range(trips * pairs_per_trip, n_pairs):
            pair(u)

    m_ref[...] = jnp.full(m_ref.shape, -jnp.inf, F32)
    acc_ref[...] = jnp.zeros(acc_ref.shape, F32)
    assert n_chunks % 2 == 0 and n_past % 2 == 0
    step(issue=(0, *own_item(0)), issue_own=True)
    run_pairs(own_item, (n_chunks - 2) // 2, OWN_PAIRS_PER_TRIP, own=True)
    step(issue=(1, *own_item(n_chunks - 1)), absorb=(0, *own_item(n_chunks - 2)),
         issue_own=True, absorb_own=True)
    step(issue=(0, *past_item(0)), absorb=(1, *own_item(n_chunks - 1)), absorb_own=True)
    run_pairs(past_item, (n_past - 2) // 2, PAST_PAIRS_PER_TRIP, own=False)
    step(issue=(1, *past_item(n_past - 1)), absorb=(0, *past_item(n_past - 2)))
    step(absorb=(1, *past_item(n_past - 1)))

    def finish(tile, carry):
        outs = [acc_ref[tile, hd, :HEAD_DIM] / acc_ref[tile, hd, HEAD_DIM:HEAD_DIM + 1]
                for hd in range(2)]
        start = pl.multiple_of(tile * CHUNK, CHUNK)
        o_ref[0, pl.ds(start, CHUNK), :] = jnp.concatenate(outs, axis=0).T.astype(o_ref.dtype)
        return carry

    lax.fori_loop(0, n_chunks, finish, 0)


def _moba(q_aug, k_aug, v_aug):
    bsz, seq, wide = k_aug.shape
    pairs = wide // (2 * LANES)
    n_chunks = seq // CHUNK
    past = [(t, c) for t in range(n_chunks) for c in range(t)]
    item_tile = jnp.asarray([t for t, _ in past], jnp.int32)
    item_chunk = jnp.asarray([c for _, c in past], jnp.int32)
    return pl.pallas_call(
        functools.partial(_moba_kernel, n_chunks),
        grid_spec=pltpu.PrefetchScalarGridSpec(
            num_scalar_prefetch=2,
            grid=(bsz, pairs),
            in_specs=[pl.BlockSpec((1, 2 * LANES, seq), lambda b, p, *_: (b, p, 0)),
                      pl.BlockSpec((1, seq, 2 * LANES), lambda b, p, *_: (b, 0, p)),
                      pl.BlockSpec((1, 2 * V_ROWS, seq), lambda b, p, *_: (b, p, 0))],
            out_specs=pl.BlockSpec((1, seq, LANES), lambda b, p, *_: (b, 0, p)),
            scratch_shapes=[pltpu.VMEM((2, 2, CHUNK, CHUNK), BF16),
                            pltpu.VMEM((2, 2, 1, CHUNK), F32),
                            pltpu.VMEM((n_chunks, 2, 1, CHUNK), F32),
                            pltpu.VMEM((n_chunks, 2, V_ROWS, CHUNK), F32)]),
        out_shape=jax.ShapeDtypeStruct((bsz, seq, pairs * LANES), BF16),
        compiler_params=pltpu.CompilerParams(
            dimension_semantics=("arbitrary", "arbitrary"),
            vmem_limit_bytes=VMEM_LIMIT),
        name="moba",
    )(item_tile, item_chunk, q_aug, k_aug, v_aug)


def _out_proj_kernel(ya_ref, gz_ref, yc_ref, x_ref, mod_ref, gattn_ref, wout_ref, gfin_ref, o_ref):
    ya = ya_ref[0].astype(F32)
    yn = ya * lax.rsqrt(jnp.mean(ya * ya, axis=-1, keepdims=True) + EPS) * gattn_ref[...]
    yn = (yn * gz_ref[0].astype(F32)).astype(BF16)
    y = (jnp.dot(yn, wout_ref[0:D_ATTN, :], preferred_element_type=F32)
         + jnp.dot(yc_ref[0], wout_ref[D_ATTN:, :], preferred_element_type=F32))
    xo = x_ref[0] + mod_ref[0, 2:3, :] * y
    o_ref[0] = xo * lax.rsqrt(jnp.mean(xo * xo, axis=-1, keepdims=True) + EPS) * gfin_ref[...]


def _out_proj(y_attn, gz, yc, x, mod3, g_attn_out, w_out, g_final):
    bsz, seq, _ = x.shape
    row_spec = lambda width: pl.BlockSpec((1, OUT_TILE, width), lambda b, i: (b, i, 0))
    const = lambda shape: pl.BlockSpec(shape, lambda b, i: (0,) * len(shape))
    return pl.pallas_call(
        _out_proj_kernel,
        grid=(bsz, seq // OUT_TILE),
        in_specs=[row_spec(D_ATTN), row_spec(D_ATTN), row_spec(D_CONV), row_spec(D_MODEL),
                  pl.BlockSpec((1, 3, D_MODEL), lambda b, i: (b, 0, 0)),
                  const((1, D_ATTN)), const(w_out.shape), const((1, D_MODEL))],
        out_specs=row_spec(D_MODEL),
        out_shape=jax.ShapeDtypeStruct((bsz, seq, D_MODEL), F32),
        compiler_params=pltpu.CompilerParams(dimension_semantics=("arbitrary", "arbitrary"),
                                             vmem_limit_bytes=VMEM_LIMIT),
        name="out_proj",
    )(y_attn, gz, yc, x, mod3, g_attn_out.reshape(1, D_ATTN), w_out, g_final.reshape(1, D_MODEL))


def kernel(x, c, positions, w_ada, b_ada, g_norm, w_in, w_conv, g_attn_out, g_conv_out, w_out, g_final):
    bsz, seq, _ = x.shape
    assert seq % (TILES_PER_STEP * TILE) == 0 and seq % OUT_TILE == 0 and seq // TILE <= HEAD_DIM // 2
    mod3 = _adaln(c, w_ada, b_ada).reshape(bsz, 3, D_MODEL)

    inv_freq = ROPE_THETA ** (-jnp.arange(HALF, dtype=F32) / HALF)
    inv_freq = jnp.tile(inv_freq, LANES // HALF).reshape(1, LANES)

    q_aug, k_aug, v_aug, gz, yc = _in_proj(x, mod3, g_norm, positions, inv_freq,
                                           w_in, w_conv, g_conv_out)
    y_attn = _moba(q_aug, k_aug, v_aug)
    return _out_proj(y_attn, gz, yc, x, mod3, g_attn_out, w_out.astype(BF16), g_final)
```

```python
import functools

import jax
import jax.numpy as jnp
from jax import lax
from jax.experimental import pallas as pl
from jax.experimental.pallas import tpu as pltpu

D_MODEL = 1024
D_ATTN = 512
D_CONV = 512
N_HEADS = 8
HEAD_DIM = 64
HALF = HEAD_DIM // 2
CONV_WIDTH = 3
MOBA_BLOCK = 256
MOBA_TOPK = 3
ROPE_THETA = 10000.0
EPS = 1e-6
NEG = -1e30
LOG2E = 1.4426950408889634

LANES = 128
V_ROWS = HEAD_DIM + 16
TILE = MOBA_BLOCK
CHUNK = 2 * TILE
TILES_PER_STEP = 2
MXU_COLS = 256
OWN_PAIRS_PER_TRIP = 7
PAST_PAIRS_PER_TRIP = 5
OUT_TILE = 1024
VMEM_LIMIT = 56 * 1024 * 1024

F32 = jnp.float32
BF16 = jnp.bfloat16


def _silu(z):
    return z * (1.0 / (1.0 + jnp.exp(-z)))


def _split_bf16(a):
    hi = a.astype(BF16)
    lo = (a - hi.astype(F32)).astype(BF16)
    return hi, lo


def _adaln_kernel(c_ref, w_ref, b_ref, o_ref):
    act = _silu(c_ref[...])
    rows = [jnp.sum(act[:, b:b + 1] * w_ref[...], axis=0, keepdims=True)
            for b in range(act.shape[1])]
    o_ref[...] = jnp.concatenate(rows, axis=0) + b_ref[...]


def _adaln(c, w_ada, b_ada):
    bsz = c.shape[0]
    n = w_ada.shape[1]
    bn = D_MODEL
    return pl.pallas_call(
        _adaln_kernel,
        grid=(n // bn,),
        in_specs=[pl.BlockSpec((D_MODEL, bsz), lambda j: (0, 0)),
                  pl.BlockSpec((D_MODEL, bn), lambda j: (0, j)),
                  pl.BlockSpec((1, bn), lambda j: (0, j))],
        out_specs=pl.BlockSpec((bsz, bn), lambda j: (0, j)),
        out_shape=jax.ShapeDtypeStruct((bsz, n), F32),
        compiler_params=pltpu.CompilerParams(dimension_semantics=("arbitrary",),
                                             vmem_limit_bytes=VMEM_LIMIT),
        name="adaln",
    )(c.T, w_ada, b_ada.reshape(1, n))


def _in_proj_kernel(n_blocks, n_groups, x_ref, mod_ref, gnorm_ref, pos_ref, freq_ref, *refs):
    w_refs = refs[:n_groups]
    (wconv_ref, gconv_ref, q_ref, k_ref, v_ref, gz_ref, yc_ref,
     kbd_ref, carry_ref, cos_ref, sin_ref, proj_ref, h_ref) = refs[n_groups:]
    step = pl.program_id(1)

    @pl.when(step == 0)
    def _():
        kbd_ref[...] = jnp.zeros_like(kbd_ref)
        carry_ref[...] = jnp.zeros_like(carry_ref)

    n_cols = n_groups * D_ATTN
    col_groups = [(0, 2 * D_ATTN), (2 * D_ATTN, 3 * D_ATTN), (3 * D_ATTN, 4 * D_ATTN),
                  (4 * D_ATTN, n_cols)]

    def norm(r):
        x = x_ref[0, r * TILE:(r + 1) * TILE, :]
        xn = x * lax.rsqrt(jnp.mean(x * x, axis=-1, keepdims=True) + EPS) * gnorm_ref[...]
        h_ref[r] = (xn * (1.0 + mod_ref[0, 1:2, :]) + mod_ref[0, 0:1, :]).astype(BF16)

    def project(r, group):
        lo, hi = col_groups[group]
        for c0 in range(lo, hi, D_ATTN):
            proj_ref[r * TILE:(r + 1) * TILE, c0:c0 + D_ATTN] = jnp.dot(
                h_ref[r], w_refs[c0 // D_ATTN][...], preferred_element_type=F32)

    posts = [_block_posts(n_blocks, r, step * TILES_PER_STEP + r, pos_ref, freq_ref, wconv_ref,
                          gconv_ref, q_ref, k_ref, v_ref, gz_ref, yc_ref, kbd_ref, carry_ref,
                          cos_ref, sin_ref, proj_ref)
             for r in range(TILES_PER_STEP)]
    stages = [(r, group) for r in range(TILES_PER_STEP) for group in range(len(col_groups))]
    norm(0)
    project(*stages[0])
    for s, (r, group) in enumerate(stages):
        if s + 1 < len(stages):
            if stages[s + 1][1] == 0:
                norm(stages[s + 1][0])
            project(*stages[s + 1])
        posts[r][group]()


def _block_posts(n_blocks, r, i, pos_ref, freq_ref, wconv_ref, gconv_ref,
                 q_ref, k_ref, v_ref, gz_ref, yc_ref, kbd_ref, carry_ref, cos_ref, sin_ref,
                 proj_ref):
    rows = slice(r * TILE, (r + 1) * TILE)

    def proj(c0, c1):
        return proj_ref[rows, c0:c1]

    def post_qk():
        lane_c = lax.broadcasted_iota(jnp.int32, (TILE // 4, LANES), 1)
        ang = pos_ref[0, r * TILE // 4:(r + 1) * TILE // 4, :].astype(F32) * freq_ref[...]
        sign = jnp.where((lane_c % HEAD_DIM) < HALF, -1.0, 1.0)
        for table_ref, table, scale in ((cos_ref, jnp.cos(ang), None), (sin_ref, jnp.sin(ang), sign)):
            for a in range(LANES // HALF):
                z = jnp.where(lane_c // HALF == a, table, 0.0)
                z = z + pltpu.roll(z, 2 * HALF, 1)
                z = z + pltpu.roll(z, HALF, 1)
                table_ref[r, pl.ds(a, TILE // 4, stride=LANES // HALF), :] = (
                    z if scale is None else z * scale)
        cos = cos_ref[r]
        sin = sin_ref[r]

        lane = lax.broadcasted_iota(jnp.int32, (TILE, LANES), 1)
        first_half = (lane % HEAD_DIM) < HALF

        def rope(t):
            partner = jnp.where(first_half, pltpu.roll(t, LANES - HALF, 1), pltpu.roll(t, HALF, 1))
            return t * cos + partner * sin

        n_groups = D_ATTN // LANES
        q_groups = [rope(proj(g * LANES, (g + 1) * LANES)) for g in range(n_groups)]
        k_groups = [rope(proj(D_ATTN + g * LANES, D_ATTN + (g + 1) * LANES)) for g in range(n_groups)]
        q = jnp.concatenate(q_groups, axis=1)
        k = jnp.concatenate(k_groups, axis=1)

        nt = (((1,), (1,)), ((), ()))
        scores = lax.dot_general(kbd_ref[...].astype(BF16), q.astype(BF16), nt,
                                 preferred_element_type=F32)
        g3 = scores.reshape(N_HEADS, n_blocks, TILE)
        blk = lax.broadcasted_iota(jnp.int32, g3.shape, 1)
        past = blk < i
        g3 = jnp.where(past, g3, NEG)
        picked = jnp.zeros(g3.shape, jnp.bool_)
        for _ in range(min(MOBA_TOPK, n_blocks - 1)):
            top = jnp.max(g3, axis=1, keepdims=True)
            first = jnp.min(jnp.where(g3 == top, blk, n_blocks), axis=1, keepdims=True)
            hit = blk == first
            picked = jnp.logical_or(picked, hit)
            g3 = jnp.where(hit, -jnp.inf, g3)
        keep = jnp.logical_or(jnp.logical_and(picked, past), blk == i)
        bias = jnp.where(keep, 0.0, NEG)

        k_mean = jnp.mean(k, axis=0, keepdims=True)
        lane_w = lax.broadcasted_iota(jnp.int32, (1, D_ATTN), 1)
        for hd in range(N_HEADS):
            own = (lane_w // HEAD_DIM) == hd
            kbd_ref[pl.ds(hd * n_blocks + i, 1), :] = jnp.where(own, k_mean, 0.0)

        q_t = (q * (HEAD_DIM ** -0.5 * LOG2E)).T
        pad = jnp.zeros((HEAD_DIM - n_blocks, TILE), F32)
        parts = []
        for hd in range(N_HEADS):
            dims = q_t[hd * HEAD_DIM:(hd + 1) * HEAD_DIM]
            parts += [dims, bias[hd], pad] if hd % 2 == 0 else [bias[hd], pad, dims]
        q_ref[0, :, rows] = jnp.concatenate(parts, axis=0).astype(BF16)

        low = lane < HEAD_DIM
        onehot_even = jnp.where(lane == HEAD_DIM + i, 1.0, 0.0)
        onehot_odd = jnp.where(lane == i, 1.0, 0.0)
        for g in range(n_groups):
            even = slice(2 * g * LANES, (2 * g + 1) * LANES)
            odd = slice((2 * g + 1) * LANES, (2 * g + 2) * LANES)
            k_ref[0, rows, even] = jnp.where(low, k_groups[g], onehot_even).astype(BF16)
            k_ref[0, rows, odd] = jnp.where(low, onehot_odd, k_groups[g]).astype(BF16)

    def post_v():
        v_t = proj(2 * D_ATTN, 3 * D_ATTN).T
        ones = jnp.ones((V_ROWS - HEAD_DIM, TILE), F32)
        parts = []
        for hd in range(N_HEADS):
            parts += [v_t[hd * HEAD_DIM:(hd + 1) * HEAD_DIM], ones]
        v_ref[0, :, rows] = jnp.concatenate(parts, axis=0).astype(BF16)

    def post_gz():
        gz_ref[0, rows, :] = _silu(proj(3 * D_ATTN, 4 * D_ATTN)).astype(BF16)

    def post_conv():
        c0 = 4 * D_ATTN
        b_g = proj(c0, c0 + D_CONV)
        u = proj(c0 + D_CONV, c0 + 2 * D_CONV) * proj(c0 + 2 * D_CONV, c0 + 3 * D_CONV)
        z_c = proj(c0 + 3 * D_CONV, c0 + 4 * D_CONV)
        row = lax.broadcasted_iota(jnp.int32, u.shape, 0)
        prev1 = carry_ref[7:8, :]
        prev2 = carry_ref[6:7, :]
        u1 = jnp.where(row == 0, prev1, pltpu.roll(u, 1, 0))
        u2 = jnp.where(row == 0, prev2, jnp.where(row == 1, prev1, pltpu.roll(u, 2, 0)))
        carry_ref[...] = u[TILE - 8:TILE, :]
        y = b_g * (wconv_ref[2:3, :] * u + wconv_ref[1:2, :] * u1 + wconv_ref[0:1, :] * u2)
        y = y * lax.rsqrt(jnp.mean(y * y, axis=-1, keepdims=True) + EPS) * gconv_ref[...]
        yc_ref[0, rows, :] = (y * _silu(z_c)).astype(BF16)

    return [post_qk, post_v, post_gz, post_conv]


def _in_proj(x, mod3, g_norm, positions, inv_freq, w_hi, w_conv, g_conv_out):
    bsz, seq, _ = x.shape
    n_blocks = seq // TILE
    step_rows = TILES_PER_STEP * TILE
    wide = N_HEADS * LANES
    row_spec = lambda width: pl.BlockSpec((1, step_rows, width), lambda b, i: (b, i, 0))
    col_spec = lambda height: pl.BlockSpec((1, height, step_rows), lambda b, i: (b, 0, i))
    const = lambda shape: pl.BlockSpec(shape, lambda b, i: (0,) * len(shape))
    kbd_rows = N_HEADS * n_blocks
    n_groups = w_hi.shape[1] // D_ATTN
    per_row = LANES // HALF
    pos_compact = jnp.repeat(positions.reshape(bsz, seq // per_row, per_row), HALF, axis=2)
    return pl.pallas_call(
        functools.partial(_in_proj_kernel, n_blocks, n_groups),
        grid=(bsz, n_blocks // TILES_PER_STEP),
        in_specs=[row_spec(D_MODEL),
                  pl.BlockSpec((1, 3, D_MODEL), lambda b, i: (b, 0, 0)),
                  const((1, D_MODEL)),
                  pl.BlockSpec((1, step_rows // per_row, LANES), lambda b, i: (b, i, 0)),
                  const((1, LANES)),
                  *[pl.BlockSpec((D_MODEL, D_ATTN), lambda b, i, g=g: (0, g))
                    for g in range(n_groups)],
                  const((CONV_WIDTH, D_CONV)),
                  const((1, D_CONV))],
        out_specs=[col_spec(wide), row_spec(wide), col_spec(N_HEADS * V_ROWS),
                   row_spec(D_ATTN), row_spec(D_CONV)],
        out_shape=[jax.ShapeDtypeStruct((bsz, wide, seq), BF16),
                   jax.ShapeDtypeStruct((bsz, seq, wide), BF16),
                   jax.ShapeDtypeStruct((bsz, N_HEADS * V_ROWS, seq), BF16),
                   jax.ShapeDtypeStruct((bsz, seq, D_ATTN), BF16),
                   jax.ShapeDtypeStruct((bsz, seq, D_CONV), BF16)],
        scratch_shapes=[pltpu.VMEM((kbd_rows, D_ATTN), F32),
                        pltpu.VMEM((8, D_CONV), F32),
                        pltpu.VMEM((TILES_PER_STEP, TILE, LANES), F32),
                        pltpu.VMEM((TILES_PER_STEP, TILE, LANES), F32),
                        pltpu.VMEM((step_rows, w_hi.shape[1]), F32),
                        pltpu.VMEM((TILES_PER_STEP, TILE, D_MODEL), BF16)],
        compiler_params=pltpu.CompilerParams(dimension_semantics=("arbitrary", "arbitrary"),
                                             vmem_limit_bytes=VMEM_LIMIT),
        name="in_proj",
    )(x, mod3, g_norm.reshape(1, D_MODEL), pos_compact, inv_freq,
      *([w_hi] * n_groups), w_conv, g_conv_out.reshape(1, D_CONV))


def _moba_kernel(n_chunks, tile_ref, chunk_ref, q_ref, k_ref, v_ref, o_ref, s_ref, smax_ref,
                 m_ref, acc_ref):
    n_past = n_chunks * (n_chunks - 1) // 2

    def own_item(t):
        return t, t

    def past_item(j):
        width = n_chunks - 1
        p = j // width
        r = j - p * width
        in_first = r < p
        return jnp.where(in_first, p, width - p), jnp.where(in_first, r, r - p)

    def own_keys(half):
        return (half + 1) * MXU_COLS

    def issue_unit(buf, tile, chunk, hd, half, own):
        n_keys = own_keys(half) if own else CHUNK
        start = pl.multiple_of(chunk * CHUNK, CHUNK)
        q_start = pl.multiple_of(tile * CHUNK + half * MXU_COLS, MXU_COLS)
        rows = slice(hd * LANES, (hd + 1) * LANES)
        cols = slice(half * MXU_COLS, (half + 1) * MXU_COLS)
        kj = k_ref[0, pl.ds(start, n_keys), rows]
        s = jnp.dot(kj, q_ref[0, rows, pl.ds(q_start, MXU_COLS)],
                    preferred_element_type=F32)
        s_ref[buf, hd, :n_keys, cols] = s.astype(BF16)
        if not own:
            smax_ref[buf, hd, :, cols] = jnp.max(s, axis=0, keepdims=True)

    def absorb_unit(buf, tile, chunk, hd, half, own):
        n_keys = own_keys(half) if own else CHUNK
        start = pl.multiple_of(chunk * CHUNK, CHUNK)
        cols = slice(half * MXU_COLS, (half + 1) * MXU_COLS)
        m = m_ref[tile, hd, :, cols]
        if own:
            diag = s_ref[buf, hd, n_keys - MXU_COLS:n_keys, cols].astype(F32)
            key = lax.broadcasted_iota(jnp.int32, diag.shape, 0)
            qry = lax.broadcasted_iota(jnp.int32, diag.shape, 1)
            diag = jnp.where(key <= qry, diag, NEG)
            blocks = [s_ref[buf, hd, b * MXU_COLS:(b + 1) * MXU_COLS, cols] for b in range(half)]
            blocks.append(diag.astype(BF16))
            smax = jnp.max(diag, axis=0, keepdims=True)
            for block in blocks[:-1]:
                smax = jnp.maximum(smax, jnp.max(block, axis=0, keepdims=True).astype(F32))
        else:
            blocks = [s_ref[buf, hd, :, cols]]
            smax = smax_ref[buf, hd, :, cols]
        m_new = jnp.maximum(m, smax).astype(BF16)
        p = [jnp.exp2(block - m_new) for block in blocks]
        p = p[0] if len(p) == 1 else jnp.concatenate(p, axis=0)
        m_new = m_new.astype(F32)
        vj = v_ref[0, hd * V_ROWS:(hd + 1) * V_ROWS, pl.ds(start, n_keys)]
        acc_ref[tile, hd, :, cols] = (jnp.exp2(m - m_new) * acc_ref[tile, hd, :, cols]
                                      + jnp.dot(vj, p, preferred_element_type=F32))
        m_ref[tile, hd, :, cols] = m_new

    def step(issue=None, absorb=None, issue_own=False, absorb_own=False):
        for hd in range(2):
            for half in range(CHUNK // MXU_COLS):
                if issue is not None:
                    issue_unit(*issue, hd, half, issue_own)
                if absorb is not None:
                    absorb_unit(*absorb, hd, half, absorb_own)

    def run_pairs(item, n_pairs, pairs_per_trip, own):
        def pair(u):
            step(issue=(1, *item(2 * u + 1)), absorb=(0, *item(2 * u)),
                 issue_own=own, absorb_own=own)
            step(issue=(0, *item(2 * u + 2)), absorb=(1, *item(2 * u + 1)),
                 issue_own=own, absorb_own=own)

        def trip(t, carry):
            for u in range(pairs_per_trip):
                pair(pairs_per_trip * t + u)
            return carry

        trips = n_pairs // pairs_per_trip
        lax.fori_loop(0, trips, trip, 0)
        for u in range(trips * pairs_per_trip, n_pairs):
            pair(u)

    m_ref[...] = jnp.full(m_ref.shape, -jnp.inf, F32)
    acc_ref[...] = jnp.zeros(acc_ref.shape, F32)
    assert n_chunks % 2 == 0 and n_past % 2 == 0
    step(issue=(0, *own_item(0)), issue_own=True)
    run_pairs(own_item, (n_chunks - 2) // 2, OWN_PAIRS_PER_TRIP, own=True)
    step(issue=(1, *own_item(n_chunks - 1)), absorb=(0, *own_item(n_chunks - 2)),
         issue_own=True, absorb_own=True)
    step(issue=(0, *past_item(0)), absorb=(1, *own_item(n_chunks - 1)), absorb_own=True)
    run_pairs(past_item, (n_past - 2) // 2, PAST_PAIRS_PER_TRIP, own=False)
    step(issue=(1, *past_item(n_past - 1)), absorb=(0, *past_item(n_past - 2)))
    step(absorb=(1, *past_item(n_past - 1)))

    def finish(tile, carry):
        outs = [acc_ref[tile, hd, :HEAD_DIM] / acc_ref[tile, hd, HEAD_DIM:HEAD_DIM + 1]
                for hd in range(2)]
        start = pl.multiple_of(tile * CHUNK, CHUNK)
        o_ref[0, pl.ds(start, CHUNK), :] = jnp.concatenate(outs, axis=0).T.astype(o_ref.dtype)
        return carry

    lax.fori_loop(0, n_chunks, finish, 0)


def _moba(q_aug, k_aug, v_aug):
    bsz, seq, wide = k_aug.shape
    pairs = wide // (2 * LANES)
    n_chunks = seq // CHUNK
    past = [(t, c) for t in range(n_chunks) for c in range(t)]
    item_tile = jnp.asarray([t for t, _ in past], jnp.int32)
    item_chunk = jnp.asarray([c for _, c in past], jnp.int32)
    return pl.pallas_call(
        functools.partial(_moba_kernel, n_chunks),
        grid_spec=pltpu.PrefetchScalarGridSpec(
            num_scalar_prefetch=2,
            grid=(bsz, pairs),
            in_specs=[pl.BlockSpec((1, 2 * LANES, seq), lambda b, p, *_: (b, p, 0)),
                      pl.BlockSpec((1, seq, 2 * LANES), lambda b, p, *_: (b, 0, p)),
                      pl.BlockSpec((1, 2 * V_ROWS, seq), lambda b, p, *_: (b, p, 0))],
            out_specs=pl.BlockSpec((1, seq, LANES), lambda b, p, *_: (b, 0, p)),
            scratch_shapes=[pltpu.VMEM((2, 2, CHUNK, CHUNK), BF16),
                            pltpu.VMEM((2, 2, 1, CHUNK), F32),
                            pltpu.VMEM((n_chunks, 2, 1, CHUNK), F32),
                            pltpu.VMEM((n_chunks, 2, V_ROWS, CHUNK), F32)]),
        out_shape=jax.ShapeDtypeStruct((bsz, seq, pairs * LANES), BF16),
        compiler_params=pltpu.CompilerParams(
            dimension_semantics=("arbitrary", "arbitrary"),
            vmem_limit_bytes=VMEM_LIMIT),
        name="moba",
    )(item_tile, item_chunk, q_aug, k_aug, v_aug)


def _out_proj_kernel(ya_ref, gz_ref, yc_ref, x_ref, mod_ref, gattn_ref, wout_ref, gfin_ref, o_ref):
    ya = ya_ref[0].astype(F32)
    yn = ya * lax.rsqrt(jnp.mean(ya * ya, axis=-1, keepdims=True) + EPS) * gattn_ref[...]
    yn = (yn * gz_ref[0].astype(F32)).astype(BF16)
    y = (jnp.dot(yn, wout_ref[0:D_ATTN, :], preferred_element_type=F32)
         + jnp.dot(yc_ref[0], wout_ref[D_ATTN:, :], preferred_element_type=F32))
    xo = x_ref[0] + mod_ref[0, 2:3, :] * y
    o_ref[0] = xo * lax.rsqrt(jnp.mean(xo * xo, axis=-1, keepdims=True) + EPS) * gfin_ref[...]


def _out_proj(y_attn, gz, yc, x, mod3, g_attn_out, w_out, g_final):
    bsz, seq, _ = x.shape
    row_spec = lambda width: pl.BlockSpec((1, OUT_TILE, width), lambda b, i: (b, i, 0))
    const = lambda shape: pl.BlockSpec(shape, lambda b, i: (0,) * len(shape))
    return pl.pallas_call(
        _out_proj_kernel,
        grid=(bsz, seq // OUT_TILE),
        in_specs=[row_spec(D_ATTN), row_spec(D_ATTN), row_spec(D_CONV), row_spec(D_MODEL),
                  pl.BlockSpec((1, 3, D_MODEL), lambda b, i: (b, 0, 0)),
                  const((1, D_ATTN)), const(w_out.shape), const((1, D_MODEL))],
        out_specs=row_spec(D_MODEL),
        out_shape=jax.ShapeDtypeStruct((bsz, seq, D_MODEL), F32),
        compiler_params=pltpu.CompilerParams(dimension_semantics=("arbitrary", "arbitrary"),
                                             vmem_limit_bytes=VMEM_LIMIT),
        name="out_proj",
    )(y_attn, gz, yc, x, mod3, g_attn_out.reshape(1, D_ATTN), w_out, g_final.reshape(1, D_MODEL))


def kernel(x, c, positions, w_ada, b_ada, g_norm, w_in, w_conv, g_attn_out, g_conv_out, w_out, g_final):
    bsz, seq, _ = x.shape
    assert seq % (TILES_PER_STEP * TILE) == 0 and seq % OUT_TILE == 0 and seq // TILE <= HEAD_DIM // 2
    mod3 = _adaln(c, w_ada, b_ada).reshape(bsz, 3, D_MODEL)

    inv_freq = ROPE_THETA ** (-jnp.arange(HALF, dtype=F32) / HALF)
    inv_freq = jnp.tile(inv_freq, LANES // HALF).reshape(1, LANES)

    q_aug, k_aug, v_aug, gz, yc = _in_proj(x, mod3, g_norm, positions, inv_freq,
                                           w_in.astype(BF16), w_conv, g_conv_out)
    y_attn = _moba(q_aug, k_aug, v_aug)
    return _out_proj(y_attn, gz, yc, x, mod3, g_attn_out, w_out.astype(BF16), g_final)
```

```python
import functools

import jax
import jax.numpy as jnp
from jax import lax
from jax.experimental import pallas as pl
from jax.experimental.pallas import tpu as pltpu

D_MODEL = 1024
D_ATTN = 512
D_CONV = 512
N_HEADS = 8
HEAD_DIM = 64
HALF = HEAD_DIM // 2
CONV_WIDTH = 3
MOBA_BLOCK = 256
MOBA_TOPK = 3
ROPE_THETA = 10000.0
EPS = 1e-6
NEG = -1e30
LOG2E = 1.4426950408889634

LANES = 128
V_ROWS = HEAD_DIM + 16
TILE = MOBA_BLOCK
CHUNK = 2 * TILE
TILES_PER_STEP = 2
MXU_COLS = 256
POS_PER_ROW = LANES // HALF
OWN_PAIRS_PER_TRIP = 7
PAST_PAIRS_PER_TRIP = 5
OUT_TILE = 1024
VMEM_LIMIT = 56 * 1024 * 1024

F32 = jnp.float32
BF16 = jnp.bfloat16


def _silu(z):
    return z * (1.0 / (1.0 + jnp.exp(-z)))


def _split_bf16(a):
    hi = a.astype(BF16)
    lo = (a - hi.astype(F32)).astype(BF16)
    return hi, lo


def _adaln_kernel(c_ref, w_ref, b_ref, o_ref):
    act = _silu(c_ref[...])
    rows = [jnp.sum(act[:, b:b + 1] * w_ref[...], axis=0, keepdims=True)
            for b in range(act.shape[1])]
    o_ref[...] = jnp.concatenate(rows, axis=0) + b_ref[...]


def _adaln(c, w_ada, b_ada):
    bsz = c.shape[0]
    n = w_ada.shape[1]
    bn = D_MODEL // 2
    return pl.pallas_call(
        _adaln_kernel,
        grid=(n // bn,),
        in_specs=[pl.BlockSpec((D_MODEL, bsz), lambda j: (0, 0)),
                  pl.BlockSpec((D_MODEL, bn), lambda j: (0, j)),
                  pl.BlockSpec((1, bn), lambda j: (0, j))],
        out_specs=pl.BlockSpec((bsz, bn), lambda j: (0, j)),
        out_shape=jax.ShapeDtypeStruct((bsz, n), F32),
        compiler_params=pltpu.CompilerParams(dimension_semantics=("arbitrary",),
                                             vmem_limit_bytes=VMEM_LIMIT),
        name="adaln",
    )(c.T, w_ada, b_ada.reshape(1, n))


def _in_proj_kernel(n_blocks, n_groups, x_ref, mod_ref, gnorm_ref, pos_ref, freq_ref, *refs):
    w_refs = refs[:n_groups]
    (wconv_ref, gconv_ref, q_ref, k_ref, v_ref, gz_ref, yc_ref,
     kbd_ref, carry_ref, cos_ref, sin_ref, proj_ref, h_ref) = refs[n_groups:]
    step = pl.program_id(1)

    @pl.when(step == 0)
    def _():
        kbd_ref[...] = jnp.zeros_like(kbd_ref)
        carry_ref[...] = jnp.zeros_like(carry_ref)

    n_cols = n_groups * D_ATTN
    col_groups = [(0, 2 * D_ATTN), (2 * D_ATTN, 3 * D_ATTN), (3 * D_ATTN, 4 * D_ATTN),
                  (4 * D_ATTN, n_cols)]

    def norm(r):
        x = x_ref[0, r * TILE:(r + 1) * TILE, :]
        xn = x * lax.rsqrt(jnp.mean(x * x, axis=-1, keepdims=True) + EPS) * gnorm_ref[...]
        h_ref[r] = (xn * (1.0 + mod_ref[0, 1:2, :]) + mod_ref[0, 0:1, :]).astype(BF16)

    def project(r, group):
        lo, hi = col_groups[group]
        for c0 in range(lo, hi, D_ATTN):
            proj_ref[r * TILE:(r + 1) * TILE, c0:c0 + D_ATTN] = jnp.dot(
                h_ref[r], w_refs[c0 // D_ATTN][...], preferred_element_type=F32)

    posts = [_block_posts(n_blocks, r, step * TILES_PER_STEP + r, pos_ref, freq_ref, wconv_ref,
                          gconv_ref, q_ref, k_ref, v_ref, gz_ref, yc_ref, kbd_ref, carry_ref,
                          cos_ref, sin_ref, proj_ref)
             for r in range(TILES_PER_STEP)]
    stages = [(r, group) for r in range(TILES_PER_STEP) for group in range(len(col_groups))]
    norm(0)
    project(*stages[0])
    for s, (r, group) in enumerate(stages):
        if s + 1 < len(stages):
            if stages[s + 1][1] == 0:
                norm(stages[s + 1][0])
            project(*stages[s + 1])
        posts[r][group]()


def _block_posts(n_blocks, r, i, pos_ref, freq_ref, wconv_ref, gconv_ref,
                 q_ref, k_ref, v_ref, gz_ref, yc_ref, kbd_ref, carry_ref, cos_ref, sin_ref,
                 proj_ref):
    rows = slice(r * TILE, (r + 1) * TILE)

    def proj(c0, c1):
        return proj_ref[rows, c0:c1]

    def post_qk():
        packed = TILE // POS_PER_ROW
        lane_c = lax.broadcasted_iota(jnp.int32, (packed, LANES), 1)
        ang = pos_ref[0, r * packed:(r + 1) * packed, :].astype(F32) * freq_ref[...]
        sign = jnp.where((lane_c % HEAD_DIM) < HALF, -1.0, 1.0)
        for table_ref, table, scale in ((cos_ref, jnp.cos(ang), None), (sin_ref, jnp.sin(ang), sign)):
            for a in range(POS_PER_ROW):
                z = jnp.where(lane_c // HALF == a, table, 0.0)
                z = z + pltpu.roll(z, 2 * HALF, 1)
                z = z + pltpu.roll(z, HALF, 1)
                table_ref[r, pl.ds(a, packed, stride=POS_PER_ROW), :] = (
                    z if scale is None else z * scale)
        cos = cos_ref[r]
        sin = sin_ref[r]

        lane = lax.broadcasted_iota(jnp.int32, (TILE, LANES), 1)
        first_half = (lane % HEAD_DIM) < HALF

        def rope(t):
            partner = jnp.where(first_half, pltpu.roll(t, LANES - HALF, 1), pltpu.roll(t, HALF, 1))
            return t * cos + partner * sin

        n_groups = D_ATTN // LANES
        q_groups = [rope(proj(g * LANES, (g + 1) * LANES)) for g in range(n_groups)]
        k_groups = [rope(proj(D_ATTN + g * LANES, D_ATTN + (g + 1) * LANES)) for g in range(n_groups)]
        q = jnp.concatenate(q_groups, axis=1)
        k = jnp.concatenate(k_groups, axis=1)

        nt = (((1,), (1,)), ((), ()))
        scores = lax.dot_general(kbd_ref[...].astype(BF16), q.astype(BF16), nt,
                                 preferred_element_type=F32)
        g3 = scores.reshape(N_HEADS, n_blocks, TILE)
        blk = lax.broadcasted_iota(jnp.int32, g3.shape, 1)
        past = blk < i
        g3 = jnp.where(past, g3, NEG)
        picked = jnp.zeros(g3.shape, jnp.bool_)
        for _ in range(min(MOBA_TOPK, n_blocks - 1)):
            top = jnp.max(g3, axis=1, keepdims=True)
            first = jnp.min(jnp.where(g3 == top, blk, n_blocks), axis=1, keepdims=True)
            hit = blk == first
            picked = jnp.logical_or(picked, hit)
            g3 = jnp.where(hit, -jnp.inf, g3)
        keep = jnp.logical_or(jnp.logical_and(picked, past), blk == i)
        bias = jnp.where(keep, 0.0, NEG)

        k_mean = jnp.mean(k, axis=0, keepdims=True)
        lane_w = lax.broadcasted_iota(jnp.int32, (1, D_ATTN), 1)
        for hd in range(N_HEADS):
            own = (lane_w // HEAD_DIM) == hd
            kbd_ref[pl.ds(hd * n_blocks + i, 1), :] = jnp.where(own, k_mean, 0.0)

        q_t = (q * (HEAD_DIM ** -0.5 * LOG2E)).T
        pad = jnp.zeros((HEAD_DIM - n_blocks, TILE), F32)
        parts = []
        for hd in range(N_HEADS):
            dims = q_t[hd * HEAD_DIM:(hd + 1) * HEAD_DIM]
            parts += [dims, bias[hd], pad] if hd % 2 == 0 else [bias[hd], pad, dims]
        q_ref[0, :, rows] = jnp.concatenate(parts, axis=0).astype(BF16)

        low = lane < HEAD_DIM
        onehot_even = jnp.where(lane == HEAD_DIM + i, 1.0, 0.0)
        onehot_odd = jnp.where(lane == i, 1.0, 0.0)
        for g in range(n_groups):
            even = slice(2 * g * LANES, (2 * g + 1) * LANES)
            odd = slice((2 * g + 1) * LANES, (2 * g + 2) * LANES)
            k_ref[0, rows, even] = jnp.where(low, k_groups[g], onehot_even).astype(BF16)
            k_ref[0, rows, odd] = jnp.where(low, onehot_odd, k_groups[g]).astype(BF16)

    def post_v():
        v_t = proj(2 * D_ATTN, 3 * D_ATTN).T
        ones = jnp.ones((V_ROWS - HEAD_DIM, TILE), F32)
        parts = []
        for hd in range(N_HEADS):
            parts += [v_t[hd * HEAD_DIM:(hd + 1) * HEAD_DIM], ones]
        v_ref[0, :, rows] = jnp.concatenate(parts, axis=0).astype(BF16)

    def post_gz():
        gz_ref[0, rows, :] = _silu(proj(3 * D_ATTN, 4 * D_ATTN)).astype(BF16)

    def post_conv():
        c0 = 4 * D_ATTN
        b_g = proj(c0, c0 + D_CONV)
        u = proj(c0 + D_CONV, c0 + 2 * D_CONV) * proj(c0 + 2 * D_CONV, c0 + 3 * D_CONV)
        z_c = proj(c0 + 3 * D_CONV, c0 + 4 * D_CONV)
        row = lax.broadcasted_iota(jnp.int32, u.shape, 0)
        prev1 = carry_ref[7:8, :]
        prev2 = carry_ref[6:7, :]
        u1 = jnp.where(row == 0, prev1, pltpu.roll(u, 1, 0))
        u2 = jnp.where(row == 0, prev2, jnp.where(row == 1, prev1, pltpu.roll(u, 2, 0)))
        carry_ref[...] = u[TILE - 8:TILE, :]
        y = b_g * (wconv_ref[2:3, :] * u + wconv_ref[1:2, :] * u1 + wconv_ref[0:1, :] * u2)
        y = y * lax.rsqrt(jnp.mean(y * y, axis=-1, keepdims=True) + EPS) * gconv_ref[...]
        yc_ref[0, rows, :] = (y * _silu(z_c)).astype(BF16)

    return [post_qk, post_v, post_gz, post_conv]


def _in_proj(x, mod3, g_norm, positions, inv_freq, w_hi, w_conv, g_conv_out):
    bsz, seq, _ = x.shape
    n_blocks = seq // TILE
    step_rows = TILES_PER_STEP * TILE
    wide = N_HEADS * LANES
    row_spec = lambda width: pl.BlockSpec((1, step_rows, width), lambda b, i: (b, i, 0))
    col_spec = lambda height: pl.BlockSpec((1, height, step_rows), lambda b, i: (b, 0, i))
    const = lambda shape: pl.BlockSpec(shape, lambda b, i: (0,) * len(shape))
    kbd_rows = N_HEADS * n_blocks
    n_groups = w_hi.shape[1] // D_ATTN
    pos_compact = jnp.repeat(positions.reshape(bsz, seq // POS_PER_ROW, POS_PER_ROW), HALF, axis=2)
    return pl.pallas_call(
        functools.partial(_in_proj_kernel, n_blocks, n_groups),
        grid=(bsz, n_blocks // TILES_PER_STEP),
        in_specs=[row_spec(D_MODEL),
                  pl.BlockSpec((1, 3, D_MODEL), lambda b, i: (b, 0, 0)),
                  const((1, D_MODEL)),
                  pl.BlockSpec((1, step_rows // POS_PER_ROW, LANES), lambda b, i: (b, i, 0)),
                  const((1, LANES)),
                  *[pl.BlockSpec((D_MODEL, D_ATTN), lambda b, i, g=g: (0, g))
                    for g in range(n_groups)],
                  const((CONV_WIDTH, D_CONV)),
                  const((1, D_CONV))],
        out_specs=[col_spec(wide), row_spec(wide), col_spec(N_HEADS * V_ROWS),
                   row_spec(D_ATTN), row_spec(D_CONV)],
        out_shape=[jax.ShapeDtypeStruct((bsz, wide, seq), BF16),
                   jax.ShapeDtypeStruct((bsz, seq, wide), BF16),
                   jax.ShapeDtypeStruct((bsz, N_HEADS * V_ROWS, seq), BF16),
                   jax.ShapeDtypeStruct((bsz, seq, D_ATTN), BF16),
                   jax.ShapeDtypeStruct((bsz, seq, D_CONV), BF16)],
        scratch_shapes=[pltpu.VMEM((kbd_rows, D_ATTN), F32),
                        pltpu.VMEM((8, D_CONV), F32),
                        pltpu.VMEM((TILES_PER_STEP, TILE, LANES), F32),
                        pltpu.VMEM((TILES_PER_STEP, TILE, LANES), F32),
                        pltpu.VMEM((step_rows, w_hi.shape[1]), F32),
                        pltpu.VMEM((TILES_PER_STEP, TILE, D_MODEL), BF16)],
        compiler_params=pltpu.CompilerParams(dimension_semantics=("arbitrary", "arbitrary"),
                                             vmem_limit_bytes=VMEM_LIMIT),
        name="in_proj",
    )(x, mod3, g_norm.reshape(1, D_MODEL), pos_compact, inv_freq,
      *([w_hi] * n_groups), w_conv, g_conv_out.reshape(1, D_CONV))


def _moba_kernel(n_chunks, tile_ref, chunk_ref, q_ref, k_ref, v_ref, o_ref, s_ref, smax_ref,
                 m_ref, acc_ref):
    n_past = n_chunks * (n_chunks - 1) // 2

    def own_item(t):
        return t, t

    def past_item(j):
        return tile_ref[j], chunk_ref[j]

    def own_keys(half):
        return (half + 1) * MXU_COLS

    def issue_unit(buf, tile, chunk, hd, half, own):
        n_keys = own_keys(half) if own else CHUNK
        start = pl.multiple_of(chunk * CHUNK, CHUNK)
        q_start = pl.multiple_of(tile * CHUNK + half * MXU_COLS, MXU_COLS)
        rows = slice(hd * LANES, (hd + 1) * LANES)
        cols = slice(half * MXU_COLS, (half + 1) * MXU_COLS)
        kj = k_ref[0, pl.ds(start, n_keys), rows]
        s = jnp.dot(kj, q_ref[0, rows, pl.ds(q_start, MXU_COLS)],
                    preferred_element_type=F32)
        s_ref[buf, hd, :n_keys, cols] = s.astype(BF16)
        if not own:
            smax_ref[buf, hd, :, cols] = jnp.max(s, axis=0, keepdims=True)

    def absorb_unit(buf, tile, chunk, hd, half, own):
        n_keys = own_keys(half) if own else CHUNK
        start = pl.multiple_of(chunk * CHUNK, CHUNK)
        cols = slice(half * MXU_COLS, (half + 1) * MXU_COLS)
        m = m_ref[tile, hd, :, cols]
        if own:
            diag = s_ref[buf, hd, n_keys - MXU_COLS:n_keys, cols].astype(F32)
            key = lax.broadcasted_iota(jnp.int32, diag.shape, 0)
            qry = lax.broadcasted_iota(jnp.int32, diag.shape, 1)
            diag = jnp.where(key <= qry, diag, NEG)
            blocks = [s_ref[buf, hd, b * MXU_COLS:(b + 1) * MXU_COLS, cols] for b in range(half)]
            blocks.append(diag.astype(BF16))
            smax = jnp.max(diag, axis=0, keepdims=True)
            for block in blocks[:-1]:
                smax = jnp.maximum(smax, jnp.max(block, axis=0, keepdims=True).astype(F32))
        else:
            blocks = [s_ref[buf, hd, :, cols]]
            smax = smax_ref[buf, hd, :, cols]
        m_new = jnp.maximum(m, smax).astype(BF16)
        p = [jnp.exp2(block - m_new) for block in blocks]
        p = p[0] if len(p) == 1 else jnp.concatenate(p, axis=0)
        m_new = m_new.astype(F32)
        vj = v_ref[0, hd * V_ROWS:(hd + 1) * V_ROWS, pl.ds(start, n_keys)]
        acc_ref[tile, hd, :, cols] = (jnp.exp2(m - m_new) * acc_ref[tile, hd, :, cols]
                                      + jnp.dot(vj, p, preferred_element_type=F32))
        m_ref[tile, hd, :, cols] = m_new

    def step(issue=None, absorb=None, issue_own=False, absorb_own=False):
        for hd in range(2):
            for half in range(CHUNK // MXU_COLS):
                if issue is not None:
                    issue_unit(*issue, hd, half, issue_own)
                if absorb is not None:
                    absorb_unit(*absorb, hd, half, absorb_own)

    def run_pairs(item, n_pairs, pairs_per_trip, own):
        def pair(u):
            step(issue=(1, *item(2 * u + 1)), absorb=(0, *item(2 * u)),
                 issue_own=own, absorb_own=own)
            step(issue=(0, *item(2 * u + 2)), absorb=(1, *item(2 * u + 1)),
                 issue_own=own, absorb_own=own)

        def trip(t, carry):
            for u in range(pairs_per_trip):
                pair(pairs_per_trip * t + u)
            return carry

        trips = n_pairs // pairs_per_trip
        lax.fori_loop(0, trips, trip, 0)
        for u in range(trips * pairs_per_trip, n_pairs):
            pair(u)

    m_ref[...] = jnp.full(m_ref.shape, -jnp.inf, F32)
    acc_ref[...] = jnp.zeros(acc_ref.shape, F32)
    assert n_chunks % 2 == 0 and n_past % 2 == 0
    step(issue=(0, *own_item(0)), issue_own=True)
    run_pairs(own_item, (n_chunks - 2) // 2, OWN_PAIRS_PER_TRIP, own=True)
    step(issue=(1, *own_item(n_chunks - 1)), absorb=(0, *own_item(n_chunks - 2)),
         issue_own=True, absorb_own=True)
    step(issue=(0, *past_item(0)), absorb=(1, *own_item(n_chunks - 1)), absorb_own=True)
    run_pairs(past_item, (n_past - 2) // 2, PAST_PAIRS_PER_TRIP, own=False)
    step(issue=(1, *past_item(n_past - 1)), absorb=(0, *past_item(n_past - 2)))
    step(absorb=(1, *past_item(n_past - 1)))

    def finish(tile, carry):
        outs = [acc_ref[tile, hd, :HEAD_DIM] / acc_ref[tile, hd, HEAD_DIM:HEAD_DIM + 1]
                for hd in range(2)]
        start = pl.multiple_of(tile * CHUNK, CHUNK)
        o_ref[0, pl.ds(start, CHUNK), :] = jnp.concatenate(outs, axis=0).T.astype(o_ref.dtype)
        return carry

    lax.fori_loop(0, n_chunks, finish, 0)


def _moba(q_aug, k_aug, v_aug):
    bsz, seq, wide = k_aug.shape
    pairs = wide // (2 * LANES)
    n_chunks = seq // CHUNK
    past = [(t, c) for t in range(n_chunks) for c in range(t)]
    item_tile = jnp.asarray([t for t, _ in past], jnp.int32)
    item_chunk = jnp.asarray([c for _, c in past], jnp.int32)
    return pl.pallas_call(
        functools.partial(_moba_kernel, n_chunks),
        grid_spec=pltpu.PrefetchScalarGridSpec(
            num_scalar_prefetch=2,
            grid=(bsz, pairs),
            in_specs=[pl.BlockSpec((1, 2 * LANES, seq), lambda b, p, *_: (b, p, 0)),
                      pl.BlockSpec((1, seq, 2 * LANES), lambda b, p, *_: (b, 0, p)),
                      pl.BlockSpec((1, 2 * V_ROWS, seq), lambda b, p, *_: (b, p, 0))],
            out_specs=pl.BlockSpec((1, seq, LANES), lambda b, p, *_: (b, 0, p)),
            scratch_shapes=[pltpu.VMEM((2, 2, CHUNK, CHUNK), BF16),
                            pltpu.VMEM((2, 2, 1, CHUNK), F32),
                            pltpu.VMEM((n_chunks, 2, 1, CHUNK), F32),
                            pltpu.VMEM((n_chunks, 2, V_ROWS, CHUNK), F32)]),
        out_shape=jax.ShapeDtypeStruct((bsz, seq, pairs * LANES), BF16),
        compiler_params=pltpu.CompilerParams(
            dimension_semantics=("arbitrary", "arbitrary"),
            vmem_limit_bytes=VMEM_LIMIT),
        name="moba",
    )(item_tile, item_chunk, q_aug, k_aug, v_aug)


def _out_proj_kernel(ya_ref, gz_ref, yc_ref, x_ref, mod_ref, gattn_ref, wout_ref, gfin_ref, o_ref):
    ya = ya_ref[0].astype(F32)
    yn = ya * lax.rsqrt(jnp.mean(ya * ya, axis=-1, keepdims=True) + EPS) * gattn_ref[...]
    yn = (yn * gz_ref[0].astype(F32)).astype(BF16)
    y = (jnp.dot(yn, wout_ref[0:D_ATTN, :], preferred_element_type=F32)
         + jnp.dot(yc_ref[0], wout_ref[D_ATTN:, :], preferred_element_type=F32))
    xo = x_ref[0] + mod_ref[0, 2:3, :] * y
    o_ref[0] = xo * lax.rsqrt(jnp.mean(xo * xo, axis=-1, keepdims=True) + EPS) * gfin_ref[...]


def _out_proj(y_attn, gz, yc, x, mod3, g_attn_out, w_out, g_final):
    bsz, seq, _ = x.shape
    row_spec = lambda width: pl.BlockSpec((1, OUT_TILE, width), lambda b, i: (b, i, 0))
    const = lambda shape: pl.BlockSpec(shape, lambda b, i: (0,) * len(shape))
    return pl.pallas_call(
        _out_proj_kernel,
        grid=(bsz, seq // OUT_TILE),
        in_specs=[row_spec(D_ATTN), row_spec(D_ATTN), row_spec(D_CONV), row_spec(D_MODEL),
                  pl.BlockSpec((1, 3, D_MODEL), lambda b, i: (b, 0, 0)),
                  const((1, D_ATTN)), const(w_out.shape), const((1, D_MODEL))],
        out_specs=row_spec(D_MODEL),
        out_shape=jax.ShapeDtypeStruct((bsz, seq, D_MODEL), F32),
        compiler_params=pltpu.CompilerParams(dimension_semantics=("arbitrary", "arbitrary"),
                                             vmem_limit_bytes=VMEM_LIMIT),
        name="out_proj",
    )(y_attn, gz, yc, x, mod3, g_attn_out.reshape(1, D_ATTN), w_out, g_final.reshape(1, D_MODEL))


def kernel(x, c, positions, w_ada, b_ada, g_norm, w_in, w_conv, g_attn_out, g_conv_out, w_out, g_final):
    bsz, seq, _ = x.shape
    assert seq % (TILES_PER_STEP * TILE) == 0 and seq % OUT_TILE == 0 and seq // TILE <= HEAD_DIM // 2
    mod3 = _adaln(c, w_ada, b_ada).reshape(bsz, 3, D_MODEL)

    inv_freq = ROPE_THETA ** (-jnp.arange(HALF, dtype=F32) / HALF)
    inv_freq = jnp.tile(inv_freq, LANES // HALF).reshape(1, LANES)

    q_aug, k_aug, v_aug, gz, yc = _in_proj(x, mod3, g_norm, positions, inv_freq,
                                           w_in.astype(BF16), w_conv, g_conv_out)
    y_attn = _moba(q_aug, k_aug, v_aug)
    return _out_proj(y_attn, gz, yc, x, mod3, g_attn_out, w_out.astype(BF16), g_final)
```

```python
import functools

import jax
import jax.numpy as jnp
from jax import lax
from jax.experimental import pallas as pl
from jax.experimental.pallas import tpu as pltpu

D_MODEL = 1024
D_ATTN = 512
D_CONV = 512
N_HEADS = 8
HEAD_DIM = 64
HALF = HEAD_DIM // 2
CONV_WIDTH = 3
MOBA_BLOCK = 256
MOBA_TOPK = 3
ROPE_THETA = 10000.0
EPS = 1e-6
NEG = -1e30
LOG2E = 1.4426950408889634

LANES = 128
V_ROWS = HEAD_DIM + 16
TILE = MOBA_BLOCK
CHUNK = 2 * TILE
TILES_PER_STEP = 2
MXU_COLS = 256
POS_PER_ROW = LANES // HALF
OWN_PAIRS_PER_TRIP = 7
PAST_PAIRS_PER_TRIP = 5
OUT_TILE = 1024
VMEM_LIMIT = 56 * 1024 * 1024

F32 = jnp.float32
BF16 = jnp.bfloat16


def _silu(z):
    return z * (1.0 / (1.0 + jnp.exp(-z)))


def _split_bf16(a):
    hi = a.astype(BF16)
    lo = (a - hi.astype(F32)).astype(BF16)
    return hi, lo


def _adaln_kernel(c_ref, w_ref, b_ref, o_ref):
    act = _silu(c_ref[...])
    rows = [jnp.sum(act[:, b:b + 1] * w_ref[...], axis=0, keepdims=True)
            for b in range(act.shape[1])]
    o_ref[...] = jnp.concatenate(rows, axis=0) + b_ref[...]


def _adaln(c, w_ada, b_ada):
    bsz = c.shape[0]
    n = w_ada.shape[1]
    bn = D_MODEL
    return pl.pallas_call(
        _adaln_kernel,
        grid=(n // bn,),
        in_specs=[pl.BlockSpec((D_MODEL, bsz), lambda j: (0, 0)),
                  pl.BlockSpec((D_MODEL, bn), lambda j: (0, j)),
                  pl.BlockSpec((1, bn), lambda j: (0, j))],
        out_specs=pl.BlockSpec((bsz, bn), lambda j: (0, j)),
        out_shape=jax.ShapeDtypeStruct((bsz, n), F32),
        compiler_params=pltpu.CompilerParams(dimension_semantics=("arbitrary",),
                                             vmem_limit_bytes=VMEM_LIMIT),
        name="adaln",
    )(c.T, w_ada, b_ada.reshape(1, n))


def _in_proj_kernel(n_blocks, n_groups, x_ref, mod_ref, gnorm_ref, pos_ref, freq_ref, *refs):
    w_refs = refs[:n_groups]
    (wconv_ref, gconv_ref, q_ref, k_ref, v_ref, gz_ref, yc_ref,
     kbd_ref, carry_ref, cos_ref, sin_ref, proj_ref, h_ref) = refs[n_groups:]
    step = pl.program_id(1)

    @pl.when(step == 0)
    def _():
        kbd_ref[...] = jnp.zeros_like(kbd_ref)
        carry_ref[...] = jnp.zeros_like(carry_ref)

    n_cols = n_groups * D_ATTN
    col_groups = [(0, 2 * D_ATTN), (2 * D_ATTN, 3 * D_ATTN), (3 * D_ATTN, 4 * D_ATTN),
                  (4 * D_ATTN, n_cols)]

    def norm(r):
        x = x_ref[0, r * TILE:(r + 1) * TILE, :]
        xn = x * lax.rsqrt(jnp.mean(x * x, axis=-1, keepdims=True) + EPS) * gnorm_ref[...]
        h_ref[r] = (xn * (1.0 + mod_ref[0, 1:2, :]) + mod_ref[0, 0:1, :]).astype(BF16)

    def project(r, group):
        lo, hi = col_groups[group]
        for c0 in range(lo, hi, D_ATTN):
            proj_ref[r * TILE:(r + 1) * TILE, c0:c0 + D_ATTN] = jnp.dot(
                h_ref[r], w_refs[c0 // D_ATTN][...], preferred_element_type=F32)

    posts = [_block_posts(n_blocks, r, step * TILES_PER_STEP + r, pos_ref, freq_ref, wconv_ref,
                          gconv_ref, q_ref, k_ref, v_ref, gz_ref, yc_ref, kbd_ref, carry_ref,
                          cos_ref, sin_ref, proj_ref)
             for r in range(TILES_PER_STEP)]
    stages = [(r, group) for r in range(TILES_PER_STEP) for group in range(len(col_groups))]
    norm(0)
    project(*stages[0])
    for s, (r, group) in enumerate(stages):
        if s + 1 < len(stages):
            if stages[s + 1][1] == 0:
                norm(stages[s + 1][0])
            project(*stages[s + 1])
        posts[r][group]()


def _block_posts(n_blocks, r, i, pos_ref, freq_ref, wconv_ref, gconv_ref,
                 q_ref, k_ref, v_ref, gz_ref, yc_ref, kbd_ref, carry_ref, cos_ref, sin_ref,
                 proj_ref):
    rows = slice(r * TILE, (r + 1) * TILE)

    def proj(c0, c1):
        return proj_ref[rows, c0:c1]

    def post_qk():
        packed = TILE // POS_PER_ROW
        lane_c = lax.broadcasted_iota(jnp.int32, (packed, LANES), 1)
        ang = pos_ref[0, r * packed:(r + 1) * packed, :].astype(F32) * freq_ref[...]
        sign = jnp.where((lane_c % HEAD_DIM) < HALF, -1.0, 1.0)
        for table_ref, table, scale in ((cos_ref, jnp.cos(ang), None), (sin_ref, jnp.sin(ang), sign)):
            for a in range(POS_PER_ROW):
                z = jnp.where(lane_c // HALF == a, table, 0.0)
                z = z + pltpu.roll(z, 2 * HALF, 1)
                z = z + pltpu.roll(z, HALF, 1)
                table_ref[r, pl.ds(a, packed, stride=POS_PER_ROW), :] = (
                    z if scale is None else z * scale)
        cos = cos_ref[r]
        sin = sin_ref[r]

        lane = lax.broadcasted_iota(jnp.int32, (TILE, LANES), 1)
        first_half = (lane % HEAD_DIM) < HALF

        def rope(t):
            partner = jnp.where(first_half, pltpu.roll(t, LANES - HALF, 1), pltpu.roll(t, HALF, 1))
            return t * cos + partner * sin

        n_groups = D_ATTN // LANES
        q_groups = [rope(proj(g * LANES, (g + 1) * LANES)) for g in range(n_groups)]
        k_groups = [rope(proj(D_ATTN + g * LANES, D_ATTN + (g + 1) * LANES)) for g in range(n_groups)]
        q = jnp.concatenate(q_groups, axis=1)
        k = jnp.concatenate(k_groups, axis=1)

        nt = (((1,), (1,)), ((), ()))
        scores = lax.dot_general(kbd_ref[...].astype(BF16), q.astype(BF16), nt,
                                 preferred_element_type=F32)
        g3 = scores.reshape(N_HEADS, n_blocks, TILE)
        blk = lax.broadcasted_iota(jnp.int32, g3.shape, 1)
        past = blk < i
        g3 = jnp.where(past, g3, NEG)
        picked = jnp.zeros(g3.shape, jnp.bool_)
        for _ in range(min(MOBA_TOPK, n_blocks - 1)):
            top = jnp.max(g3, axis=1, keepdims=True)
            first = jnp.min(jnp.where(g3 == top, blk, n_blocks), axis=1, keepdims=True)
            hit = blk == first
            picked = jnp.logical_or(picked, hit)
            g3 = jnp.where(hit, -jnp.inf, g3)
        keep = jnp.logical_or(jnp.logical_and(picked, past), blk == i)
        bias = jnp.where(keep, 0.0, NEG)

        k_mean = jnp.mean(k, axis=0, keepdims=True)
        lane_w = lax.broadcasted_iota(jnp.int32, (1, D_ATTN), 1)
        for hd in range(N_HEADS):
            own = (lane_w // HEAD_DIM) == hd
            kbd_ref[pl.ds(hd * n_blocks + i, 1), :] = jnp.where(own, k_mean, 0.0)

        q_t = (q * (HEAD_DIM ** -0.5 * LOG2E)).T
        pad = jnp.zeros((HEAD_DIM - n_blocks, TILE), F32)
        parts = []
        for hd in range(N_HEADS):
            dims = q_t[hd * HEAD_DIM:(hd + 1) * HEAD_DIM]
            parts += [dims, bias[hd], pad] if hd % 2 == 0 else [bias[hd], pad, dims]
        q_ref[0, :, rows] = jnp.concatenate(parts, axis=0).astype(BF16)

        low = lane < HEAD_DIM
        onehot_even = jnp.where(lane == HEAD_DIM + i, 1.0, 0.0)
        onehot_odd = jnp.where(lane == i, 1.0, 0.0)
        for g in range(n_groups):
            even = slice(2 * g * LANES, (2 * g + 1) * LANES)
            odd = slice((2 * g + 1) * LANES, (2 * g + 2) * LANES)
            k_ref[0, rows, even] = jnp.where(low, k_groups[g], onehot_even).astype(BF16)
            k_ref[0, rows, odd] = jnp.where(low, onehot_odd, k_groups[g]).astype(BF16)

    def post_v():
        v_t = proj(2 * D_ATTN, 3 * D_ATTN).T
        ones = jnp.ones((V_ROWS - HEAD_DIM, TILE), F32)
        parts = []
        for hd in range(N_HEADS):
            parts += [v_t[hd * HEAD_DIM:(hd + 1) * HEAD_DIM], ones]
        v_ref[0, :, rows] = jnp.concatenate(parts, axis=0).astype(BF16)

    def post_gz():
        gz_ref[0, rows, :] = _silu(proj(3 * D_ATTN, 4 * D_ATTN)).astype(BF16)

    def post_conv():
        c0 = 4 * D_ATTN
        b_g = proj(c0, c0 + D_CONV)
        u = proj(c0 + D_CONV, c0 + 2 * D_CONV) * proj(c0 + 2 * D_CONV, c0 + 3 * D_CONV)
        z_c = proj(c0 + 3 * D_CONV, c0 + 4 * D_CONV)
        row = lax.broadcasted_iota(jnp.int32, u.shape, 0)
        prev1 = carry_ref[7:8, :]
        prev2 = carry_ref[6:7, :]
        u1 = jnp.where(row == 0, prev1, pltpu.roll(u, 1, 0))
        u2 = jnp.where(row == 0, prev2, jnp.where(row == 1, prev1, pltpu.roll(u, 2, 0)))
        carry_ref[...] = u[TILE - 8:TILE, :]
        y = b_g * (wconv_ref[2:3, :] * u + wconv_ref[1:2, :] * u1 + wconv_ref[0:1, :] * u2)
        y = y * lax.rsqrt(jnp.mean(y * y, axis=-1, keepdims=True) + EPS) * gconv_ref[...]
        yc_ref[0, rows, :] = (y * _silu(z_c)).astype(BF16)

    return [post_qk, post_v, post_gz, post_conv]


def _in_proj(x, mod3, g_norm, positions, inv_freq, w_hi, w_conv, g_conv_out):
    bsz, seq, _ = x.shape
    n_blocks = seq // TILE
    step_rows = TILES_PER_STEP * TILE
    wide = N_HEADS * LANES
    row_spec = lambda width: pl.BlockSpec((1, step_rows, width), lambda b, i: (b, i, 0))
    col_spec = lambda height: pl.BlockSpec((1, height, step_rows), lambda b, i: (b, 0, i))
    const = lambda shape: pl.BlockSpec(shape, lambda b, i: (0,) * len(shape))
    kbd_rows = N_HEADS * n_blocks
    n_groups = w_hi.shape[1] // D_ATTN
    pos_compact = jnp.repeat(positions.reshape(bsz, seq // POS_PER_ROW, POS_PER_ROW), HALF, axis=2)
    return pl.pallas_call(
        functools.partial(_in_proj_kernel, n_blocks, n_groups),
        grid=(bsz, n_blocks // TILES_PER_STEP),
        in_specs=[row_spec(D_MODEL),
                  pl.BlockSpec((1, 3, D_MODEL), lambda b, i: (b, 0, 0)),
                  const((1, D_MODEL)),
                  pl.BlockSpec((1, step_rows // POS_PER_ROW, LANES), lambda b, i: (b, i, 0)),
                  const((1, LANES)),
                  *[pl.BlockSpec((D_MODEL, D_ATTN), lambda b, i, g=g: (0, g))
                    for g in range(n_groups)],
                  const((CONV_WIDTH, D_CONV)),
                  const((1, D_CONV))],
        out_specs=[col_spec(wide), row_spec(wide), col_spec(N_HEADS * V_ROWS),
                   row_spec(D_ATTN), row_spec(D_CONV)],
        out_shape=[jax.ShapeDtypeStruct((bsz, wide, seq), BF16),
                   jax.ShapeDtypeStruct((bsz, seq, wide), BF16),
                   jax.ShapeDtypeStruct((bsz, N_HEADS * V_ROWS, seq), BF16),
                   jax.ShapeDtypeStruct((bsz, seq, D_ATTN), BF16),
                   jax.ShapeDtypeStruct((bsz, seq, D_CONV), BF16)],
        scratch_shapes=[pltpu.VMEM((kbd_rows, D_ATTN), F32),
                        pltpu.VMEM((8, D_CONV), F32),
                        pltpu.VMEM((TILES_PER_STEP, TILE, LANES), F32),
                        pltpu.VMEM((TILES_PER_STEP, TILE, LANES), F32),
                        pltpu.VMEM((step_rows, w_hi.shape[1]), F32),
                        pltpu.VMEM((TILES_PER_STEP, TILE, D_MODEL), BF16)],
        compiler_params=pltpu.CompilerParams(dimension_semantics=("arbitrary", "arbitrary"),
                                             vmem_limit_bytes=VMEM_LIMIT),
        name="in_proj",
    )(x, mod3, g_norm.reshape(1, D_MODEL), pos_compact, inv_freq,
      *([w_hi] * n_groups), w_conv, g_conv_out.reshape(1, D_CONV))


def _moba_kernel(n_chunks, tile_ref, chunk_ref, q_ref, k_ref, v_ref, o_ref, s_ref, smax_ref,
                 m_ref, acc_ref):
    n_past = n_chunks * (n_chunks - 1) // 2

    def own_item(t):
        return t, t

    def past_item(j):
        return tile_ref[j], chunk_ref[j]

    def own_keys(half):
        return (half + 1) * MXU_COLS

    def issue_unit(buf, tile, chunk, hd, half, own):
        n_keys = own_keys(half) if own else CHUNK
        start = pl.multiple_of(chunk * CHUNK, CHUNK)
        q_start = pl.multiple_of(tile * CHUNK + half * MXU_COLS, MXU_COLS)
        rows = slice(hd * LANES, (hd + 1) * LANES)
        cols = slice(half * MXU_COLS, (half + 1) * MXU_COLS)
        kj = k_ref[0, pl.ds(start, n_keys), rows]
        s = jnp.dot(kj, q_ref[0, rows, pl.ds(q_start, MXU_COLS)],
                    preferred_element_type=F32)
        s_ref[buf, hd, :n_keys, cols] = s.astype(BF16)
        if not own:
            smax_ref[buf, hd, :, cols] = jnp.max(s, axis=0, keepdims=True)

    def absorb_unit(buf, tile, chunk, hd, half, own):
        n_keys = own_keys(half) if own else CHUNK
        start = pl.multiple_of(chunk * CHUNK, CHUNK)
        cols = slice(half * MXU_COLS, (half + 1) * MXU_COLS)
        m = m_ref[tile, hd, :, cols]
        if own:
            diag = s_ref[buf, hd, n_keys - MXU_COLS:n_keys, cols].astype(F32)
            key = lax.broadcasted_iota(jnp.int32, diag.shape, 0)
            qry = lax.broadcasted_iota(jnp.int32, diag.shape, 1)
            diag = jnp.where(key <= qry, diag, NEG)
            blocks = [s_ref[buf, hd, b * MXU_COLS:(b + 1) * MXU_COLS, cols] for b in range(half)]
            blocks.append(diag.astype(BF16))
            smax = jnp.max(diag, axis=0, keepdims=True)
            for block in blocks[:-1]:
                smax = jnp.maximum(smax, jnp.max(block, axis=0, keepdims=True).astype(F32))
        else:
            blocks = [s_ref[buf, hd, :, cols]]
            smax = smax_ref[buf, hd, :, cols]
        m_new = jnp.maximum(m, smax).astype(BF16)
        p = [jnp.exp2(block - m_new) for block in blocks]
        p = p[0] if len(p) == 1 else jnp.concatenate(p, axis=0)
        m_new = m_new.astype(F32)
        vj = v_ref[0, hd * V_ROWS:(hd + 1) * V_ROWS, pl.ds(start, n_keys)]
        acc_ref[tile, hd, :, cols] = (jnp.exp2(m - m_new) * acc_ref[tile, hd, :, cols]
                                      + jnp.dot(vj, p, preferred_element_type=F32))
        m_ref[tile, hd, :, cols] = m_new

    def step(issue=None, absorb=None, issue_own=False, absorb_own=False):
        for hd in range(2):
            for half in range(CHUNK // MXU_COLS):
                if issue is not None:
                    issue_unit(*issue, hd, half, issue_own)
                if absorb is not None:
                    absorb_unit(*absorb, hd, half, absorb_own)

    def run_pairs(item, n_pairs, pairs_per_trip, own):
        def pair(u):
            step(issue=(1, *item(2 * u + 1)), absorb=(0, *item(2 * u)),
                 issue_own=own, absorb_own=own)
            step(issue=(0, *item(2 * u + 2)), absorb=(1, *item(2 * u + 1)),
                 issue_own=own, absorb_own=own)

        def trip(t, carry):
            for u in range(pairs_per_trip):
                pair(pairs_per_trip * t + u)
            return carry

        trips = n_pairs // pairs_per_trip
        lax.fori_loop(0, trips, trip, 0)
        for u in range(trips * pairs_per_trip, n_pairs):
            pair(u)

    m_ref[...] = jnp.full(m_ref.shape, -jnp.inf, F32)
    acc_ref[...] = jnp.zeros(acc_ref.shape, F32)
    assert n_chunks % 2 == 0 and n_past % 2 == 0
    step(issue=(0, *own_item(0)), issue_own=True)
    run_pairs(own_item, (n_chunks - 2) // 2, OWN_PAIRS_PER_TRIP, own=True)
    step(issue=(1, *own_item(n_chunks - 1)), absorb=(0, *own_item(n_chunks - 2)),
         issue_own=True, absorb_own=True)
    step(issue=(0, *past_item(0)), absorb=(1, *own_item(n_chunks - 1)), absorb_own=True)
    run_pairs(past_item, (n_past - 2) // 2, PAST_PAIRS_PER_TRIP, own=False)
    step(issue=(1, *past_item(n_past - 1)), absorb=(0, *past_item(n_past - 2)))
    step(absorb=(1, *past_item(n_past - 1)))

    def finish(tile, carry):
        outs = [acc_ref[tile, hd, :HEAD_DIM] / acc_ref[tile, hd, HEAD_DIM:HEAD_DIM + 1]
                for hd in range(2)]
        start = pl.multiple_of(tile * CHUNK, CHUNK)
        o_ref[0, pl.ds(start, CHUNK), :] = jnp.concatenate(outs, axis=0).T.astype(o_ref.dtype)
        return carry

    lax.fori_loop(0, n_chunks, finish, 0)


def _moba(q_aug, k_aug, v_aug):
    bsz, seq, wide = k_aug.shape
    pairs = wide // (2 * LANES)
    n_chunks = seq // CHUNK
    past = [(t, c) for t in range(n_chunks) for c in range(t)]
    item_tile = jnp.asarray([t for t, _ in past], jnp.int32)
    item_chunk = jnp.asarray([c for _, c in past], jnp.int32)
    return pl.pallas_call(
        functools.partial(_moba_kernel, n_chunks),
        grid_spec=pltpu.PrefetchScalarGridSpec(
            num_scalar_prefetch=2,
            grid=(bsz, pairs),
            in_specs=[pl.BlockSpec((1, 2 * LANES, seq), lambda b, p, *_: (b, p, 0)),
                      pl.BlockSpec((1, seq, 2 * LANES), lambda b, p, *_: (b, 0, p)),
                      pl.BlockSpec((1, 2 * V_ROWS, seq), lambda b, p, *_: (b, p, 0))],
            out_specs=pl.BlockSpec((1, seq, LANES), lambda b, p, *_: (b, 0, p)),
            scratch_shapes=[pltpu.VMEM((2, 2, CHUNK, CHUNK), BF16),
                            pltpu.VMEM((2, 2, 1, CHUNK), F32),
                            pltpu.VMEM((n_chunks, 2, 1, CHUNK), F32),
                            pltpu.VMEM((n_chunks, 2, V_ROWS, CHUNK), F32)]),
        out_shape=jax.ShapeDtypeStruct((bsz, seq, pairs * LANES), BF16),
        compiler_params=pltpu.CompilerParams(
            dimension_semantics=("arbitrary", "arbitrary"),
            vmem_limit_bytes=VMEM_LIMIT),
        name="moba",
    )(item_tile, item_chunk, q_aug, k_aug, v_aug)


def _out_proj_kernel(ya_ref, gz_ref, yc_ref, x_ref, mod_ref, gattn_ref, wout_ref, gfin_ref, o_ref):
    ya = ya_ref[0].astype(F32)
    yn = ya * lax.rsqrt(jnp.mean(ya * ya, axis=-1, keepdims=True) + EPS) * gattn_ref[...]
    yn = (yn * gz_ref[0].astype(F32)).astype(BF16)
    y = (jnp.dot(yn, wout_ref[0:D_ATTN, :], preferred_element_type=F32)
         + jnp.dot(yc_ref[0], wout_ref[D_ATTN:, :], preferred_element_type=F32))
    xo = x_ref[0] + mod_ref[0, 2:3, :] * y
    o_ref[0] = xo * lax.rsqrt(jnp.mean(xo * xo, axis=-1, keepdims=True) + EPS) * gfin_ref[...]


def _out_proj(y_attn, gz, yc, x, mod3, g_attn_out, w_out, g_final):
    bsz, seq, _ = x.shape
    row_spec = lambda width: pl.BlockSpec((1, OUT_TILE, width), lambda b, i: (b, i, 0))
    const = lambda shape: pl.BlockSpec(shape, lambda b, i: (0,) * len(shape))
    return pl.pallas_call(
        _out_proj_kernel,
        grid=(bsz, seq // OUT_TILE),
        in_specs=[row_spec(D_ATTN), row_spec(D_ATTN), row_spec(D_CONV), row_spec(D_MODEL),
                  pl.BlockSpec((1, 3, D_MODEL), lambda b, i: (b, 0, 0)),
                  const((1, D_ATTN)), const(w_out.shape), const((1, D_MODEL))],
        out_specs=row_spec(D_MODEL),
        out_shape=jax.ShapeDtypeStruct((bsz, seq, D_MODEL), F32),
        compiler_params=pltpu.CompilerParams(dimension_semantics=("arbitrary", "arbitrary"),
                                             vmem_limit_bytes=VMEM_LIMIT),
        name="out_proj",
    )(y_attn, gz, yc, x, mod3, g_attn_out.reshape(1, D_ATTN), w_out, g_final.reshape(1, D_MODEL))


def kernel(x, c, positions, w_ada, b_ada, g_norm, w_in, w_conv, g_attn_out, g_conv_out, w_out, g_final):
    bsz, seq, _ = x.shape
    assert seq % (TILES_PER_STEP * TILE) == 0 and seq % OUT_TILE == 0 and seq // TILE <= HEAD_DIM // 2
    mod3 = _adaln(c, w_ada, b_ada).reshape(bsz, 3, D_MODEL)

    inv_freq = ROPE_THETA ** (-jnp.arange(HALF, dtype=F32) / HALF)
    inv_freq = jnp.tile(inv_freq, LANES // HALF).reshape(1, LANES)

    q_aug, k_aug, v_aug, gz, yc = _in_proj(x, mod3, g_norm, positions, inv_freq,
                                           w_in.astype(BF16), w_conv, g_conv_out)
    y_attn = _moba(q_aug, k_aug, v_aug)
    return _out_proj(y_attn, gz, yc, x, mod3, g_attn_out, w_out.astype(BF16), g_final)
```

```python
import functools

import jax
import jax.numpy as jnp
from jax import lax
from jax.experimental import pallas as pl
from jax.experimental.pallas import tpu as pltpu

D_MODEL = 1024
D_ATTN = 512
D_CONV = 512
N_HEADS = 8
HEAD_DIM = 64
HALF = HEAD_DIM // 2
CONV_WIDTH = 3
MOBA_BLOCK = 256
MOBA_TOPK = 3
ROPE_THETA = 10000.0
EPS = 1e-6
NEG = -1e30
LOG2E = 1.4426950408889634

LANES = 128
V_ROWS = HEAD_DIM + 16
TILE = MOBA_BLOCK
CHUNK = 2 * TILE
TILES_PER_STEP = 2
MXU_COLS = 256
POS_PER_ROW = LANES // HALF
OWN_PAIRS_PER_TRIP = 7
PAST_PAIRS_PER_TRIP = 5
OUT_TILE = 1024
VMEM_LIMIT = 56 * 1024 * 1024

F32 = jnp.float32
BF16 = jnp.bfloat16


def _silu(z):
    return z * (1.0 / (1.0 + jnp.exp(-z)))


def _split_bf16(a):
    hi = a.astype(BF16)
    lo = (a - hi.astype(F32)).astype(BF16)
    return hi, lo


def _adaln_kernel(c_ref, w_ref, b_ref, o_ref):
    act = _silu(c_ref[...])
    rows = [jnp.sum(act[:, b:b + 1] * w_ref[...], axis=0, keepdims=True)
            for b in range(act.shape[1])]
    o_ref[:, pl.ds(pl.program_id(0), 1), :] = (jnp.concatenate(rows, axis=0) + b_ref[...])[:, None, :]


def _adaln(c, w_ada, b_ada):
    bsz = c.shape[0]
    n = w_ada.shape[1]
    bn = D_MODEL
    return pl.pallas_call(
        _adaln_kernel,
        grid=(n // bn,),
        in_specs=[pl.BlockSpec((D_MODEL, bsz), lambda j: (0, 0)),
                  pl.BlockSpec((D_MODEL, bn), lambda j: (0, j)),
                  pl.BlockSpec((1, bn), lambda j: (0, j))],
        out_specs=pl.BlockSpec((bsz, n // bn, bn), lambda j: (0, 0, 0)),
        out_shape=jax.ShapeDtypeStruct((bsz, n // bn, bn), F32),
        compiler_params=pltpu.CompilerParams(dimension_semantics=("arbitrary",),
                                             vmem_limit_bytes=VMEM_LIMIT),
        name="adaln",
    )(c.T, w_ada, b_ada.reshape(1, n))


def _in_proj_kernel(n_blocks, n_groups, x_ref, mod_ref, gnorm_ref, pos_ref, freq_ref, *refs):
    w_refs = refs[:n_groups]
    (wconv_ref, gconv_ref, q_ref, k_ref, v_ref, gz_ref, yc_ref,
     kbd_ref, carry_ref, cos_ref, sin_ref, proj_ref, h_ref) = refs[n_groups:]
    step = pl.program_id(1)

    @pl.when(step == 0)
    def _():
        kbd_ref[...] = jnp.zeros_like(kbd_ref)
        carry_ref[...] = jnp.zeros_like(carry_ref)

    n_cols = n_groups * D_ATTN
    col_groups = [(0, 2 * D_ATTN), (2 * D_ATTN, 3 * D_ATTN), (3 * D_ATTN, 4 * D_ATTN),
                  (4 * D_ATTN, n_cols)]

    def norm(r):
        x = x_ref[0, r * TILE:(r + 1) * TILE, :]
        xn = x * lax.rsqrt(jnp.mean(x * x, axis=-1, keepdims=True) + EPS) * gnorm_ref[...]
        h_ref[r] = (xn * (1.0 + mod_ref[0, 1:2, :]) + mod_ref[0, 0:1, :]).astype(BF16)

    def project(r, group):
        lo, hi = col_groups[group]
        for c0 in range(lo, hi, D_ATTN):
            proj_ref[r * TILE:(r + 1) * TILE, c0:c0 + D_ATTN] = jnp.dot(
                h_ref[r], w_refs[c0 // D_ATTN][...], preferred_element_type=F32)

    posts = [_block_posts(n_blocks, r, step * TILES_PER_STEP + r, pos_ref, freq_ref, wconv_ref,
                          gconv_ref, q_ref, k_ref, v_ref, gz_ref, yc_ref, kbd_ref, carry_ref,
                          cos_ref, sin_ref, proj_ref)
             for r in range(TILES_PER_STEP)]
    stages = [(r, group) for r in range(TILES_PER_STEP) for group in range(len(col_groups))]
    norm(0)
    project(*stages[0])
    for s, (r, group) in enumerate(stages):
        if s + 1 < len(stages):
            if stages[s + 1][1] == 0:
                norm(stages[s + 1][0])
            project(*stages[s + 1])
        posts[r][group]()


def _block_posts(n_blocks, r, i, pos_ref, freq_ref, wconv_ref, gconv_ref,
                 q_ref, k_ref, v_ref, gz_ref, yc_ref, kbd_ref, carry_ref, cos_ref, sin_ref,
                 proj_ref):
    rows = slice(r * TILE, (r + 1) * TILE)

    def proj(c0, c1):
        return proj_ref[rows, c0:c1]

    def post_qk():
        packed = TILE // POS_PER_ROW
        lane_c = lax.broadcasted_iota(jnp.int32, (packed, LANES), 1)
        ang = pos_ref[0, r * packed:(r + 1) * packed, :].astype(F32) * freq_ref[...]
        sign = jnp.where((lane_c % HEAD_DIM) < HALF, -1.0, 1.0)
        for table_ref, table, scale in ((cos_ref, jnp.cos(ang), None), (sin_ref, jnp.sin(ang), sign)):
            for a in range(POS_PER_ROW):
                z = jnp.where(lane_c // HALF == a, table, 0.0)
                z = z + pltpu.roll(z, 2 * HALF, 1)
                z = z + pltpu.roll(z, HALF, 1)
                table_ref[r, pl.ds(a, packed, stride=POS_PER_ROW), :] = (
                    z if scale is None else z * scale)
        cos = cos_ref[r]
        sin = sin_ref[r]

        lane = lax.broadcasted_iota(jnp.int32, (TILE, LANES), 1)
        first_half = (lane % HEAD_DIM) < HALF

        def rope(t):
            partner = jnp.where(first_half, pltpu.roll(t, LANES - HALF, 1), pltpu.roll(t, HALF, 1))
            return t * cos + partner * sin

        n_groups = D_ATTN // LANES
        q_groups = [rope(proj(g * LANES, (g + 1) * LANES)) for g in range(n_groups)]
        k_groups = [rope(proj(D_ATTN + g * LANES, D_ATTN + (g + 1) * LANES)) for g in range(n_groups)]
        q = jnp.concatenate(q_groups, axis=1)
        k = jnp.concatenate(k_groups, axis=1)

        nt = (((1,), (1,)), ((), ()))
        scores = lax.dot_general(kbd_ref[...].astype(BF16), q.astype(BF16), nt,
                                 preferred_element_type=F32)
        g3 = scores.reshape(N_HEADS, n_blocks, TILE)
        blk = lax.broadcasted_iota(jnp.int32, g3.shape, 1)
        past = blk < i
        g3 = jnp.where(past, g3, NEG)
        picked = jnp.zeros(g3.shape, jnp.bool_)
        for _ in range(min(MOBA_TOPK, n_blocks - 1)):
            top = jnp.max(g3, axis=1, keepdims=True)
            first = jnp.min(jnp.where(g3 == top, blk, n_blocks), axis=1, keepdims=True)
            hit = blk == first
            picked = jnp.logical_or(picked, hit)
            g3 = jnp.where(hit, -jnp.inf, g3)
        keep = jnp.logical_or(jnp.logical_and(picked, past), blk == i)
        bias = jnp.where(keep, 0.0, NEG)

        k_mean = jnp.mean(k, axis=0, keepdims=True)
        lane_w = lax.broadcasted_iota(jnp.int32, (1, D_ATTN), 1)
        for hd in range(N_HEADS):
            own = (lane_w // HEAD_DIM) == hd
            kbd_ref[pl.ds(hd * n_blocks + i, 1), :] = jnp.where(own, k_mean, 0.0)

        q_t = (q * (HEAD_DIM ** -0.5 * LOG2E)).T
        pad = jnp.zeros((HEAD_DIM - n_blocks, TILE), F32)
        parts = []
        for hd in range(N_HEADS):
            dims = q_t[hd * HEAD_DIM:(hd + 1) * HEAD_DIM]
            parts += [dims, bias[hd], pad] if hd % 2 == 0 else [bias[hd], pad, dims]
        q_ref[0, :, rows] = jnp.concatenate(parts, axis=0).astype(BF16)

        low = lane < HEAD_DIM
        onehot_even = jnp.where(lane == HEAD_DIM + i, 1.0, 0.0)
        onehot_odd = jnp.where(lane == i, 1.0, 0.0)
        for g in range(n_groups):
            even = slice(2 * g * LANES, (2 * g + 1) * LANES)
            odd = slice((2 * g + 1) * LANES, (2 * g + 2) * LANES)
            k_ref[0, rows, even] = jnp.where(low, k_groups[g], onehot_even).astype(BF16)
            k_ref[0, rows, odd] = jnp.where(low, onehot_odd, k_groups[g]).astype(BF16)

    def post_v():
        v_t = proj(2 * D_ATTN, 3 * D_ATTN).T
        ones = jnp.ones((V_ROWS - HEAD_DIM, TILE), F32)
        parts = []
        for hd in range(N_HEADS):
            parts += [v_t[hd * HEAD_DIM:(hd + 1) * HEAD_DIM], ones]
        v_ref[0, :, rows] = jnp.concatenate(parts, axis=0).astype(BF16)

    def post_gz():
        gz_ref[0, rows, :] = _silu(proj(3 * D_ATTN, 4 * D_ATTN)).astype(BF16)

    def post_conv():
        c0 = 4 * D_ATTN
        b_g = proj(c0, c0 + D_CONV)
        u = proj(c0 + D_CONV, c0 + 2 * D_CONV) * proj(c0 + 2 * D_CONV, c0 + 3 * D_CONV)
        z_c = proj(c0 + 3 * D_CONV, c0 + 4 * D_CONV)
        row = lax.broadcasted_iota(jnp.int32, u.shape, 0)
        prev1 = carry_ref[7:8, :]
        prev2 = carry_ref[6:7, :]
        u1 = jnp.where(row == 0, prev1, pltpu.roll(u, 1, 0))
        u2 = jnp.where(row == 0, prev2, jnp.where(row == 1, prev1, pltpu.roll(u, 2, 0)))
        carry_ref[...] = u[TILE - 8:TILE, :]
        y = b_g * (wconv_ref[2:3, :] * u + wconv_ref[1:2, :] * u1 + wconv_ref[0:1, :] * u2)
        y = y * lax.rsqrt(jnp.mean(y * y, axis=-1, keepdims=True) + EPS) * gconv_ref[...]
        yc_ref[0, rows, :] = (y * _silu(z_c)).astype(BF16)

    return [post_qk, post_v, post_gz, post_conv]


def _in_proj(x, mod3, g_norm, positions, inv_freq, w_hi, w_conv, g_conv_out):
    bsz, seq, _ = x.shape
    n_blocks = seq // TILE
    step_rows = TILES_PER_STEP * TILE
    wide = N_HEADS * LANES
    row_spec = lambda width: pl.BlockSpec((1, step_rows, width), lambda b, i: (b, i, 0))
    col_spec = lambda height: pl.BlockSpec((1, height, step_rows), lambda b, i: (b, 0, i))
    const = lambda shape: pl.BlockSpec(shape, lambda b, i: (0,) * len(shape))
    kbd_rows = N_HEADS * n_blocks
    n_groups = w_hi.shape[1] // D_ATTN
    pos_compact = jnp.repeat(positions.reshape(bsz, seq // POS_PER_ROW, POS_PER_ROW), HALF, axis=2)
    return pl.pallas_call(
        functools.partial(_in_proj_kernel, n_blocks, n_groups),
        grid=(bsz, n_blocks // TILES_PER_STEP),
        in_specs=[row_spec(D_MODEL),
                  pl.BlockSpec((1, 3, D_MODEL), lambda b, i: (b, 0, 0)),
                  const((1, D_MODEL)),
                  pl.BlockSpec((1, step_rows // POS_PER_ROW, LANES), lambda b, i: (b, i, 0)),
                  const((1, LANES)),
                  *[pl.BlockSpec((D_MODEL, D_ATTN), lambda b, i, g=g: (0, g))
                    for g in range(n_groups)],
                  const((CONV_WIDTH, D_CONV)),
                  const((1, D_CONV))],
        out_specs=[col_spec(wide), row_spec(wide), col_spec(N_HEADS * V_ROWS),
                   row_spec(D_ATTN), row_spec(D_CONV)],
        out_shape=[jax.ShapeDtypeStruct((bsz, wide, seq), BF16),
                   jax.ShapeDtypeStruct((bsz, seq, wide), BF16),
                   jax.ShapeDtypeStruct((bsz, N_HEADS * V_ROWS, seq), BF16),
                   jax.ShapeDtypeStruct((bsz, seq, D_ATTN), BF16),
                   jax.ShapeDtypeStruct((bsz, seq, D_CONV), BF16)],
        scratch_shapes=[pltpu.VMEM((kbd_rows, D_ATTN), F32),
                        pltpu.VMEM((8, D_CONV), F32),
                        pltpu.VMEM((TILES_PER_STEP, TILE, LANES), F32),
                        pltpu.VMEM((TILES_PER_STEP, TILE, LANES), F32),
                        pltpu.VMEM((step_rows, w_hi.shape[1]), F32),
                        pltpu.VMEM((TILES_PER_STEP, TILE, D_MODEL), BF16)],
        compiler_params=pltpu.CompilerParams(dimension_semantics=("arbitrary", "arbitrary"),
                                             vmem_limit_bytes=VMEM_LIMIT),
        name="in_proj",
    )(x, mod3, g_norm.reshape(1, D_MODEL), pos_compact, inv_freq,
      *([w_hi] * n_groups), w_conv, g_conv_out.reshape(1, D_CONV))


def _moba_kernel(n_chunks, tile_ref, chunk_ref, q_ref, k_ref, v_ref, o_ref, s_ref, smax_ref,
                 m_ref, acc_ref):
    n_past = n_chunks * (n_chunks - 1) // 2

    def own_item(t):
        return t, t

    def past_item(j):
        return tile_ref[j], chunk_ref[j]

    def own_keys(half):
        return (half + 1) * MXU_COLS

    def issue_unit(buf, tile, chunk, hd, half, own):
        n_keys = own_keys(half) if own else CHUNK
        start = pl.multiple_of(chunk * CHUNK, CHUNK)
        q_start = pl.multiple_of(tile * CHUNK + half * MXU_COLS, MXU_COLS)
        rows = slice(hd * LANES, (hd + 1) * LANES)
        cols = slice(half * MXU_COLS, (half + 1) * MXU_COLS)
        kj = k_ref[0, pl.ds(start, n_keys), rows]
        s = jnp.dot(kj, q_ref[0, rows, pl.ds(q_start, MXU_COLS)],
                    preferred_element_type=F32)
        if own:
            diag = s[n_keys - MXU_COLS:]
            key = lax.broadcasted_iota(jnp.int32, diag.shape, 0)
            qry = lax.broadcasted_iota(jnp.int32, diag.shape, 1)
            diag = jnp.where(key <= qry, diag, NEG)
            s = diag if half == 0 else jnp.concatenate([s[:n_keys - MXU_COLS], diag], axis=0)
        s_ref[buf, hd, :n_keys, cols] = s.astype(BF16)
        smax_ref[buf, hd, :, cols] = jnp.max(s, axis=0, keepdims=True)

    def absorb_unit(buf, tile, chunk, hd, half, own):
        n_keys = own_keys(half) if own else CHUNK
        start = pl.multiple_of(chunk * CHUNK, CHUNK)
        cols = slice(half * MXU_COLS, (half + 1) * MXU_COLS)
        m = m_ref[tile, hd, :, cols]
        m_new = jnp.maximum(m, smax_ref[buf, hd, :, cols]).astype(BF16)
        p = jnp.exp2(s_ref[buf, hd, :n_keys, cols] - m_new)
        m_new = m_new.astype(F32)
        vj = v_ref[0, hd * V_ROWS:(hd + 1) * V_ROWS, pl.ds(start, n_keys)]
        acc_ref[tile, hd, :, cols] = (jnp.exp2(m - m_new) * acc_ref[tile, hd, :, cols]
                                      + jnp.dot(vj, p, preferred_element_type=F32))
        m_ref[tile, hd, :, cols] = m_new

    def step(issue=None, absorb=None, issue_own=False, absorb_own=False):
        for hd in range(2):
            for half in range(CHUNK // MXU_COLS):
                if issue is not None:
                    issue_unit(*issue, hd, half, issue_own)
                if absorb is not None:
                    absorb_unit(*absorb, hd, half, absorb_own)

    def run_pairs(item, n_pairs, pairs_per_trip, own):
        def pair(u):
            step(issue=(1, *item(2 * u + 1)), absorb=(0, *item(2 * u)),
                 issue_own=own, absorb_own=own)
            step(issue=(0, *item(2 * u + 2)), absorb=(1, *item(2 * u + 1)),
                 issue_own=own, absorb_own=own)

        def trip(t, carry):
            for u in range(pairs_per_trip):
                pair(pairs_per_trip * t + u)
            return carry

        trips = n_pairs // pairs_per_trip
        lax.fori_loop(0, trips, trip, 0)
        for u in range(trips * pairs_per_trip, n_pairs):
            pair(u)

    m_ref[...] = jnp.full(m_ref.shape, -jnp.inf, F32)
    acc_ref[...] = jnp.zeros(acc_ref.shape, F32)
    assert n_chunks % 2 == 0 and n_past % 2 == 0
    step(issue=(0, *own_item(0)), issue_own=True)
    run_pairs(own_item, (n_chunks - 2) // 2, OWN_PAIRS_PER_TRIP, own=True)
    step(issue=(1, *own_item(n_chunks - 1)), absorb=(0, *own_item(n_chunks - 2)),
         issue_own=True, absorb_own=True)
    step(issue=(0, *past_item(0)), absorb=(1, *own_item(n_chunks - 1)), absorb_own=True)
    run_pairs(past_item, (n_past - 2) // 2, PAST_PAIRS_PER_TRIP, own=False)
    step(issue=(1, *past_item(n_past - 1)), absorb=(0, *past_item(n_past - 2)))
    step(absorb=(1, *past_item(n_past - 1)))

    def finish(tile, carry):
        outs = [acc_ref[tile, hd, :HEAD_DIM] / acc_ref[tile, hd, HEAD_DIM:HEAD_DIM + 1]
                for hd in range(2)]
        start = pl.multiple_of(tile * CHUNK, CHUNK)
        o_ref[0, pl.ds(start, CHUNK), :] = jnp.concatenate(outs, axis=0).T.astype(o_ref.dtype)
        return carry

    lax.fori_loop(0, n_chunks, finish, 0)


def _moba(q_aug, k_aug, v_aug):
    bsz, seq, wide = k_aug.shape
    pairs = wide // (2 * LANES)
    n_chunks = seq // CHUNK
    past = [(t, c) for t in range(n_chunks) for c in range(t)]
    item_tile = jnp.asarray([t for t, _ in past], jnp.int32)
    item_chunk = jnp.asarray([c for _, c in past], jnp.int32)
    return pl.pallas_call(
        functools.partial(_moba_kernel, n_chunks),
        grid_spec=pltpu.PrefetchScalarGridSpec(
            num_scalar_prefetch=2,
            grid=(bsz, pairs),
            in_specs=[pl.BlockSpec((1, 2 * LANES, seq), lambda b, p, *_: (b, p, 0)),
                      pl.BlockSpec((1, seq, 2 * LANES), lambda b, p, *_: (b, 0, p)),
                      pl.BlockSpec((1, 2 * V_ROWS, seq), lambda b, p, *_: (b, p, 0))],
            out_specs=pl.BlockSpec((1, seq, LANES), lambda b, p, *_: (b, 0, p)),
            scratch_shapes=[pltpu.VMEM((2, 2, CHUNK, CHUNK), BF16),
                            pltpu.VMEM((2, 2, 1, CHUNK), F32),
                            pltpu.VMEM((n_chunks, 2, 1, CHUNK), F32),
                            pltpu.VMEM((n_chunks, 2, V_ROWS, CHUNK), F32)]),
        out_shape=jax.ShapeDtypeStruct((bsz, seq, pairs * LANES), BF16),
        compiler_params=pltpu.CompilerParams(
            dimension_semantics=("arbitrary", "arbitrary"),
            vmem_limit_bytes=VMEM_LIMIT),
        name="moba",
    )(item_tile, item_chunk, q_aug, k_aug, v_aug)


def _out_proj_kernel(ya_ref, gz_ref, yc_ref, x_ref, mod_ref, gattn_ref, wout_ref, gfin_ref, o_ref,
                     wbf_ref):
    @pl.when(jnp.logical_and(pl.program_id(0) == 0, pl.program_id(1) == 0))
    def _():
        wbf_ref[...] = wout_ref[...].astype(BF16)

    ya = ya_ref[0].astype(F32)
    yn = ya * lax.rsqrt(jnp.mean(ya * ya, axis=-1, keepdims=True) + EPS) * gattn_ref[...]
    yn = (yn * gz_ref[0].astype(F32)).astype(BF16)
    y = (jnp.dot(yn, wbf_ref[0:D_ATTN, :], preferred_element_type=F32)
         + jnp.dot(yc_ref[0], wbf_ref[D_ATTN:, :], preferred_element_type=F32))
    xo = x_ref[0] + mod_ref[0, 2:3, :] * y
    o_ref[0] = xo * lax.rsqrt(jnp.mean(xo * xo, axis=-1, keepdims=True) + EPS) * gfin_ref[...]


def _out_proj(y_attn, gz, yc, x, mod3, g_attn_out, w_out, g_final):
    bsz, seq, _ = x.shape
    row_spec = lambda width: pl.BlockSpec((1, OUT_TILE, width), lambda b, i: (b, i, 0))
    const = lambda shape: pl.BlockSpec(shape, lambda b, i: (0,) * len(shape))
    return pl.pallas_call(
        _out_proj_kernel,
        grid=(bsz, seq // OUT_TILE),
        in_specs=[row_spec(D_ATTN), row_spec(D_ATTN), row_spec(D_CONV), row_spec(D_MODEL),
                  pl.BlockSpec((1, 3, D_MODEL), lambda b, i: (b, 0, 0)),
                  const((1, D_ATTN)), const(w_out.shape), const((1, D_MODEL))],
        out_specs=row_spec(D_MODEL),
        out_shape=jax.ShapeDtypeStruct((bsz, seq, D_MODEL), F32),
        scratch_shapes=[pltpu.VMEM(w_out.shape, BF16)],
        compiler_params=pltpu.CompilerParams(dimension_semantics=("arbitrary", "arbitrary"),
                                             vmem_limit_bytes=VMEM_LIMIT),
        name="out_proj",
    )(y_attn, gz, yc, x, mod3, g_attn_out.reshape(1, D_ATTN), w_out, g_final.reshape(1, D_MODEL))


def kernel(x, c, positions, w_ada, b_ada, g_norm, w_in, w_conv, g_attn_out, g_conv_out, w_out, g_final):
    bsz, seq, _ = x.shape
    assert seq % (TILES_PER_STEP * TILE) == 0 and seq % OUT_TILE == 0 and seq // TILE <= HEAD_DIM // 2
    mod3 = _adaln(c, w_ada, b_ada)

    inv_freq = ROPE_THETA ** (-jnp.arange(HALF, dtype=F32) / HALF)
    inv_freq = jnp.tile(inv_freq, LANES // HALF).reshape(1, LANES)

    q_aug, k_aug, v_aug, gz, yc = _in_proj(x, mod3, g_norm, positions, inv_freq,
                                           w_in.astype(BF16), w_conv, g_conv_out)
    y_attn = _moba(q_aug, k_aug, v_aug)
    return _out_proj(y_attn, gz, yc, x, mod3, g_attn_out, w_out, g_final)
```

```python
import functools

import jax
import jax.numpy as jnp
from jax import lax
from jax.experimental import pallas as pl
from jax.experimental.pallas import tpu as pltpu

D_MODEL = 1024
D_ATTN = 512
D_CONV = 512
N_HEADS = 8
HEAD_DIM = 64
HALF = HEAD_DIM // 2
CONV_WIDTH = 3
MOBA_BLOCK = 256
MOBA_TOPK = 3
ROPE_THETA = 10000.0
EPS = 1e-6
NEG = -1e30
LOG2E = 1.4426950408889634

LANES = 128
V_ROWS = HEAD_DIM + 16
TILE = MOBA_BLOCK
CHUNK = 2 * TILE
TILES_PER_STEP = 2
MXU_COLS = 256
POS_PER_ROW = LANES // HALF
OWN_PAIRS_PER_TRIP = 7
PAST_PAIRS_PER_TRIP = 5
OUT_TILE = 1024
VMEM_LIMIT = 56 * 1024 * 1024

F32 = jnp.float32
BF16 = jnp.bfloat16


def _silu(z):
    return z * (1.0 / (1.0 + jnp.exp(-z)))


def _split_bf16(a):
    hi = a.astype(BF16)
    lo = (a - hi.astype(F32)).astype(BF16)
    return hi, lo


def _adaln_kernel(c_ref, w_ref, b_ref, o_ref):
    act = _silu(c_ref[...])
    rows = [jnp.sum(act[:, b:b + 1] * w_ref[...], axis=0, keepdims=True)
            for b in range(act.shape[1])]
    o_ref[...] = jnp.concatenate(rows, axis=0) + b_ref[...]


def _adaln(c, w_ada, b_ada):
    bsz = c.shape[0]
    n = w_ada.shape[1]
    bn = D_MODEL
    return pl.pallas_call(
        _adaln_kernel,
        grid=(n // bn,),
        in_specs=[pl.BlockSpec((D_MODEL, bsz), lambda j: (0, 0)),
                  pl.BlockSpec((D_MODEL, bn), lambda j: (0, j)),
                  pl.BlockSpec((1, bn), lambda j: (0, j))],
        out_specs=pl.BlockSpec((bsz, bn), lambda j: (0, j)),
        out_shape=jax.ShapeDtypeStruct((bsz, n), F32),
        compiler_params=pltpu.CompilerParams(dimension_semantics=("arbitrary",),
                                             vmem_limit_bytes=VMEM_LIMIT),
        name="adaln",
    )(c.T, w_ada, b_ada.reshape(1, n))


def _in_proj_kernel(n_blocks, n_groups, x_ref, mod_ref, gnorm_ref, pos_ref, freq_ref, *refs):
    w_refs = refs[:n_groups]
    (wconv_ref, gconv_ref, q_ref, k_ref, v_ref, gz_ref, yc_ref,
     kbd_ref, carry_ref, cos_ref, sin_ref, proj_ref, h_ref) = refs[n_groups:]
    step = pl.program_id(1)

    @pl.when(step == 0)
    def _():
        kbd_ref[...] = jnp.zeros_like(kbd_ref)
        carry_ref[...] = jnp.zeros_like(carry_ref)

    n_cols = n_groups * D_ATTN
    col_groups = [(0, 2 * D_ATTN), (2 * D_ATTN, 3 * D_ATTN), (3 * D_ATTN, 4 * D_ATTN),
                  (4 * D_ATTN, n_cols)]

    def norm(r):
        x = x_ref[0, r * TILE:(r + 1) * TILE, :]
        xn = x * lax.rsqrt(jnp.mean(x * x, axis=-1, keepdims=True) + EPS) * gnorm_ref[...]
        h_ref[r] = (xn * (1.0 + mod_ref[0, 1:2, :]) + mod_ref[0, 0:1, :]).astype(BF16)

    def project(r, group):
        lo, hi = col_groups[group]
        for c0 in range(lo, hi, D_ATTN):
            proj_ref[r * TILE:(r + 1) * TILE, c0:c0 + D_ATTN] = jnp.dot(
                h_ref[r], w_refs[c0 // D_ATTN][...], preferred_element_type=F32)

    posts = [_block_posts(n_blocks, r, step * TILES_PER_STEP + r, pos_ref, freq_ref, wconv_ref,
                          gconv_ref, q_ref, k_ref, v_ref, gz_ref, yc_ref, kbd_ref, carry_ref,
                          cos_ref, sin_ref, proj_ref)
             for r in range(TILES_PER_STEP)]
    stages = [(r, group) for r in range(TILES_PER_STEP) for group in (3, 0, 2, 1)]
    norm(0)
    project(*stages[0])
    for s, (r, group) in enumerate(stages):
        if s + 1 < len(stages):
            if stages[s + 1][0] != r:
                norm(stages[s + 1][0])
            project(*stages[s + 1])
        posts[r][group]()


def _block_posts(n_blocks, r, i, pos_ref, freq_ref, wconv_ref, gconv_ref,
                 q_ref, k_ref, v_ref, gz_ref, yc_ref, kbd_ref, carry_ref, cos_ref, sin_ref,
                 proj_ref):
    rows = slice(r * TILE, (r + 1) * TILE)

    def proj(c0, c1):
        return proj_ref[rows, c0:c1]

    def post_qk():
        packed = TILE // POS_PER_ROW
        lane_c = lax.broadcasted_iota(jnp.int32, (packed, LANES), 1)
        ang = pos_ref[0, r * packed:(r + 1) * packed, :].astype(F32) * freq_ref[...]
        sign = jnp.where((lane_c % HEAD_DIM) < HALF, -1.0, 1.0)
        for table_ref, table, scale in ((cos_ref, jnp.cos(ang), None), (sin_ref, jnp.sin(ang), sign)):
            for a in range(POS_PER_ROW):
                z = jnp.where(lane_c // HALF == a, table, 0.0)
                z = z + pltpu.roll(z, 2 * HALF, 1)
                z = z + pltpu.roll(z, HALF, 1)
                table_ref[r, pl.ds(a, packed, stride=POS_PER_ROW), :] = (
                    z if scale is None else z * scale)
        cos = cos_ref[r]
        sin = sin_ref[r]

        lane = lax.broadcasted_iota(jnp.int32, (TILE, LANES), 1)
        first_half = (lane % HEAD_DIM) < HALF

        def rope(t):
            partner = jnp.where(first_half, pltpu.roll(t, LANES - HALF, 1), pltpu.roll(t, HALF, 1))
            return t * cos + partner * sin

        n_groups = D_ATTN // LANES
        q_groups = [rope(proj(g * LANES, (g + 1) * LANES)) for g in range(n_groups)]
        k_groups = [rope(proj(D_ATTN + g * LANES, D_ATTN + (g + 1) * LANES)) for g in range(n_groups)]
        q = jnp.concatenate(q_groups, axis=1)
        k = jnp.concatenate(k_groups, axis=1)

        nt = (((1,), (1,)), ((), ()))
        scores = lax.dot_general(kbd_ref[...].astype(BF16), q.astype(BF16), nt,
                                 preferred_element_type=F32)
        g3 = scores.reshape(N_HEADS, n_blocks, TILE)
        blk = lax.broadcasted_iota(jnp.int32, g3.shape, 1)
        past = blk < i
        g3 = jnp.where(past, g3, NEG)
        picked = jnp.zeros(g3.shape, jnp.bool_)
        for _ in range(min(MOBA_TOPK, n_blocks - 1)):
            top = jnp.max(g3, axis=1, keepdims=True)
            first = jnp.min(jnp.where(g3 == top, blk, n_blocks), axis=1, keepdims=True)
            hit = blk == first
            picked = jnp.logical_or(picked, hit)
            g3 = jnp.where(hit, -jnp.inf, g3)
        keep = jnp.logical_or(jnp.logical_and(picked, past), blk == i)
        bias = jnp.where(keep, 0.0, NEG)

        k_mean = jnp.mean(k, axis=0, keepdims=True)
        lane_w = lax.broadcasted_iota(jnp.int32, (1, D_ATTN), 1)
        for hd in range(N_HEADS):
            own = (lane_w // HEAD_DIM) == hd
            kbd_ref[pl.ds(hd * n_blocks + i, 1), :] = jnp.where(own, k_mean, 0.0)

        q_t = (q * (HEAD_DIM ** -0.5 * LOG2E)).T
        pad = jnp.zeros((HEAD_DIM - n_blocks, TILE), F32)
        parts = []
        for hd in range(N_HEADS):
            dims = q_t[hd * HEAD_DIM:(hd + 1) * HEAD_DIM]
            parts += [dims, bias[hd], pad] if hd % 2 == 0 else [bias[hd], pad, dims]
        q_ref[0, :, rows] = jnp.concatenate(parts, axis=0).astype(BF16)

        low = lane < HEAD_DIM
        onehot_even = jnp.where(lane == HEAD_DIM + i, 1.0, 0.0)
        onehot_odd = jnp.where(lane == i, 1.0, 0.0)
        for g in range(n_groups):
            even = slice(2 * g * LANES, (2 * g + 1) * LANES)
            odd = slice((2 * g + 1) * LANES, (2 * g + 2) * LANES)
            k_ref[0, rows, even] = jnp.where(low, k_groups[g], onehot_even).astype(BF16)
            k_ref[0, rows, odd] = jnp.where(low, onehot_odd, k_groups[g]).astype(BF16)

    def post_v():
        v_t = proj(2 * D_ATTN, 3 * D_ATTN).T
        ones = jnp.ones((V_ROWS - HEAD_DIM, TILE), F32)
        parts = []
        for hd in range(N_HEADS):
            parts += [v_t[hd * HEAD_DIM:(hd + 1) * HEAD_DIM], ones]
        v_ref[0, :, rows] = jnp.concatenate(parts, axis=0).astype(BF16)

    def post_gz():
        gz_ref[0, rows, :] = _silu(proj(3 * D_ATTN, 4 * D_ATTN)).astype(BF16)

    def post_conv():
        c0 = 4 * D_ATTN
        b_g = proj(c0, c0 + D_CONV)
        u = proj(c0 + D_CONV, c0 + 2 * D_CONV) * proj(c0 + 2 * D_CONV, c0 + 3 * D_CONV)
        z_c = proj(c0 + 3 * D_CONV, c0 + 4 * D_CONV)
        row = lax.broadcasted_iota(jnp.int32, u.shape, 0)
        prev1 = carry_ref[7:8, :]
        prev2 = carry_ref[6:7, :]
        u1 = jnp.where(row == 0, prev1, pltpu.roll(u, 1, 0))
        u2 = jnp.where(row == 0, prev2, jnp.where(row == 1, prev1, pltpu.roll(u, 2, 0)))
        carry_ref[...] = u[TILE - 8:TILE, :]
        y = b_g * (wconv_ref[2:3, :] * u + wconv_ref[1:2, :] * u1 + wconv_ref[0:1, :] * u2)
        y = y * lax.rsqrt(jnp.mean(y * y, axis=-1, keepdims=True) + EPS) * gconv_ref[...]
        yc_ref[0, rows, :] = (y * _silu(z_c)).astype(BF16)

    return [post_qk, post_v, post_gz, post_conv]


def _in_proj(x, mod3, g_norm, positions, inv_freq, w_hi, w_conv, g_conv_out):
    bsz, seq, _ = x.shape
    n_blocks = seq // TILE
    step_rows = TILES_PER_STEP * TILE
    wide = N_HEADS * LANES
    row_spec = lambda width: pl.BlockSpec((1, step_rows, width), lambda b, i: (b, i, 0))
    col_spec = lambda height: pl.BlockSpec((1, height, step_rows), lambda b, i: (b, 0, i))
    const = lambda shape: pl.BlockSpec(shape, lambda b, i: (0,) * len(shape))
    kbd_rows = N_HEADS * n_blocks
    n_groups = w_hi.shape[1] // D_ATTN
    pos_compact = jnp.repeat(positions.reshape(bsz, seq // POS_PER_ROW, POS_PER_ROW), HALF, axis=2)
    return pl.pallas_call(
        functools.partial(_in_proj_kernel, n_blocks, n_groups),
        grid=(bsz, n_blocks // TILES_PER_STEP),
        in_specs=[row_spec(D_MODEL),
                  pl.BlockSpec((1, 3, D_MODEL), lambda b, i: (b, 0, 0)),
                  const((1, D_MODEL)),
                  pl.BlockSpec((1, step_rows // POS_PER_ROW, LANES), lambda b, i: (b, i, 0)),
                  const((1, LANES)),
                  *[pl.BlockSpec((D_MODEL, D_ATTN), lambda b, i, g=g: (0, g))
                    for g in range(n_groups)],
                  const((CONV_WIDTH, D_CONV)),
                  const((1, D_CONV))],
        out_specs=[col_spec(wide), row_spec(wide), col_spec(N_HEADS * V_ROWS),
                   row_spec(D_ATTN), row_spec(D_CONV)],
        out_shape=[jax.ShapeDtypeStruct((bsz, wide, seq), BF16),
                   jax.ShapeDtypeStruct((bsz, seq, wide), BF16),
                   jax.ShapeDtypeStruct((bsz, N_HEADS * V_ROWS, seq), BF16),
                   jax.ShapeDtypeStruct((bsz, seq, D_ATTN), BF16),
                   jax.ShapeDtypeStruct((bsz, seq, D_CONV), BF16)],
        scratch_shapes=[pltpu.VMEM((kbd_rows, D_ATTN), F32),
                        pltpu.VMEM((8, D_CONV), F32),
                        pltpu.VMEM((TILES_PER_STEP, TILE, LANES), F32),
                        pltpu.VMEM((TILES_PER_STEP, TILE, LANES), F32),
                        pltpu.VMEM((step_rows, w_hi.shape[1]), F32),
                        pltpu.VMEM((TILES_PER_STEP, TILE, D_MODEL), BF16)],
        compiler_params=pltpu.CompilerParams(dimension_semantics=("arbitrary", "arbitrary"),
                                             vmem_limit_bytes=VMEM_LIMIT),
        name="in_proj",
    )(x, mod3, g_norm.reshape(1, D_MODEL), pos_compact, inv_freq,
      *([w_hi] * n_groups), w_conv, g_conv_out.reshape(1, D_CONV))


def _moba_kernel(n_chunks, tile_ref, chunk_ref, q_ref, k_ref, v_ref, o_ref, s_ref, smax_ref,
                 m_ref, acc_ref):
    n_past = n_chunks * (n_chunks - 1) // 2

    def own_item(t):
        return t, t

    def past_item(j):
        return tile_ref[j], chunk_ref[j]

    def own_keys(half):
        return (half + 1) * MXU_COLS

    def issue_unit(buf, tile, chunk, hd, half, own):
        n_keys = own_keys(half) if own else CHUNK
        start = pl.multiple_of(chunk * CHUNK, CHUNK)
        q_start = pl.multiple_of(tile * CHUNK + half * MXU_COLS, MXU_COLS)
        rows = slice(hd * LANES, (hd + 1) * LANES)
        cols = slice(half * MXU_COLS, (half + 1) * MXU_COLS)
        kj = k_ref[0, pl.ds(start, n_keys), rows]
        s = jnp.dot(kj, q_ref[0, rows, pl.ds(q_start, MXU_COLS)],
                    preferred_element_type=F32)
        s_ref[buf, hd, :n_keys, cols] = s.astype(BF16)
        if not own:
            smax_ref[buf, hd, :, cols] = jnp.max(s, axis=0, keepdims=True)

    def absorb_unit(buf, tile, chunk, hd, half, own):
        n_keys = own_keys(half) if own else CHUNK
        start = pl.multiple_of(chunk * CHUNK, CHUNK)
        cols = slice(half * MXU_COLS, (half + 1) * MXU_COLS)
        m = m_ref[tile, hd, :, cols]
        if own:
            diag = s_ref[buf, hd, n_keys - MXU_COLS:n_keys, cols].astype(F32)
            key = lax.broadcasted_iota(jnp.int32, diag.shape, 0)
            qry = lax.broadcasted_iota(jnp.int32, diag.shape, 1)
            diag = jnp.where(key <= qry, diag, NEG)
            blocks = [s_ref[buf, hd, b * MXU_COLS:(b + 1) * MXU_COLS, cols] for b in range(half)]
            blocks.append(diag.astype(BF16))
            smax = jnp.max(diag, axis=0, keepdims=True)
            for block in blocks[:-1]:
                smax = jnp.maximum(smax, jnp.max(block, axis=0, keepdims=True).astype(F32))
        else:
            blocks = [s_ref[buf, hd, :, cols]]
            smax = smax_ref[buf, hd, :, cols]
        m_new = jnp.maximum(m, smax).astype(BF16)
        p = [jnp.exp2(block - m_new) for block in blocks]
        p = p[0] if len(p) == 1 else jnp.concatenate(p, axis=0)
        m_new = m_new.astype(F32)
        vj = v_ref[0, hd * V_ROWS:(hd + 1) * V_ROWS, pl.ds(start, n_keys)]
        acc_ref[tile, hd, :, cols] = (jnp.exp2(m - m_new) * acc_ref[tile, hd, :, cols]
                                      + jnp.dot(vj, p, preferred_element_type=F32))
        m_ref[tile, hd, :, cols] = m_new

    def step(issue=None, absorb=None, issue_own=False, absorb_own=False):
        for hd in range(2):
            for half in range(CHUNK // MXU_COLS):
                if issue is not None:
                    issue_unit(*issue, hd, half, issue_own)
                if absorb is not None:
                    absorb_unit(*absorb, hd, half, absorb_own)

    def run_pairs(item, n_pairs, pairs_per_trip, own):
        def pair(u):
            step(issue=(1, *item(2 * u + 1)), absorb=(0, *item(2 * u)),
                 issue_own=own, absorb_own=own)
            step(issue=(0, *item(2 * u + 2)), absorb=(1, *item(2 * u + 1)),
                 issue_own=own, absorb_own=own)

        def trip(t, carry):
            for u in range(pairs_per_trip):
                pair(pairs_per_trip * t + u)
            return carry

        trips = n_pairs // pairs_per_trip
        lax.fori_loop(0, trips, trip, 0)
        for u in range(trips * pairs_per_trip, n_pairs):
            pair(u)

    m_ref[...] = jnp.full(m_ref.shape, -jnp.inf, F32)
    acc_ref[...] = jnp.zeros(acc_ref.shape, F32)
    assert n_chunks % 2 == 0 and n_past % 2 == 0
    step(issue=(0, *own_item(0)), issue_own=True)
    run_pairs(own_item, (n_chunks - 2) // 2, OWN_PAIRS_PER_TRIP, own=True)
    step(issue=(1, *own_item(n_chunks - 1)), absorb=(0, *own_item(n_chunks - 2)),
         issue_own=True, absorb_own=True)
    step(issue=(0, *past_item(0)), absorb=(1, *own_item(n_chunks - 1)), absorb_own=True)
    run_pairs(past_item, (n_past - 2) // 2, PAST_PAIRS_PER_TRIP, own=False)
    step(issue=(1, *past_item(n_past - 1)), absorb=(0, *past_item(n_past - 2)))
    step(absorb=(1, *past_item(n_past - 1)))

    def finish(tile, carry):
        outs = [acc_ref[tile, hd, :HEAD_DIM] / acc_ref[tile, hd, HEAD_DIM:HEAD_DIM + 1]
                for hd in range(2)]
        start = pl.multiple_of(tile * CHUNK, CHUNK)
        o_ref[0, pl.ds(start, CHUNK), :] = jnp.concatenate(outs, axis=0).T.astype(o_ref.dtype)
        return carry

    lax.fori_loop(0, n_chunks, finish, 0)


def _moba(q_aug, k_aug, v_aug):
    bsz, seq, wide = k_aug.shape
    pairs = wide // (2 * LANES)
    n_chunks = seq // CHUNK
    past = [(t, c) for t in range(n_chunks) for c in range(t)]
    item_tile = jnp.asarray([t for t, _ in past], jnp.int32)
    item_chunk = jnp.asarray([c for _, c in past], jnp.int32)
    return pl.pallas_call(
        functools.partial(_moba_kernel, n_chunks),
        grid_spec=pltpu.PrefetchScalarGridSpec(
            num_scalar_prefetch=2,
            grid=(bsz, pairs),
            in_specs=[pl.BlockSpec((1, 2 * LANES, seq), lambda b, p, *_: (b, p, 0)),
                      pl.BlockSpec((1, seq, 2 * LANES), lambda b, p, *_: (b, 0, p)),
                      pl.BlockSpec((1, 2 * V_ROWS, seq), lambda b, p, *_: (b, p, 0))],
            out_specs=pl.BlockSpec((1, seq, LANES), lambda b, p, *_: (b, 0, p)),
            scratch_shapes=[pltpu.VMEM((2, 2, CHUNK, CHUNK), BF16),
                            pltpu.VMEM((2, 2, 1, CHUNK), F32),
                            pltpu.VMEM((n_chunks, 2, 1, CHUNK), F32),
                            pltpu.VMEM((n_chunks, 2, V_ROWS, CHUNK), F32)]),
        out_shape=jax.ShapeDtypeStruct((bsz, seq, pairs * LANES), BF16),
        compiler_params=pltpu.CompilerParams(
            dimension_semantics=("arbitrary", "arbitrary"),
            vmem_limit_bytes=VMEM_LIMIT),
        name="moba",
    )(item_tile, item_chunk, q_aug, k_aug, v_aug)


def _out_proj_kernel(ya_ref, gz_ref, yc_ref, x_ref, mod_ref, gattn_ref, wout_ref, gfin_ref, o_ref):
    ya = ya_ref[0].astype(F32)
    yn = ya * lax.rsqrt(jnp.mean(ya * ya, axis=-1, keepdims=True) + EPS) * gattn_ref[...]
    yn = (yn * gz_ref[0].astype(F32)).astype(BF16)
    y = (jnp.dot(yn, wout_ref[0:D_ATTN, :], preferred_element_type=F32)
         + jnp.dot(yc_ref[0], wout_ref[D_ATTN:, :], preferred_element_type=F32))
    xo = x_ref[0] + mod_ref[0, 2:3, :] * y
    o_ref[0] = xo * lax.rsqrt(jnp.mean(xo * xo, axis=-1, keepdims=True) + EPS) * gfin_ref[...]


def _out_proj(y_attn, gz, yc, x, mod3, g_attn_out, w_out, g_final):
    bsz, seq, _ = x.shape
    row_spec = lambda width: pl.BlockSpec((1, OUT_TILE, width), lambda b, i: (b, i, 0))
    const = lambda shape: pl.BlockSpec(shape, lambda b, i: (0,) * len(shape))
    return pl.pallas_call(
        _out_proj_kernel,
        grid=(bsz, seq // OUT_TILE),
        in_specs=[row_spec(D_ATTN), row_spec(D_ATTN), row_spec(D_CONV), row_spec(D_MODEL),
                  pl.BlockSpec((1, 3, D_MODEL), lambda b, i: (b, 0, 0)),
                  const((1, D_ATTN)), const(w_out.shape), const((1, D_MODEL))],
        out_specs=row_spec(D_MODEL),
        out_shape=jax.ShapeDtypeStruct((bsz, seq, D_MODEL), F32),
        compiler_params=pltpu.CompilerParams(dimension_semantics=("arbitrary", "arbitrary"),
                                             vmem_limit_bytes=VMEM_LIMIT),
        name="out_proj",
    )(y_attn, gz, yc, x, mod3, g_attn_out.reshape(1, D_ATTN), w_out, g_final.reshape(1, D_MODEL))


def kernel(x, c, positions, w_ada, b_ada, g_norm, w_in, w_conv, g_attn_out, g_conv_out, w_out, g_final):
    bsz, seq, _ = x.shape
    assert seq % (TILES_PER_STEP * TILE) == 0 and seq % OUT_TILE == 0 and seq // TILE <= HEAD_DIM // 2
    mod3 = _adaln(c, w_ada, b_ada).reshape(bsz, 3, D_MODEL)

    inv_freq = ROPE_THETA ** (-jnp.arange(HALF, dtype=F32) / HALF)
    inv_freq = jnp.tile(inv_freq, LANES // HALF).reshape(1, LANES)

    q_aug, k_aug, v_aug, gz, yc = _in_proj(x, mod3, g_norm, positions, inv_freq,
                                           w_in.astype(BF16), w_conv, g_conv_out)
    y_attn = _moba(q_aug, k_aug, v_aug)
    return _out_proj(y_attn, gz, yc, x, mod3, g_attn_out, w_out.astype(BF16), g_final)
```

```python
import functools

import jax
import jax.numpy as jnp
from jax import lax
from jax.experimental import pallas as pl
from jax.experimental.pallas import tpu as pltpu

D_MODEL = 1024
D_ATTN = 512
D_CONV = 512
N_HEADS = 8
HEAD_DIM = 64
HALF = HEAD_DIM // 2
CONV_WIDTH = 3
MOBA_BLOCK = 256
MOBA_TOPK = 3
ROPE_THETA = 10000.0
EPS = 1e-6
NEG = -1e30
LOG2E = 1.4426950408889634

LANES = 128
V_ROWS = HEAD_DIM + 16
TILE = MOBA_BLOCK
CHUNK = 2 * TILE
TILES_PER_STEP = 2
MXU_COLS = 256
POS_PER_ROW = LANES // HALF
OWN_PAIRS_PER_TRIP = 7
PAST_PAIRS_PER_TRIP = 5
OUT_TILE = 1024
VMEM_LIMIT = 56 * 1024 * 1024

F32 = jnp.float32
BF16 = jnp.bfloat16


def _silu(z):
    return z * (1.0 / (1.0 + jnp.exp(-z)))


def _split_bf16(a):
    hi = a.astype(BF16)
    lo = (a - hi.astype(F32)).astype(BF16)
    return hi, lo


def _adaln_kernel(c_ref, w_ref, b_ref, o_ref):
    act = _silu(c_ref[...])
    rows = [jnp.sum(act[:, b:b + 1] * w_ref[...], axis=0, keepdims=True)
            for b in range(act.shape[1])]
    o_ref[...] = jnp.concatenate(rows, axis=0) + b_ref[...]


def _adaln(c, w_ada, b_ada):
    bsz = c.shape[0]
    n = w_ada.shape[1]
    bn = D_MODEL
    return pl.pallas_call(
        _adaln_kernel,
        grid=(n // bn,),
        in_specs=[pl.BlockSpec((D_MODEL, bsz), lambda j: (0, 0)),
                  pl.BlockSpec((D_MODEL, bn), lambda j: (0, j)),
                  pl.BlockSpec((1, bn), lambda j: (0, j))],
        out_specs=pl.BlockSpec((bsz, bn), lambda j: (0, j)),
        out_shape=jax.ShapeDtypeStruct((bsz, n), F32),
        compiler_params=pltpu.CompilerParams(dimension_semantics=("arbitrary",),
                                             vmem_limit_bytes=VMEM_LIMIT),
        name="adaln",
    )(c.T, w_ada, b_ada.reshape(1, n))


def _in_proj_kernel(n_blocks, n_groups, x_ref, mod_ref, gnorm_ref, pos_ref, freq_ref, *refs):
    w_refs = refs[:n_groups]
    (wconv_ref, gconv_ref, q_ref, k_ref, v_ref, gz_ref, yc_ref,
     kbd_ref, carry_ref, cos_ref, sin_ref, proj_ref, h_ref) = refs[n_groups:]
    step = pl.program_id(1)

    @pl.when(step == 0)
    def _():
        kbd_ref[...] = jnp.zeros_like(kbd_ref)
        carry_ref[...] = jnp.zeros_like(carry_ref)

    n_cols = n_groups * D_ATTN
    col_groups = [(0, 2 * D_ATTN), (2 * D_ATTN, 3 * D_ATTN), (3 * D_ATTN, 4 * D_ATTN),
                  (4 * D_ATTN, n_cols)]

    def norm(r):
        x = x_ref[0, r * TILE:(r + 1) * TILE, :]
        xn = x * lax.rsqrt(jnp.mean(x * x, axis=-1, keepdims=True) + EPS) * gnorm_ref[...]
        h_ref[r] = (xn * (1.0 + mod_ref[0, 1:2, :]) + mod_ref[0, 0:1, :]).astype(BF16)

    def project(r, group):
        lo, hi = col_groups[group]
        for c0 in range(lo, hi, D_ATTN):
            proj_ref[r * TILE:(r + 1) * TILE, c0:c0 + D_ATTN] = jnp.dot(
                h_ref[r], w_refs[c0 // D_ATTN][...], preferred_element_type=F32)

    posts = [_block_posts(n_blocks, r, step * TILES_PER_STEP + r, pos_ref, freq_ref, wconv_ref,
                          gconv_ref, q_ref, k_ref, v_ref, gz_ref, yc_ref, kbd_ref, carry_ref,
                          cos_ref, sin_ref, proj_ref)
             for r in range(TILES_PER_STEP)]
    stages = [(r, group) for r in range(TILES_PER_STEP) for group in (3, 0, 2, 1)]
    norm(0)
    project(*stages[0])
    for s, (r, group) in enumerate(stages):
        if s + 1 < len(stages):
            if stages[s + 1][0] != r:
                norm(stages[s + 1][0])
            project(*stages[s + 1])
        posts[r][group]()


def _block_posts(n_blocks, r, i, pos_ref, freq_ref, wconv_ref, gconv_ref,
                 q_ref, k_ref, v_ref, gz_ref, yc_ref, kbd_ref, carry_ref, cos_ref, sin_ref,
                 proj_ref):
    rows = slice(r * TILE, (r + 1) * TILE)

    def proj(c0, c1):
        return proj_ref[rows, c0:c1]

    def post_qk():
        packed = TILE // POS_PER_ROW
        lane_c = lax.broadcasted_iota(jnp.int32, (packed, LANES), 1)
        ang = pos_ref[0, r * packed:(r + 1) * packed, :].astype(F32) * freq_ref[...]
        sign = jnp.where((lane_c % HEAD_DIM) < HALF, -1.0, 1.0)
        for table_ref, table, scale in ((cos_ref, jnp.cos(ang), None), (sin_ref, jnp.sin(ang), sign)):
            for a in range(POS_PER_ROW):
                z = jnp.where(lane_c // HALF == a, table, 0.0)
                z = z + pltpu.roll(z, 2 * HALF, 1)
                z = z + pltpu.roll(z, HALF, 1)
                table_ref[r, pl.ds(a, packed, stride=POS_PER_ROW), :] = (
                    z if scale is None else z * scale)
        cos = cos_ref[r]
        sin = sin_ref[r]

        lane = lax.broadcasted_iota(jnp.int32, (TILE, LANES), 1)
        first_half = (lane % HEAD_DIM) < HALF

        def rope(t):
            partner = jnp.where(first_half, pltpu.roll(t, LANES - HALF, 1), pltpu.roll(t, HALF, 1))
            return t * cos + partner * sin

        n_groups = D_ATTN // LANES
        q_groups = [rope(proj(g * LANES, (g + 1) * LANES)) for g in range(n_groups)]
        k_groups = [rope(proj(D_ATTN + g * LANES, D_ATTN + (g + 1) * LANES)) for g in range(n_groups)]
        q = jnp.concatenate(q_groups, axis=1)
        k = jnp.concatenate(k_groups, axis=1)

        nt = (((1,), (1,)), ((), ()))
        scores = lax.dot_general(kbd_ref[...].astype(BF16), q.astype(BF16), nt,
                                 preferred_element_type=F32)
        g3 = scores.reshape(N_HEADS, n_blocks, TILE)
        blk = lax.broadcasted_iota(jnp.int32, g3.shape, 1)
        past = blk < i
        g3 = jnp.where(past, g3, NEG)
        picked = jnp.zeros(g3.shape, jnp.bool_)
        for _ in range(min(MOBA_TOPK, n_blocks - 1)):
            top = jnp.max(g3, axis=1, keepdims=True)
            first = jnp.min(jnp.where(g3 == top, blk, n_blocks), axis=1, keepdims=True)
            hit = blk == first
            picked = jnp.logical_or(picked, hit)
            g3 = jnp.where(hit, -jnp.inf, g3)
        keep = jnp.logical_or(jnp.logical_and(picked, past), blk == i)
        bias = jnp.where(keep, 0.0, NEG)

        k_mean = jnp.mean(k, axis=0, keepdims=True)
        lane_w = lax.broadcasted_iota(jnp.int32, (1, D_ATTN), 1)
        for hd in range(N_HEADS):
            own = (lane_w // HEAD_DIM) == hd
            kbd_ref[pl.ds(hd * n_blocks + i, 1), :] = jnp.where(own, k_mean, 0.0)

        q_t = (q * (HEAD_DIM ** -0.5 * LOG2E)).T
        pad = jnp.zeros((HEAD_DIM - n_blocks, TILE), F32)
        parts = []
        for hd in range(N_HEADS):
            dims = q_t[hd * HEAD_DIM:(hd + 1) * HEAD_DIM]
            parts += [dims, bias[hd], pad] if hd % 2 == 0 else [bias[hd], pad, dims]
        q_ref[0, :, rows] = jnp.concatenate(parts, axis=0).astype(BF16)

        low = lane < HEAD_DIM
        onehot_even = jnp.where(lane == HEAD_DIM + i, 1.0, 0.0)
        onehot_odd = jnp.where(lane == i, 1.0, 0.0)
        for g in range(n_groups):
            even = slice(2 * g * LANES, (2 * g + 1) * LANES)
            odd = slice((2 * g + 1) * LANES, (2 * g + 2) * LANES)
            k_ref[0, rows, even] = jnp.where(low, k_groups[g], onehot_even).astype(BF16)
            k_ref[0, rows, odd] = jnp.where(low, onehot_odd, k_groups[g]).astype(BF16)

    def post_v():
        v_t = proj(2 * D_ATTN, 3 * D_ATTN).T
        ones = jnp.ones((V_ROWS - HEAD_DIM, TILE), F32)
        parts = []
        for hd in range(N_HEADS):
            parts += [v_t[hd * HEAD_DIM:(hd + 1) * HEAD_DIM], ones]
        v_ref[0, :, rows] = jnp.concatenate(parts, axis=0).astype(BF16)

    def post_gz():
        gz_ref[0, rows, :] = _silu(proj(3 * D_ATTN, 4 * D_ATTN)).astype(BF16)

    def post_conv():
        c0 = 4 * D_ATTN
        b_g = proj(c0, c0 + D_CONV)
        u = proj(c0 + D_CONV, c0 + 2 * D_CONV) * proj(c0 + 2 * D_CONV, c0 + 3 * D_CONV)
        z_c = proj(c0 + 3 * D_CONV, c0 + 4 * D_CONV)
        row = lax.broadcasted_iota(jnp.int32, u.shape, 0)
        prev1 = carry_ref[7:8, :]
        prev2 = carry_ref[6:7, :]
        u1 = jnp.where(row == 0, prev1, pltpu.roll(u, 1, 0))
        u2 = jnp.where(row == 0, prev2, jnp.where(row == 1, prev1, pltpu.roll(u, 2, 0)))
        carry_ref[...] = u[TILE - 8:TILE, :]
        y = b_g * (wconv_ref[2:3, :] * u + wconv_ref[1:2, :] * u1 + wconv_ref[0:1, :] * u2)
        y = y * lax.rsqrt(jnp.mean(y * y, axis=-1, keepdims=True) + EPS) * gconv_ref[...]
        yc_ref[0, rows, :] = (y * _silu(z_c)).astype(BF16)

    return [post_qk, post_v, post_gz, post_conv]


def _in_proj(x, mod3, g_norm, positions, inv_freq, w_hi, w_conv, g_conv_out):
    bsz, seq, _ = x.shape
    n_blocks = seq // TILE
    step_rows = TILES_PER_STEP * TILE
    wide = N_HEADS * LANES
    row_spec = lambda width: pl.BlockSpec((1, step_rows, width), lambda b, i: (b, i, 0))
    col_spec = lambda height: pl.BlockSpec((1, height, step_rows), lambda b, i: (b, 0, i))
    const = lambda shape: pl.BlockSpec(shape, lambda b, i: (0,) * len(shape))
    kbd_rows = N_HEADS * n_blocks
    n_groups = w_hi.shape[1] // D_ATTN
    pos_compact = jnp.repeat(positions.reshape(bsz, seq // POS_PER_ROW, POS_PER_ROW), HALF, axis=2)
    return pl.pallas_call(
        functools.partial(_in_proj_kernel, n_blocks, n_groups),
        grid=(bsz, n_blocks // TILES_PER_STEP),
        in_specs=[row_spec(D_MODEL),
                  pl.BlockSpec((1, 3, D_MODEL), lambda b, i: (b, 0, 0)),
                  const((1, D_MODEL)),
                  pl.BlockSpec((1, step_rows // POS_PER_ROW, LANES), lambda b, i: (b, i, 0)),
                  const((1, LANES)),
                  *[pl.BlockSpec((D_MODEL, D_ATTN), lambda b, i, g=g: (0, g))
                    for g in range(n_groups)],
                  const((CONV_WIDTH, D_CONV)),
                  const((1, D_CONV))],
        out_specs=[col_spec(wide), row_spec(wide), col_spec(N_HEADS * V_ROWS),
                   row_spec(D_ATTN), row_spec(D_CONV)],
        out_shape=[jax.ShapeDtypeStruct((bsz, wide, seq), BF16),
                   jax.ShapeDtypeStruct((bsz, seq, wide), BF16),
                   jax.ShapeDtypeStruct((bsz, N_HEADS * V_ROWS, seq), BF16),
                   jax.ShapeDtypeStruct((bsz, seq, D_ATTN), BF16),
                   jax.ShapeDtypeStruct((bsz, seq, D_CONV), BF16)],
        scratch_shapes=[pltpu.VMEM((kbd_rows, D_ATTN), F32),
                        pltpu.VMEM((8, D_CONV), F32),
                        pltpu.VMEM((TILES_PER_STEP, TILE, LANES), F32),
                        pltpu.VMEM((TILES_PER_STEP, TILE, LANES), F32),
                        pltpu.VMEM((step_rows, w_hi.shape[1]), F32),
                        pltpu.VMEM((TILES_PER_STEP, TILE, D_MODEL), BF16)],
        compiler_params=pltpu.CompilerParams(dimension_semantics=("arbitrary", "arbitrary"),
                                             vmem_limit_bytes=VMEM_LIMIT),
        name="in_proj",
    )(x, mod3, g_norm.reshape(1, D_MODEL), pos_compact, inv_freq,
      *([w_hi] * n_groups), w_conv, g_conv_out.reshape(1, D_CONV))


def _moba_kernel(n_chunks, tile_ref, chunk_ref, q_ref, k_ref, v_ref, o_ref, s_ref, smax_ref,
                 m_ref, acc_ref):
    n_past = n_chunks * (n_chunks - 1) // 2

    def own_item(t):
        return t, t

    def past_item(j):
        return tile_ref[j], chunk_ref[j]

    def own_keys(half):
        return (half + 1) * MXU_COLS

    def issue_unit(buf, tile, chunk, hd, half, own):
        n_keys = own_keys(half) if own else CHUNK
        start = pl.multiple_of(chunk * CHUNK, CHUNK)
        q_start = pl.multiple_of(tile * CHUNK + half * MXU_COLS, MXU_COLS)
        rows = slice(hd * LANES, (hd + 1) * LANES)
        cols = slice(half * MXU_COLS, (half + 1) * MXU_COLS)
        kj = k_ref[0, pl.ds(start, n_keys), rows]
        s = jnp.dot(kj, q_ref[0, rows, pl.ds(q_start, MXU_COLS)],
                    preferred_element_type=F32)
        s_ref[buf, hd, :n_keys, cols] = s.astype(BF16)
        if not own:
            smax_ref[buf, hd, :, cols] = jnp.max(s, axis=0, keepdims=True)

    def absorb_unit(buf, tile, chunk, hd, half, own):
        n_keys = own_keys(half) if own else CHUNK
        start = pl.multiple_of(chunk * CHUNK, CHUNK)
        cols = slice(half * MXU_COLS, (half + 1) * MXU_COLS)
        m = m_ref[tile, hd, :, cols]
        if own:
            diag = s_ref[buf, hd, n_keys - MXU_COLS:n_keys, cols].astype(F32)
            key = lax.broadcasted_iota(jnp.int32, diag.shape, 0)
            qry = lax.broadcasted_iota(jnp.int32, diag.shape, 1)
            diag = jnp.where(key <= qry, diag, NEG)
            blocks = [s_ref[buf, hd, b * MXU_COLS:(b + 1) * MXU_COLS, cols] for b in range(half)]
            blocks.append(diag.astype(BF16))
            smax = jnp.max(diag, axis=0, keepdims=True)
            for block in blocks[:-1]:
                smax = jnp.maximum(smax, jnp.max(block, axis=0, keepdims=True).astype(F32))
        else:
            blocks = [s_ref[buf, hd, :, cols]]
            smax = smax_ref[buf, hd, :, cols]
        m_new = jnp.maximum(m, smax).astype(BF16)
        p = [jnp.exp2(block - m_new) for block in blocks]
        p = p[0] if len(p) == 1 else jnp.concatenate(p, axis=0)
        m_new = m_new.astype(F32)
        vj = v_ref[0, hd * V_ROWS:(hd + 1) * V_ROWS, pl.ds(start, n_keys)]
        acc_ref[tile, hd, :, cols] = (jnp.exp2(m - m_new) * acc_ref[tile, hd, :, cols]
                                      + jnp.dot(vj, p, preferred_element_type=F32))
        m_ref[tile, hd, :, cols] = m_new

    def step(issue=None, absorb=None, issue_own=False, absorb_own=False):
        for half in range(CHUNK // MXU_COLS):
            for hd in range(2):
                if issue is not None:
                    issue_unit(*issue, hd, half, issue_own)
                if absorb is not None:
                    absorb_unit(*absorb, hd, half, absorb_own)

    def run_pairs(item, n_pairs, pairs_per_trip, own):
        def pair(u):
            step(issue=(1, *item(2 * u + 1)), absorb=(0, *item(2 * u)),
                 issue_own=own, absorb_own=own)
            step(issue=(0, *item(2 * u + 2)), absorb=(1, *item(2 * u + 1)),
                 issue_own=own, absorb_own=own)

        def trip(t, carry):
            for u in range(pairs_per_trip):
                pair(pairs_per_trip * t + u)
            return carry

        trips = n_pairs // pairs_per_trip
        lax.fori_loop(0, trips, trip, 0)
        for u in range(trips * pairs_per_trip, n_pairs):
            pair(u)

    m_ref[...] = jnp.full(m_ref.shape, -jnp.inf, F32)
    acc_ref[...] = jnp.zeros(acc_ref.shape, F32)
    assert n_chunks % 2 == 0 and n_past % 2 == 0
    step(issue=(0, *own_item(0)), issue_own=True)
    run_pairs(own_item, (n_chunks - 2) // 2, OWN_PAIRS_PER_TRIP, own=True)
    step(issue=(1, *own_item(n_chunks - 1)), absorb=(0, *own_item(n_chunks - 2)),
         issue_own=True, absorb_own=True)
    step(issue=(0, *past_item(0)), absorb=(1, *own_item(n_chunks - 1)), absorb_own=True)
    run_pairs(past_item, (n_past - 2) // 2, PAST_PAIRS_PER_TRIP, own=False)
    step(issue=(1, *past_item(n_past - 1)), absorb=(0, *past_item(n_past - 2)))
    step(absorb=(1, *past_item(n_past - 1)))

    def finish(tile, carry):
        outs = [acc_ref[tile, hd, :HEAD_DIM] / acc_ref[tile, hd, HEAD_DIM:HEAD_DIM + 1]
                for hd in range(2)]
        start = pl.multiple_of(tile * CHUNK, CHUNK)
        o_ref[0, pl.ds(start, CHUNK), :] = jnp.concatenate(outs, axis=0).T.astype(o_ref.dtype)
        return carry

    lax.fori_loop(0, n_chunks, finish, 0)


def _moba(q_aug, k_aug, v_aug):
    bsz, seq, wide = k_aug.shape
    pairs = wide // (2 * LANES)
    n_chunks = seq // CHUNK
    past = [(t, c) for t in range(n_chunks) for c in range(t)]
    item_tile = jnp.asarray([t for t, _ in past], jnp.int32)
    item_chunk = jnp.asarray([c for _, c in past], jnp.int32)
    return pl.pallas_call(
        functools.partial(_moba_kernel, n_chunks),
        grid_spec=pltpu.PrefetchScalarGridSpec(
            num_scalar_prefetch=2,
            grid=(bsz, pairs),
            in_specs=[pl.BlockSpec((1, 2 * LANES, seq), lambda b, p, *_: (b, p, 0)),
                      pl.BlockSpec((1, seq, 2 * LANES), lambda b, p, *_: (b, 0, p)),
                      pl.BlockSpec((1, 2 * V_ROWS, seq), lambda b, p, *_: (b, p, 0))],
            out_specs=pl.BlockSpec((1, seq, LANES), lambda b, p, *_: (b, 0, p)),
            scratch_shapes=[pltpu.VMEM((2, 2, CHUNK, CHUNK), BF16),
                            pltpu.VMEM((2, 2, 1, CHUNK), F32),
                            pltpu.VMEM((n_chunks, 2, 1, CHUNK), F32),
                            pltpu.VMEM((n_chunks, 2, V_ROWS, CHUNK), F32)]),
        out_shape=jax.ShapeDtypeStruct((bsz, seq, pairs * LANES), BF16),
        compiler_params=pltpu.CompilerParams(
            dimension_semantics=("arbitrary", "arbitrary"),
            vmem_limit_bytes=VMEM_LIMIT),
        name="moba",
    )(item_tile, item_chunk, q_aug, k_aug, v_aug)


def _out_proj_kernel(ya_ref, gz_ref, yc_ref, x_ref, mod_ref, gattn_ref, wout_ref, gfin_ref, o_ref):
    ya = ya_ref[0].astype(F32)
    yn = ya * lax.rsqrt(jnp.mean(ya * ya, axis=-1, keepdims=True) + EPS) * gattn_ref[...]
    yn = (yn * gz_ref[0].astype(F32)).astype(BF16)
    y = (jnp.dot(yn, wout_ref[0:D_ATTN, :], preferred_element_type=F32)
         + jnp.dot(yc_ref[0], wout_ref[D_ATTN:, :], preferred_element_type=F32))
    xo = x_ref[0] + mod_ref[0, 2:3, :] * y
    o_ref[0] = xo * lax.rsqrt(jnp.mean(xo * xo, axis=-1, keepdims=True) + EPS) * gfin_ref[...]


def _out_proj(y_attn, gz, yc, x, mod3, g_attn_out, w_out, g_final):
    bsz, seq, _ = x.shape
    row_spec = lambda width: pl.BlockSpec((1, OUT_TILE, width), lambda b, i: (b, i, 0))
    const = lambda shape: pl.BlockSpec(shape, lambda b, i: (0,) * len(shape))
    return pl.pallas_call(
        _out_proj_kernel,
        grid=(bsz, seq // OUT_TILE),
        in_specs=[row_spec(D_ATTN), row_spec(D_ATTN), row_spec(D_CONV), row_spec(D_MODEL),
                  pl.BlockSpec((1, 3, D_MODEL), lambda b, i: (b, 0, 0)),
                  const((1, D_ATTN)), const(w_out.shape), const((1, D_MODEL))],
        out_specs=row_spec(D_MODEL),
        out_shape=jax.ShapeDtypeStruct((bsz, seq, D_MODEL), F32),
        compiler_params=pltpu.CompilerParams(dimension_semantics=("arbitrary", "arbitrary"),
                                             vmem_limit_bytes=VMEM_LIMIT),
        name="out_proj",
    )(y_attn, gz, yc, x, mod3, g_attn_out.reshape(1, D_ATTN), w_out, g_final.reshape(1, D_MODEL))


def kernel(x, c, positions, w_ada, b_ada, g_norm, w_in, w_conv, g_attn_out, g_conv_out, w_out, g_final):
    bsz, seq, _ = x.shape
    assert seq % (TILES_PER_STEP * TILE) == 0 and seq % OUT_TILE == 0 and seq // TILE <= HEAD_DIM // 2
    mod3 = _adaln(c, w_ada, b_ada).reshape(bsz, 3, D_MODEL)

    inv_freq = ROPE_THETA ** (-jnp.arange(HALF, dtype=F32) / HALF)
    inv_freq = jnp.tile(inv_freq, LANES // HALF).reshape(1, LANES)

    q_aug, k_aug, v_aug, gz, yc = _in_proj(x, mod3, g_norm, positions, inv_freq,
                                           w_in.astype(BF16), w_conv, g_conv_out)
    y_attn = _moba(q_aug, k_aug, v_aug)
    return _out_proj(y_attn, gz, yc, x, mod3, g_attn_out, w_out.astype(BF16), g_final)
```

```python
import functools

import jax
import jax.numpy as jnp
from jax import lax
from jax.experimental import pallas as pl
from jax.experimental.pallas import tpu as pltpu

D_MODEL = 1024
D_ATTN = 512
D_CONV = 512
N_HEADS = 8
HEAD_DIM = 64
HALF = HEAD_DIM // 2
CONV_WIDTH = 3
MOBA_BLOCK = 256
MOBA_TOPK = 3
ROPE_THETA = 10000.0
EPS = 1e-6
NEG = -1e30
LOG2E = 1.4426950408889634

LANES = 128
V_ROWS = HEAD_DIM + 16
TILE = MOBA_BLOCK
CHUNK = 2 * TILE
TILES_PER_STEP = 2
MXU_COLS = 256
POS_PER_ROW = LANES // HALF
ROW_CHUNK = 128
OWN_PAIRS_PER_TRIP = 7
PAST_PAIRS_PER_TRIP = 5
OUT_TILE = 1024
VMEM_LIMIT = 56 * 1024 * 1024

F32 = jnp.float32
BF16 = jnp.bfloat16


def _silu(z):
    return z * (1.0 / (1.0 + jnp.exp(-z)))


def _split_bf16(a):
    hi = a.astype(BF16)
    lo = (a - hi.astype(F32)).astype(BF16)
    return hi, lo


def _adaln_kernel(c_ref, w_ref, b_ref, o_ref):
    act = _silu(c_ref[...])
    rows = [jnp.sum(act[:, b:b + 1] * w_ref[...], axis=0, keepdims=True)
            for b in range(act.shape[1])]
    o_ref[...] = jnp.concatenate(rows, axis=0) + b_ref[...]


def _adaln(c, w_ada, b_ada):
    bsz = c.shape[0]
    n = w_ada.shape[1]
    bn = D_MODEL
    return pl.pallas_call(
        _adaln_kernel,
        grid=(n // bn,),
        in_specs=[pl.BlockSpec((D_MODEL, bsz), lambda j: (0, 0)),
                  pl.BlockSpec((D_MODEL, bn), lambda j: (0, j)),
                  pl.BlockSpec((1, bn), lambda j: (0, j))],
        out_specs=pl.BlockSpec((bsz, bn), lambda j: (0, j)),
        out_shape=jax.ShapeDtypeStruct((bsz, n), F32),
        compiler_params=pltpu.CompilerParams(dimension_semantics=("arbitrary",),
                                             vmem_limit_bytes=VMEM_LIMIT),
        name="adaln",
    )(c.T, w_ada, b_ada.reshape(1, n))


def _in_proj_kernel(n_blocks, n_groups, x_ref, mod_ref, gnorm_ref, pos_ref, freq_ref, *refs):
    w_refs = refs[:n_groups]
    (wconv_ref, gconv_ref, q_ref, k_ref, v_ref, gz_ref, yc_ref,
     kbd_ref, carry_ref, cos_ref, sin_ref, proj_ref, h_ref) = refs[n_groups:]
    step = pl.program_id(1)

    @pl.when(step == 0)
    def _():
        kbd_ref[...] = jnp.zeros_like(kbd_ref)
        carry_ref[...] = jnp.zeros_like(carry_ref)

    n_cols = n_groups * D_ATTN
    col_groups = [(0, 2 * D_ATTN), (2 * D_ATTN, 3 * D_ATTN), (3 * D_ATTN, 4 * D_ATTN),
                  (4 * D_ATTN, n_cols)]

    def norm(r):
        x = x_ref[0, r * TILE:(r + 1) * TILE, :]
        xn = x * lax.rsqrt(jnp.mean(x * x, axis=-1, keepdims=True) + EPS) * gnorm_ref[...]
        h_ref[r] = (xn * (1.0 + mod_ref[0, 1:2, :]) + mod_ref[0, 0:1, :]).astype(BF16)

    def project(r, group):
        lo, hi = col_groups[group]
        for c0 in range(lo, hi, D_ATTN):
            proj_ref[r * TILE:(r + 1) * TILE, c0:c0 + D_ATTN] = jnp.dot(
                h_ref[r], w_refs[c0 // D_ATTN][...], preferred_element_type=F32)

    posts = [_block_posts(n_blocks, r, step * TILES_PER_STEP + r, pos_ref, freq_ref, wconv_ref,
                          gconv_ref, q_ref, k_ref, v_ref, gz_ref, yc_ref, kbd_ref, carry_ref,
                          cos_ref, sin_ref, proj_ref)
             for r in range(TILES_PER_STEP)]
    stages = [(r, group) for r in range(TILES_PER_STEP) for group in (3, 0, 2, 1)]
    norm(0)
    project(*stages[0])
    for s, (r, group) in enumerate(stages):
        if s + 1 < len(stages):
            if stages[s + 1][0] != r:
                norm(stages[s + 1][0])
            project(*stages[s + 1])
        posts[r][group]()


def _block_posts(n_blocks, r, i, pos_ref, freq_ref, wconv_ref, gconv_ref,
                 q_ref, k_ref, v_ref, gz_ref, yc_ref, kbd_ref, carry_ref, cos_ref, sin_ref,
                 proj_ref):
    rows = slice(r * TILE, (r + 1) * TILE)

    def proj(c0, c1):
        return proj_ref[rows, c0:c1]

    def post_qk():
        packed = TILE // POS_PER_ROW
        lane_c = lax.broadcasted_iota(jnp.int32, (packed, LANES), 1)
        ang = pos_ref[0, r * packed:(r + 1) * packed, :].astype(F32) * freq_ref[...]
        sign = jnp.where((lane_c % HEAD_DIM) < HALF, -1.0, 1.0)
        for table_ref, table, scale in ((cos_ref, jnp.cos(ang), None), (sin_ref, jnp.sin(ang), sign)):
            for a in range(POS_PER_ROW):
                z = jnp.where(lane_c // HALF == a, table, 0.0)
                z = z + pltpu.roll(z, 2 * HALF, 1)
                z = z + pltpu.roll(z, HALF, 1)
                table_ref[r, pl.ds(a, packed, stride=POS_PER_ROW), :] = (
                    z if scale is None else z * scale)
        cos = cos_ref[r]
        sin = sin_ref[r]

        lane = lax.broadcasted_iota(jnp.int32, (TILE, LANES), 1)
        first_half = (lane % HEAD_DIM) < HALF

        def rope(t):
            partner = jnp.where(first_half, pltpu.roll(t, LANES - HALF, 1), pltpu.roll(t, HALF, 1))
            return t * cos + partner * sin

        n_groups = D_ATTN // LANES
        q_groups = [rope(proj(g * LANES, (g + 1) * LANES)) for g in range(n_groups)]
        k_groups = [rope(proj(D_ATTN + g * LANES, D_ATTN + (g + 1) * LANES)) for g in range(n_groups)]
        q = jnp.concatenate(q_groups, axis=1)
        k = jnp.concatenate(k_groups, axis=1)

        nt = (((1,), (1,)), ((), ()))
        scores = lax.dot_general(kbd_ref[...].astype(BF16), q.astype(BF16), nt,
                                 preferred_element_type=F32)
        g3 = scores.reshape(N_HEADS, n_blocks, TILE)
        blk = lax.broadcasted_iota(jnp.int32, g3.shape, 1)
        past = blk < i
        g3 = jnp.where(past, g3, NEG)
        picked = jnp.zeros(g3.shape, jnp.bool_)
        for _ in range(min(MOBA_TOPK, n_blocks - 1)):
            top = jnp.max(g3, axis=1, keepdims=True)
            first = jnp.min(jnp.where(g3 == top, blk, n_blocks), axis=1, keepdims=True)
            hit = blk == first
            picked = jnp.logical_or(picked, hit)
            g3 = jnp.where(hit, -jnp.inf, g3)
        keep = jnp.logical_or(jnp.logical_and(picked, past), blk == i)
        bias = jnp.where(keep, 0.0, NEG)

        k_mean = jnp.mean(k, axis=0, keepdims=True)
        lane_w = lax.broadcasted_iota(jnp.int32, (1, D_ATTN), 1)
        for hd in range(N_HEADS):
            own = (lane_w // HEAD_DIM) == hd
            kbd_ref[pl.ds(hd * n_blocks + i, 1), :] = jnp.where(own, k_mean, 0.0)

        q_t = (q * (HEAD_DIM ** -0.5 * LOG2E)).T
        pad = jnp.zeros((HEAD_DIM - n_blocks, TILE), F32)
        parts = []
        for hd in range(N_HEADS):
            dims = q_t[hd * HEAD_DIM:(hd + 1) * HEAD_DIM]
            parts += [dims, bias[hd], pad] if hd % 2 == 0 else [bias[hd], pad, dims]
        q_ref[0, :, rows] = jnp.concatenate(parts, axis=0).astype(BF16)

        low = lane < HEAD_DIM
        onehot_even = jnp.where(lane == HEAD_DIM + i, 1.0, 0.0)
        onehot_odd = jnp.where(lane == i, 1.0, 0.0)
        for g in range(n_groups):
            even = slice(2 * g * LANES, (2 * g + 1) * LANES)
            odd = slice((2 * g + 1) * LANES, (2 * g + 2) * LANES)
            k_ref[0, rows, even] = jnp.where(low, k_groups[g], onehot_even).astype(BF16)
            k_ref[0, rows, odd] = jnp.where(low, onehot_odd, k_groups[g]).astype(BF16)

    def chunks():
        for c in range(TILE // ROW_CHUNK):
            yield slice(r * TILE + c * ROW_CHUNK, r * TILE + (c + 1) * ROW_CHUNK)

    def post_v():
        for rc in chunks():
            v_t = proj_ref[rc, 2 * D_ATTN:3 * D_ATTN].T
            ones = jnp.ones((V_ROWS - HEAD_DIM, ROW_CHUNK), F32)
            parts = []
            for hd in range(N_HEADS):
                parts += [v_t[hd * HEAD_DIM:(hd + 1) * HEAD_DIM], ones]
            v_ref[0, :, rc] = jnp.concatenate(parts, axis=0).astype(BF16)

    def post_gz():
        for rc in chunks():
            gz_ref[0, rc, :] = _silu(proj_ref[rc, 3 * D_ATTN:4 * D_ATTN]).astype(BF16)

    def post_conv():
        c0 = 4 * D_ATTN
        for rc in chunks():
            b_g = proj_ref[rc, c0:c0 + D_CONV]
            u = (proj_ref[rc, c0 + D_CONV:c0 + 2 * D_CONV]
                 * proj_ref[rc, c0 + 2 * D_CONV:c0 + 3 * D_CONV])
            z_c = proj_ref[rc, c0 + 3 * D_CONV:c0 + 4 * D_CONV]
            row = lax.broadcasted_iota(jnp.int32, u.shape, 0)
            prev1 = carry_ref[7:8, :]
            prev2 = carry_ref[6:7, :]
            u1 = jnp.where(row == 0, prev1, pltpu.roll(u, 1, 0))
            u2 = jnp.where(row == 0, prev2, jnp.where(row == 1, prev1, pltpu.roll(u, 2, 0)))
            carry_ref[...] = u[ROW_CHUNK - 8:ROW_CHUNK, :]
            y = b_g * (wconv_ref[2:3, :] * u + wconv_ref[1:2, :] * u1 + wconv_ref[0:1, :] * u2)
            y = y * lax.rsqrt(jnp.mean(y * y, axis=-1, keepdims=True) + EPS) * gconv_ref[...]
            yc_ref[0, rc, :] = (y * _silu(z_c)).astype(BF16)

    return [post_qk, post_v, post_gz, post_conv]


def _in_proj(x, mod3, g_norm, positions, inv_freq, w_hi, w_conv, g_conv_out):
    bsz, seq, _ = x.shape
    n_blocks = seq // TILE
    step_rows = TILES_PER_STEP * TILE
    wide = N_HEADS * LANES
    row_spec = lambda width: pl.BlockSpec((1, step_rows, width), lambda b, i: (b, i, 0))
    col_spec = lambda height: pl.BlockSpec((1, height, step_rows), lambda b, i: (b, 0, i))
    const = lambda shape: pl.BlockSpec(shape, lambda b, i: (0,) * len(shape))
    kbd_rows = N_HEADS * n_blocks
    n_groups = w_hi.shape[1] // D_ATTN
    pos_compact = jnp.repeat(positions.reshape(bsz, seq // POS_PER_ROW, POS_PER_ROW), HALF, axis=2)
    return pl.pallas_call(
        functools.partial(_in_proj_kernel, n_blocks, n_groups),
        grid=(bsz, n_blocks // TILES_PER_STEP),
        in_specs=[row_spec(D_MODEL),
                  pl.BlockSpec((1, 3, D_MODEL), lambda b, i: (b, 0, 0)),
                  const((1, D_MODEL)),
                  pl.BlockSpec((1, step_rows // POS_PER_ROW, LANES), lambda b, i: (b, i, 0)),
                  const((1, LANES)),
                  *[pl.BlockSpec((D_MODEL, D_ATTN), lambda b, i, g=g: (0, g))
                    for g in range(n_groups)],
                  const((CONV_WIDTH, D_CONV)),
                  const((1, D_CONV))],
        out_specs=[col_spec(wide), row_spec(wide), col_spec(N_HEADS * V_ROWS),
                   row_spec(D_ATTN), row_spec(D_CONV)],
        out_shape=[jax.ShapeDtypeStruct((bsz, wide, seq), BF16),
                   jax.ShapeDtypeStruct((bsz, seq, wide), BF16),
                   jax.ShapeDtypeStruct((bsz, N_HEADS * V_ROWS, seq), BF16),
                   jax.ShapeDtypeStruct((bsz, seq, D_ATTN), BF16),
                   jax.ShapeDtypeStruct((bsz, seq, D_CONV), BF16)],
        scratch_shapes=[pltpu.VMEM((kbd_rows, D_ATTN), F32),
                        pltpu.VMEM((8, D_CONV), F32),
                        pltpu.VMEM((TILES_PER_STEP, TILE, LANES), F32),
                        pltpu.VMEM((TILES_PER_STEP, TILE, LANES), F32),
                        pltpu.VMEM((step_rows, w_hi.shape[1]), F32),
                        pltpu.VMEM((TILES_PER_STEP, TILE, D_MODEL), BF16)],
        compiler_params=pltpu.CompilerParams(dimension_semantics=("arbitrary", "arbitrary"),
                                             vmem_limit_bytes=VMEM_LIMIT),
        name="in_proj",
    )(x, mod3, g_norm.reshape(1, D_MODEL), pos_compact, inv_freq,
      *([w_hi] * n_groups), w_conv, g_conv_out.reshape(1, D_CONV))


def _moba_kernel(n_chunks, tile_ref, chunk_ref, q_ref, k_ref, v_ref, o_ref, s_ref, smax_ref,
                 m_ref, acc_ref):
    n_past = n_chunks * (n_chunks - 1) // 2

    def own_item(t):
        return t, t

    def past_item(j):
        return tile_ref[j], chunk_ref[j]

    def own_keys(half):
        return (half + 1) * MXU_COLS

    def issue_unit(buf, tile, chunk, hd, half, own):
        n_keys = own_keys(half) if own else CHUNK
        start = pl.multiple_of(chunk * CHUNK, CHUNK)
        q_start = pl.multiple_of(tile * CHUNK + half * MXU_COLS, MXU_COLS)
        rows = slice(hd * LANES, (hd + 1) * LANES)
        cols = slice(half * MXU_COLS, (half + 1) * MXU_COLS)
        kj = k_ref[0, pl.ds(start, n_keys), rows]
        s = jnp.dot(kj, q_ref[0, rows, pl.ds(q_start, MXU_COLS)],
                    preferred_element_type=F32)
        s_ref[buf, hd, :n_keys, cols] = s.astype(BF16)
        if not own:
            smax_ref[buf, hd, :, cols] = jnp.max(s, axis=0, keepdims=True)

    def absorb_unit(buf, tile, chunk, hd, half, own):
        n_keys = own_keys(half) if own else CHUNK
        start = pl.multiple_of(chunk * CHUNK, CHUNK)
        cols = slice(half * MXU_COLS, (half + 1) * MXU_COLS)
        m = m_ref[tile, hd, :, cols]
        if own:
            diag = s_ref[buf, hd, n_keys - MXU_COLS:n_keys, cols].astype(F32)
            key = lax.broadcasted_iota(jnp.int32, diag.shape, 0)
            qry = lax.broadcasted_iota(jnp.int32, diag.shape, 1)
            diag = jnp.where(key <= qry, diag, NEG)
            blocks = [s_ref[buf, hd, b * MXU_COLS:(b + 1) * MXU_COLS, cols] for b in range(half)]
            blocks.append(diag.astype(BF16))
            smax = jnp.max(diag, axis=0, keepdims=True)
            for block in blocks[:-1]:
                smax = jnp.maximum(smax, jnp.max(block, axis=0, keepdims=True).astype(F32))
        else:
            blocks = [s_ref[buf, hd, :, cols]]
            smax = smax_ref[buf, hd, :, cols]
        m_new = jnp.maximum(m, smax).astype(BF16)
        p = [jnp.exp2(block - m_new) for block in blocks]
        p = p[0] if len(p) == 1 else jnp.concatenate(p, axis=0)
        m_new = m_new.astype(F32)
        vj = v_ref[0, hd * V_ROWS:(hd + 1) * V_ROWS, pl.ds(start, n_keys)]
        acc_ref[tile, hd, :, cols] = (jnp.exp2(m - m_new) * acc_ref[tile, hd, :, cols]
                                      + jnp.dot(vj, p, preferred_element_type=F32))
        m_ref[tile, hd, :, cols] = m_new

    def step(issue=None, absorb=None, issue_own=False, absorb_own=False):
        for half in range(CHUNK // MXU_COLS):
            for hd in range(2):
                if issue is not None:
                    issue_unit(*issue, hd, half, issue_own)
                if absorb is not None:
                    absorb_unit(*absorb, hd, half, absorb_own)

    def run_pairs(item, n_pairs, pairs_per_trip, own):
        def pair(u):
            step(issue=(1, *item(2 * u + 1)), absorb=(0, *item(2 * u)),
                 issue_own=own, absorb_own=own)
            step(issue=(0, *item(2 * u + 2)), absorb=(1, *item(2 * u + 1)),
                 issue_own=own, absorb_own=own)

        def trip(t, carry):
            for u in range(pairs_per_trip):
                pair(pairs_per_trip * t + u)
            return carry

        trips = n_pairs // pairs_per_trip
        lax.fori_loop(0, trips, trip, 0)
        for u in range(trips * pairs_per_trip, n_pairs):
            pair(u)

    m_ref[...] = jnp.full(m_ref.shape, -jnp.inf, F32)
    acc_ref[...] = jnp.zeros(acc_ref.shape, F32)
    assert n_chunks % 2 == 0 and n_past % 2 == 0
    step(issue=(0, *own_item(0)), issue_own=True)
    run_pairs(own_item, (n_chunks - 2) // 2, OWN_PAIRS_PER_TRIP, own=True)
    step(issue=(1, *own_item(n_chunks - 1)), absorb=(0, *own_item(n_chunks - 2)),
         issue_own=True, absorb_own=True)
    step(issue=(0, *past_item(0)), absorb=(1, *own_item(n_chunks - 1)), absorb_own=True)
    run_pairs(past_item, (n_past - 2) // 2, PAST_PAIRS_PER_TRIP, own=False)
    step(issue=(1, *past_item(n_past - 1)), absorb=(0, *past_item(n_past - 2)))
    step(absorb=(1, *past_item(n_past - 1)))

    def finish(tile, carry):
        outs = [acc_ref[tile, hd, :HEAD_DIM] / acc_ref[tile, hd, HEAD_DIM:HEAD_DIM + 1]
                for hd in range(2)]
        start = pl.multiple_of(tile * CHUNK, CHUNK)
        o_ref[0, pl.ds(start, CHUNK), :] = jnp.concatenate(outs, axis=0).T.astype(o_ref.dtype)
        return carry

    lax.fori_loop(0, n_chunks, finish, 0)


def _moba(q_aug, k_aug, v_aug):
    bsz, seq, wide = k_aug.shape
    pairs = wide // (2 * LANES)
    n_chunks = seq // CHUNK
    past = [(t, c) for t in range(n_chunks) for c in range(t)]
    item_tile = jnp.asarray([t for t, _ in past], jnp.int32)
    item_chunk = jnp.asarray([c for _, c in past], jnp.int32)
    return pl.pallas_call(
        functools.partial(_moba_kernel, n_chunks),
        grid_spec=pltpu.PrefetchScalarGridSpec(
            num_scalar_prefetch=2,
            grid=(bsz, pairs),
            in_specs=[pl.BlockSpec((1, 2 * LANES, seq), lambda b, p, *_: (b, p, 0)),
                      pl.BlockSpec((1, seq, 2 * LANES), lambda b, p, *_: (b, 0, p)),
                      pl.BlockSpec((1, 2 * V_ROWS, seq), lambda b, p, *_: (b, p, 0))],
            out_specs=pl.BlockSpec((1, seq, LANES), lambda b, p, *_: (b, 0, p)),
            scratch_shapes=[pltpu.VMEM((2, 2, CHUNK, CHUNK), BF16),
                            pltpu.VMEM((2, 2, 1, CHUNK), F32),
                            pltpu.VMEM((n_chunks, 2, 1, CHUNK), F32),
                            pltpu.VMEM((n_chunks, 2, V_ROWS, CHUNK), F32)]),
        out_shape=jax.ShapeDtypeStruct((bsz, seq, pairs * LANES), BF16),
        compiler_params=pltpu.CompilerParams(
            dimension_semantics=("arbitrary", "arbitrary"),
            vmem_limit_bytes=VMEM_LIMIT),
        name="moba",
    )(item_tile, item_chunk, q_aug, k_aug, v_aug)


def _out_proj_kernel(ya_ref, gz_ref, yc_ref, x_ref, mod_ref, gattn_ref, wout_ref, gfin_ref, o_ref):
    ya = ya_ref[0].astype(F32)
    yn = ya * lax.rsqrt(jnp.mean(ya * ya, axis=-1, keepdims=True) + EPS) * gattn_ref[...]
    yn = (yn * gz_ref[0].astype(F32)).astype(BF16)
    y = (jnp.dot(yn, wout_ref[0:D_ATTN, :], preferred_element_type=F32)
         + jnp.dot(yc_ref[0], wout_ref[D_ATTN:, :], preferred_element_type=F32))
    xo = x_ref[0] + mod_ref[0, 2:3, :] * y
    o_ref[0] = xo * lax.rsqrt(jnp.mean(xo * xo, axis=-1, keepdims=True) + EPS) * gfin_ref[...]


def _out_proj(y_attn, gz, yc, x, mod3, g_attn_out, w_out, g_final):
    bsz, seq, _ = x.shape
    row_spec = lambda width: pl.BlockSpec((1, OUT_TILE, width), lambda b, i: (b, i, 0))
    const = lambda shape: pl.BlockSpec(shape, lambda b, i: (0,) * len(shape))
    return pl.pallas_call(
        _out_proj_kernel,
        grid=(bsz, seq // OUT_TILE),
        in_specs=[row_spec(D_ATTN), row_spec(D_ATTN), row_spec(D_CONV), row_spec(D_MODEL),
                  pl.BlockSpec((1, 3, D_MODEL), lambda b, i: (b, 0, 0)),
                  const((1, D_ATTN)), const(w_out.shape), const((1, D_MODEL))],
        out_specs=row_spec(D_MODEL),
        out_shape=jax.ShapeDtypeStruct((bsz, seq, D_MODEL), F32),
        compiler_params=pltpu.CompilerParams(dimension_semantics=("arbitrary", "arbitrary"),
                                             vmem_limit_bytes=VMEM_LIMIT),
        name="out_proj",
    )(y_attn, gz, yc, x, mod3, g_attn_out.reshape(1, D_ATTN), w_out, g_final.reshape(1, D_MODEL))


def kernel(x, c, positions, w_ada, b_ada, g_norm, w_in, w_conv, g_attn_out, g_conv_out, w_out, g_final):
    bsz, seq, _ = x.shape
    assert seq % (TILES_PER_STEP * TILE) == 0 and seq % OUT_TILE == 0 and seq // TILE <= HEAD_DIM // 2
    mod3 = _adaln(c, w_ada, b_ada).reshape(bsz, 3, D_MODEL)

    inv_freq = ROPE_THETA ** (-jnp.arange(HALF, dtype=F32) / HALF)
    inv_freq = jnp.tile(inv_freq, LANES // HALF).reshape(1, LANES)

    q_aug, k_aug, v_aug, gz, yc = _in_proj(x, mod3, g_norm, positions, inv_freq,
                                           w_in.astype(BF16), w_conv, g_conv_out)
    y_attn = _moba(q_aug, k_aug, v_aug)
    return _out_proj(y_attn, gz, yc, x, mod3, g_attn_out, w_out.astype(BF16), g_final)
```

```python
import functools

import jax
import jax.numpy as jnp
from jax import lax
from jax.experimental import pallas as pl
from jax.experimental.pallas import tpu as pltpu

D_MODEL = 1024
D_ATTN = 512
D_CONV = 512
N_HEADS = 8
HEAD_DIM = 64
HALF = HEAD_DIM // 2
CONV_WIDTH = 3
MOBA_BLOCK = 256
MOBA_TOPK = 3
ROPE_THETA = 10000.0
EPS = 1e-6
NEG = -1e30
LOG2E = 1.4426950408889634

LANES = 128
V_ROWS = HEAD_DIM + 16
TILE = MOBA_BLOCK
CHUNK = 2 * TILE
TILES_PER_STEP = 2
MXU_COLS = 256
POS_PER_ROW = LANES // HALF
ROW_CHUNK = 128
OWN_PAIRS_PER_TRIP = 7
PAST_PAIRS_PER_TRIP = 5
OUT_TILE = 1024
OUT_CHUNK = 512
VMEM_LIMIT = 56 * 1024 * 1024

F32 = jnp.float32
BF16 = jnp.bfloat16


def _silu(z):
    return z * (1.0 / (1.0 + jnp.exp(-z)))


def _split_bf16(a):
    hi = a.astype(BF16)
    lo = (a - hi.astype(F32)).astype(BF16)
    return hi, lo


def _adaln_kernel(c_ref, w_ref, b_ref, o_ref):
    act = _silu(c_ref[...])
    rows = [jnp.sum(act[:, b:b + 1] * w_ref[...], axis=0, keepdims=True)
            for b in range(act.shape[1])]
    o_ref[...] = jnp.concatenate(rows, axis=0) + b_ref[...]


def _adaln(c, w_ada, b_ada):
    bsz = c.shape[0]
    n = w_ada.shape[1]
    bn = D_MODEL
    return pl.pallas_call(
        _adaln_kernel,
        grid=(n // bn,),
        in_specs=[pl.BlockSpec((D_MODEL, bsz), lambda j: (0, 0)),
                  pl.BlockSpec((D_MODEL, bn), lambda j: (0, j)),
                  pl.BlockSpec((1, bn), lambda j: (0, j))],
        out_specs=pl.BlockSpec((bsz, bn), lambda j: (0, j)),
        out_shape=jax.ShapeDtypeStruct((bsz, n), F32),
        compiler_params=pltpu.CompilerParams(dimension_semantics=("arbitrary",),
                                             vmem_limit_bytes=VMEM_LIMIT),
        name="adaln",
    )(c.T, w_ada, b_ada.reshape(1, n))


def _in_proj_kernel(n_blocks, n_groups, x_ref, mod_ref, gnorm_ref, pos_ref, freq_ref, *refs):
    w_refs = refs[:n_groups]
    (wconv_ref, gconv_ref, q_ref, k_ref, v_ref, gz_ref, yc_ref,
     kbd_ref, carry_ref, cos_ref, sin_ref, proj_ref, h_ref) = refs[n_groups:]
    step = pl.program_id(1)

    @pl.when(step == 0)
    def _():
        kbd_ref[...] = jnp.zeros_like(kbd_ref)
        carry_ref[...] = jnp.zeros_like(carry_ref)

    n_cols = n_groups * D_ATTN
    col_groups = [(0, 2 * D_ATTN), (2 * D_ATTN, 3 * D_ATTN), (3 * D_ATTN, 4 * D_ATTN),
                  (4 * D_ATTN, n_cols)]

    def norm(r):
        x = x_ref[0, r * TILE:(r + 1) * TILE, :]
        xn = x * lax.rsqrt(jnp.mean(x * x, axis=-1, keepdims=True) + EPS) * gnorm_ref[...]
        h_ref[r] = (xn * (1.0 + mod_ref[0, 1:2, :]) + mod_ref[0, 0:1, :]).astype(BF16)

    def project(r, group):
        lo, hi = col_groups[group]
        for c0 in range(lo, hi, D_ATTN):
            proj_ref[r * TILE:(r + 1) * TILE, c0:c0 + D_ATTN] = jnp.dot(
                h_ref[r], w_refs[c0 // D_ATTN][...], preferred_element_type=F32)

    posts = [_block_posts(n_blocks, r, step * TILES_PER_STEP + r, pos_ref, freq_ref, wconv_ref,
                          gconv_ref, q_ref, k_ref, v_ref, gz_ref, yc_ref, kbd_ref, carry_ref,
                          cos_ref, sin_ref, proj_ref)
             for r in range(TILES_PER_STEP)]
    stages = [(r, group) for r in range(TILES_PER_STEP) for group in (3, 0, 2, 1)]
    norm(0)
    project(*stages[0])
    for s, (r, group) in enumerate(stages):
        if s + 1 < len(stages):
            if stages[s + 1][0] != r:
                norm(stages[s + 1][0])
            project(*stages[s + 1])
        posts[r][group]()


def _block_posts(n_blocks, r, i, pos_ref, freq_ref, wconv_ref, gconv_ref,
                 q_ref, k_ref, v_ref, gz_ref, yc_ref, kbd_ref, carry_ref, cos_ref, sin_ref,
                 proj_ref):
    rows = slice(r * TILE, (r + 1) * TILE)

    def proj(c0, c1):
        return proj_ref[rows, c0:c1]

    def post_qk():
        packed = TILE // POS_PER_ROW
        lane_c = lax.broadcasted_iota(jnp.int32, (packed, LANES), 1)
        ang = pos_ref[0, r * packed:(r + 1) * packed, :].astype(F32) * freq_ref[...]
        sign = jnp.where((lane_c % HEAD_DIM) < HALF, -1.0, 1.0)
        for table_ref, table, scale in ((cos_ref, jnp.cos(ang), None), (sin_ref, jnp.sin(ang), sign)):
            for a in range(POS_PER_ROW):
                z = jnp.where(lane_c // HALF == a, table, 0.0)
                z = z + pltpu.roll(z, 2 * HALF, 1)
                z = z + pltpu.roll(z, HALF, 1)
                table_ref[r, pl.ds(a, packed, stride=POS_PER_ROW), :] = (
                    z if scale is None else z * scale)
        cos = cos_ref[r]
        sin = sin_ref[r]

        lane = lax.broadcasted_iota(jnp.int32, (TILE, LANES), 1)
        first_half = (lane % HEAD_DIM) < HALF

        def rope(t):
            partner = jnp.where(first_half, pltpu.roll(t, LANES - HALF, 1), pltpu.roll(t, HALF, 1))
            return t * cos + partner * sin

        n_groups = D_ATTN // LANES
        q_groups = [rope(proj(g * LANES, (g + 1) * LANES)) for g in range(n_groups)]
        k_groups = [rope(proj(D_ATTN + g * LANES, D_ATTN + (g + 1) * LANES)) for g in range(n_groups)]
        q = jnp.concatenate(q_groups, axis=1)
        k = jnp.concatenate(k_groups, axis=1)

        nt = (((1,), (1,)), ((), ()))
        scores = lax.dot_general(kbd_ref[...].astype(BF16), q.astype(BF16), nt,
                                 preferred_element_type=F32)
        g3 = scores.reshape(N_HEADS, n_blocks, TILE)
        blk = lax.broadcasted_iota(jnp.int32, g3.shape, 1)
        past = blk < i
        g3 = jnp.where(past, g3, NEG)
        picked = jnp.zeros(g3.shape, jnp.bool_)
        for _ in range(min(MOBA_TOPK, n_blocks - 1)):
            top = jnp.max(g3, axis=1, keepdims=True)
            first = jnp.min(jnp.where(g3 == top, blk, n_blocks), axis=1, keepdims=True)
            hit = blk == first
            picked = jnp.logical_or(picked, hit)
            g3 = jnp.where(hit, -jnp.inf, g3)
        keep = jnp.logical_or(jnp.logical_and(picked, past), blk == i)
        bias = jnp.where(keep, 0.0, NEG)

        k_mean = jnp.mean(k, axis=0, keepdims=True)
        lane_w = lax.broadcasted_iota(jnp.int32, (1, D_ATTN), 1)
        for hd in range(N_HEADS):
            own = (lane_w // HEAD_DIM) == hd
            kbd_ref[pl.ds(hd * n_blocks + i, 1), :] = jnp.where(own, k_mean, 0.0)

        q_t = (q * (HEAD_DIM ** -0.5 * LOG2E)).T
        pad = jnp.zeros((HEAD_DIM - n_blocks, TILE), F32)
        parts = []
        for hd in range(N_HEADS):
            dims = q_t[hd * HEAD_DIM:(hd + 1) * HEAD_DIM]
            parts += [dims, bias[hd], pad] if hd % 2 == 0 else [bias[hd], pad, dims]
        q_ref[0, :, rows] = jnp.concatenate(parts, axis=0).astype(BF16)

        low = lane < HEAD_DIM
        onehot_even = jnp.where(lane == HEAD_DIM + i, 1.0, 0.0)
        onehot_odd = jnp.where(lane == i, 1.0, 0.0)
        for g in range(n_groups):
            even = slice(2 * g * LANES, (2 * g + 1) * LANES)
            odd = slice((2 * g + 1) * LANES, (2 * g + 2) * LANES)
            k_ref[0, rows, even] = jnp.where(low, k_groups[g], onehot_even).astype(BF16)
            k_ref[0, rows, odd] = jnp.where(low, onehot_odd, k_groups[g]).astype(BF16)

    def chunks():
        for c in range(TILE // ROW_CHUNK):
            yield slice(r * TILE + c * ROW_CHUNK, r * TILE + (c + 1) * ROW_CHUNK)

    def post_v():
        for rc in chunks():
            v_t = proj_ref[rc, 2 * D_ATTN:3 * D_ATTN].T
            ones = jnp.ones((V_ROWS - HEAD_DIM, ROW_CHUNK), F32)
            parts = []
            for hd in range(N_HEADS):
                parts += [v_t[hd * HEAD_DIM:(hd + 1) * HEAD_DIM], ones]
            v_ref[0, :, rc] = jnp.concatenate(parts, axis=0).astype(BF16)

    def post_gz():
        for rc in chunks():
            gz_ref[0, rc, :] = _silu(proj_ref[rc, 3 * D_ATTN:4 * D_ATTN]).astype(BF16)

    def post_conv():
        c0 = 4 * D_ATTN
        for rc in chunks():
            b_g = proj_ref[rc, c0:c0 + D_CONV]
            u = (proj_ref[rc, c0 + D_CONV:c0 + 2 * D_CONV]
                 * proj_ref[rc, c0 + 2 * D_CONV:c0 + 3 * D_CONV])
            z_c = proj_ref[rc, c0 + 3 * D_CONV:c0 + 4 * D_CONV]
            row = lax.broadcasted_iota(jnp.int32, u.shape, 0)
            prev1 = carry_ref[7:8, :]
            prev2 = carry_ref[6:7, :]
            u1 = jnp.where(row == 0, prev1, pltpu.roll(u, 1, 0))
            u2 = jnp.where(row == 0, prev2, jnp.where(row == 1, prev1, pltpu.roll(u, 2, 0)))
            carry_ref[...] = u[ROW_CHUNK - 8:ROW_CHUNK, :]
            y = b_g * (wconv_ref[2:3, :] * u + wconv_ref[1:2, :] * u1 + wconv_ref[0:1, :] * u2)
            y = y * lax.rsqrt(jnp.mean(y * y, axis=-1, keepdims=True) + EPS) * gconv_ref[...]
            yc_ref[0, rc, :] = (y * _silu(z_c)).astype(BF16)

    return [post_qk, post_v, post_gz, post_conv]


def _in_proj(x, mod3, g_norm, positions, inv_freq, w_hi, w_conv, g_conv_out):
    bsz, seq, _ = x.shape
    n_blocks = seq // TILE
    step_rows = TILES_PER_STEP * TILE
    wide = N_HEADS * LANES
    row_spec = lambda width: pl.BlockSpec((1, step_rows, width), lambda b, i: (b, i, 0))
    col_spec = lambda height: pl.BlockSpec((1, height, step_rows), lambda b, i: (b, 0, i))
    const = lambda shape: pl.BlockSpec(shape, lambda b, i: (0,) * len(shape))
    kbd_rows = N_HEADS * n_blocks
    n_groups = w_hi.shape[1] // D_ATTN
    pos_compact = jnp.repeat(positions.reshape(bsz, seq // POS_PER_ROW, POS_PER_ROW), HALF, axis=2)
    return pl.pallas_call(
        functools.partial(_in_proj_kernel, n_blocks, n_groups),
        grid=(bsz, n_blocks // TILES_PER_STEP),
        in_specs=[row_spec(D_MODEL),
                  pl.BlockSpec((1, 3, D_MODEL), lambda b, i: (b, 0, 0)),
                  const((1, D_MODEL)),
                  pl.BlockSpec((1, step_rows // POS_PER_ROW, LANES), lambda b, i: (b, i, 0)),
                  const((1, LANES)),
                  *[pl.BlockSpec((D_MODEL, D_ATTN), lambda b, i, g=g: (0, g))
                    for g in range(n_groups)],
                  const((CONV_WIDTH, D_CONV)),
                  const((1, D_CONV))],
        out_specs=[col_spec(wide), row_spec(wide), col_spec(N_HEADS * V_ROWS),
                   row_spec(D_ATTN), row_spec(D_CONV)],
        out_shape=[jax.ShapeDtypeStruct((bsz, wide, seq), BF16),
                   jax.ShapeDtypeStruct((bsz, seq, wide), BF16),
                   jax.ShapeDtypeStruct((bsz, N_HEADS * V_ROWS, seq), BF16),
                   jax.ShapeDtypeStruct((bsz, seq, D_ATTN), BF16),
                   jax.ShapeDtypeStruct((bsz, seq, D_CONV), BF16)],
        scratch_shapes=[pltpu.VMEM((kbd_rows, D_ATTN), F32),
                        pltpu.VMEM((8, D_CONV), F32),
                        pltpu.VMEM((TILES_PER_STEP, TILE, LANES), F32),
                        pltpu.VMEM((TILES_PER_STEP, TILE, LANES), F32),
                        pltpu.VMEM((step_rows, w_hi.shape[1]), F32),
                        pltpu.VMEM((TILES_PER_STEP, TILE, D_MODEL), BF16)],
        compiler_params=pltpu.CompilerParams(dimension_semantics=("arbitrary", "arbitrary"),
                                             vmem_limit_bytes=VMEM_LIMIT),
        name="in_proj",
    )(x, mod3, g_norm.reshape(1, D_MODEL), pos_compact, inv_freq,
      *([w_hi] * n_groups), w_conv, g_conv_out.reshape(1, D_CONV))


def _moba_kernel(n_chunks, tile_ref, chunk_ref, q_ref, k_ref, v_ref, o_ref, s_ref, smax_ref,
                 m_ref, acc_ref):
    n_past = n_chunks * (n_chunks - 1) // 2

    def own_item(t):
        return t, t

    def past_item(j):
        return tile_ref[j], chunk_ref[j]

    def own_keys(half):
        return (half + 1) * MXU_COLS

    def issue_unit(buf, tile, chunk, hd, half, own):
        n_keys = own_keys(half) if own else CHUNK
        start = pl.multiple_of(chunk * CHUNK, CHUNK)
        q_start = pl.multiple_of(tile * CHUNK + half * MXU_COLS, MXU_COLS)
        rows = slice(hd * LANES, (hd + 1) * LANES)
        cols = slice(half * MXU_COLS, (half + 1) * MXU_COLS)
        kj = k_ref[0, pl.ds(start, n_keys), rows]
        s = jnp.dot(kj, q_ref[0, rows, pl.ds(q_start, MXU_COLS)],
                    preferred_element_type=F32)
        s_ref[buf, hd, :n_keys, cols] = s.astype(BF16)
        if not own:
            smax_ref[buf, hd, :, cols] = jnp.max(s, axis=0, keepdims=True)

    def absorb_unit(buf, tile, chunk, hd, half, own):
        n_keys = own_keys(half) if own else CHUNK
        start = pl.multiple_of(chunk * CHUNK, CHUNK)
        cols = slice(half * MXU_COLS, (half + 1) * MXU_COLS)
        m = m_ref[tile, hd, :, cols]
        if own:
            diag = s_ref[buf, hd, n_keys - MXU_COLS:n_keys, cols].astype(F32)
            key = lax.broadcasted_iota(jnp.int32, diag.shape, 0)
            qry = lax.broadcasted_iota(jnp.int32, diag.shape, 1)
            diag = jnp.where(key <= qry, diag, NEG)
            blocks = [s_ref[buf, hd, b * MXU_COLS:(b + 1) * MXU_COLS, cols] for b in range(half)]
            blocks.append(diag.astype(BF16))
            smax = jnp.max(diag, axis=0, keepdims=True)
            for block in blocks[:-1]:
                smax = jnp.maximum(smax, jnp.max(block, axis=0, keepdims=True).astype(F32))
        else:
            blocks = [s_ref[buf, hd, :, cols]]
            smax = smax_ref[buf, hd, :, cols]
        m_new = jnp.maximum(m, smax).astype(BF16)
        p = [jnp.exp2(block - m_new) for block in blocks]
        p = p[0] if len(p) == 1 else jnp.concatenate(p, axis=0)
        m_new = m_new.astype(F32)
        vj = v_ref[0, hd * V_ROWS:(hd + 1) * V_ROWS, pl.ds(start, n_keys)]
        acc_ref[tile, hd, :, cols] = (jnp.exp2(m - m_new) * acc_ref[tile, hd, :, cols]
                                      + jnp.dot(vj, p, preferred_element_type=F32))
        m_ref[tile, hd, :, cols] = m_new

    def step(issue=None, absorb=None, issue_own=False, absorb_own=False):
        for half in range(CHUNK // MXU_COLS):
            for hd in range(2):
                if issue is not None:
                    issue_unit(*issue, hd, half, issue_own)
                if absorb is not None:
                    absorb_unit(*absorb, hd, half, absorb_own)

    def run_pairs(item, n_pairs, pairs_per_trip, own):
        def pair(u):
            step(issue=(1, *item(2 * u + 1)), absorb=(0, *item(2 * u)),
                 issue_own=own, absorb_own=own)
            step(issue=(0, *item(2 * u + 2)), absorb=(1, *item(2 * u + 1)),
                 issue_own=own, absorb_own=own)

        def trip(t, carry):
            for u in range(pairs_per_trip):
                pair(pairs_per_trip * t + u)
            return carry

        trips = n_pairs // pairs_per_trip
        lax.fori_loop(0, trips, trip, 0)
        for u in range(trips * pairs_per_trip, n_pairs):
            pair(u)

    m_ref[...] = jnp.full(m_ref.shape, -jnp.inf, F32)
    acc_ref[...] = jnp.zeros(acc_ref.shape, F32)
    assert n_chunks % 2 == 0 and n_past % 2 == 0
    step(issue=(0, *own_item(0)), issue_own=True)
    run_pairs(own_item, (n_chunks - 2) // 2, OWN_PAIRS_PER_TRIP, own=True)
    step(issue=(1, *own_item(n_chunks - 1)), absorb=(0, *own_item(n_chunks - 2)),
         issue_own=True, absorb_own=True)
    step(issue=(0, *past_item(0)), absorb=(1, *own_item(n_chunks - 1)), absorb_own=True)
    run_pairs(past_item, (n_past - 2) // 2, PAST_PAIRS_PER_TRIP, own=False)
    step(issue=(1, *past_item(n_past - 1)), absorb=(0, *past_item(n_past - 2)))
    step(absorb=(1, *past_item(n_past - 1)))

    def finish(tile, carry):
        outs = [acc_ref[tile, hd, :HEAD_DIM] / acc_ref[tile, hd, HEAD_DIM:HEAD_DIM + 1]
                for hd in range(2)]
        start = pl.multiple_of(tile * CHUNK, CHUNK)
        o_ref[0, pl.ds(start, CHUNK), :] = jnp.concatenate(outs, axis=0).T.astype(o_ref.dtype)
        return carry

    lax.fori_loop(0, n_chunks, finish, 0)


def _moba(q_aug, k_aug, v_aug):
    bsz, seq, wide = k_aug.shape
    pairs = wide // (2 * LANES)
    n_chunks = seq // CHUNK
    past = [(t, c) for t in range(n_chunks) for c in range(t)]
    item_tile = jnp.asarray([t for t, _ in past], jnp.int32)
    item_chunk = jnp.asarray([c for _, c in past], jnp.int32)
    return pl.pallas_call(
        functools.partial(_moba_kernel, n_chunks),
        grid_spec=pltpu.PrefetchScalarGridSpec(
            num_scalar_prefetch=2,
            grid=(bsz, pairs),
            in_specs=[pl.BlockSpec((1, 2 * LANES, seq), lambda b, p, *_: (b, p, 0)),
                      pl.BlockSpec((1, seq, 2 * LANES), lambda b, p, *_: (b, 0, p)),
                      pl.BlockSpec((1, 2 * V_ROWS, seq), lambda b, p, *_: (b, p, 0))],
            out_specs=pl.BlockSpec((1, seq, LANES), lambda b, p, *_: (b, 0, p)),
            scratch_shapes=[pltpu.VMEM((2, 2, CHUNK, CHUNK), BF16),
                            pltpu.VMEM((2, 2, 1, CHUNK), F32),
                            pltpu.VMEM((n_chunks, 2, 1, CHUNK), F32),
                            pltpu.VMEM((n_chunks, 2, V_ROWS, CHUNK), F32)]),
        out_shape=jax.ShapeDtypeStruct((bsz, seq, pairs * LANES), BF16),
        compiler_params=pltpu.CompilerParams(
            dimension_semantics=("arbitrary", "arbitrary"),
            vmem_limit_bytes=VMEM_LIMIT),
        name="moba",
    )(item_tile, item_chunk, q_aug, k_aug, v_aug)


def _out_proj_kernel(ya_ref, gz_ref, yc_ref, x_ref, mod_ref, gattn_ref, wout_ref, gfin_ref, o_ref):
    for c in range(OUT_TILE // OUT_CHUNK):
        rc = slice(c * OUT_CHUNK, (c + 1) * OUT_CHUNK)
        ya = ya_ref[0, rc, :].astype(F32)
        yn = ya * lax.rsqrt(jnp.mean(ya * ya, axis=-1, keepdims=True) + EPS) * gattn_ref[...]
        yn = (yn * gz_ref[0, rc, :].astype(F32)).astype(BF16)
        y = (jnp.dot(yn, wout_ref[0:D_ATTN, :], preferred_element_type=F32)
             + jnp.dot(yc_ref[0, rc, :], wout_ref[D_ATTN:, :], preferred_element_type=F32))
        xo = x_ref[0, rc, :] + mod_ref[0, 2:3, :] * y
        o_ref[0, rc, :] = (xo * lax.rsqrt(jnp.mean(xo * xo, axis=-1, keepdims=True) + EPS)
                           * gfin_ref[...])


def _out_proj(y_attn, gz, yc, x, mod3, g_attn_out, w_out, g_final):
    bsz, seq, _ = x.shape
    row_spec = lambda width: pl.BlockSpec((1, OUT_TILE, width), lambda b, i: (b, i, 0))
    const = lambda shape: pl.BlockSpec(shape, lambda b, i: (0,) * len(shape))
    return pl.pallas_call(
        _out_proj_kernel,
        grid=(bsz, seq // OUT_TILE),
        in_specs=[row_spec(D_ATTN), row_spec(D_ATTN), row_spec(D_CONV), row_spec(D_MODEL),
                  pl.BlockSpec((1, 3, D_MODEL), lambda b, i: (b, 0, 0)),
                  const((1, D_ATTN)), const(w_out.shape), const((1, D_MODEL))],
        out_specs=row_spec(D_MODEL),
        out_shape=jax.ShapeDtypeStruct((bsz, seq, D_MODEL), F32),
        compiler_params=pltpu.CompilerParams(dimension_semantics=("arbitrary", "arbitrary"),
                                             vmem_limit_bytes=VMEM_LIMIT),
        name="out_proj",
    )(y_attn, gz, yc, x, mod3, g_attn_out.reshape(1, D_ATTN), w_out, g_final.reshape(1, D_MODEL))


def kernel(x, c, positions, w_ada, b_ada, g_norm, w_in, w_conv, g_attn_out, g_conv_out, w_out, g_final):
    bsz, seq, _ = x.shape
    assert seq % (TILES_PER_STEP * TILE) == 0 and seq % OUT_TILE == 0 and seq // TILE <= HEAD_DIM // 2
    mod3 = _adaln(c, w_ada, b_ada).reshape(bsz, 3, D_MODEL)

    inv_freq = ROPE_THETA ** (-jnp.arange(HALF, dtype=F32) / HALF)
    inv_freq = jnp.tile(inv_freq, LANES // HALF).reshape(1, LANES)

    q_aug, k_aug, v_aug, gz, yc = _in_proj(x, mod3, g_norm, positions, inv_freq,
                                           w_in.astype(BF16), w_conv, g_conv_out)
    y_attn = _moba(q_aug, k_aug, v_aug)
    return _out_proj(y_attn, gz, yc, x, mod3, g_attn_out, w_out.astype(BF16), g_final)
```
